```python
import jax
import jax.numpy as jnp
from jax import lax
import numpy as np


D_MODEL = 1024
BATCH = 8
SEQ = 4096
DEPTH = 2

GRID_W = 64
CTX_LEN = 256

N_HEADS = 16
HEAD_DIM = 64
ATT_W = N_HEADS * HEAD_DIM
WIN_R = 8
WIN_C = 16

FOURIER_GROUPS = 4
FOURIER_W = D_MODEL // 2
FOURIER_GROUP_W = FOURIER_W // FOURIER_GROUPS

POOL_WINDOWS = (2, 4, 8, 16)
POOL_W = D_MODEL // 2
POOL_GROUP_W = POOL_W // len(POOL_WINDOWS)
POOL_OUT_GROUP = D_MODEL // len(POOL_WINDOWS)

N_BRANCHES = 3
K_OFF = 0
V_OFF = K_OFF + ATT_W
Q_OFF = V_OFF + ATT_W
F_OFF = Q_OFF + ATT_W
P_OFF = F_OFF + FOURIER_W
G_OFF = P_OFF + POOL_W
IN_W = G_OFF + N_BRANCHES * D_MODEL

N_EXPERTS = 16
EC_CAPACITY_FACTOR = 2
EXPERT_FF = 2816

RMS_EPS = 1e-6

kernel_name = "hybrid_na_fourier_pool_ecmoe_dit"


def rmsnorm(x, g):
    x32 = x.astype(jnp.float32)
    y = x32 * lax.rsqrt(jnp.mean(x32 * x32, axis=-1, keepdims=True) + RMS_EPS)
    return y.astype(x.dtype) * g


def adaln_params(cond, ada_w, ada_b):
    mod = jax.nn.silu(cond) @ ada_w + ada_b
    return jnp.split(mod, 6, axis=-1)


def modulate(x, g, shift, scale):
    return rmsnorm(x, g) * (1 + scale) + shift


def split_heads(t):
    return t.reshape(t.shape[0], t.shape[1], N_HEADS, HEAD_DIM)


def neighbourhood_attention(q, k, v, k_ctx, v_ctx, rpb):
    batch, length, heads, hd = q.shape
    rows = length // GRID_W
    win_r = min(WIN_R, rows)
    win_c = min(WIN_C, GRID_W)
    n_win = win_r * win_c
    qg = q.reshape(batch, rows, GRID_W, heads, hd)
    kg = k.reshape(batch, rows, GRID_W, heads, hd)
    vg = v.reshape(batch, rows, GRID_W, heads, hd)
    col = jnp.arange(GRID_W)
    col_start = jnp.clip(col - win_c // 2, 0, GRID_W - win_c)
    cols_idx = col_start[:, None] + jnp.arange(win_c)[None, :]
    coff = cols_idx - col[:, None] + (WIN_C - 1)
    scale = hd ** -0.5

    def row_block(r):
        rs = jnp.clip(r - win_r // 2, 0, rows - win_r)
        q_r = lax.dynamic_index_in_dim(qg, r, axis=1, keepdims=False)
        k_win = lax.dynamic_slice_in_dim(kg, rs, win_r, axis=1)[:, :, cols_idx]
        v_win = lax.dynamic_slice_in_dim(vg, rs, win_r, axis=1)[:, :, cols_idx]
        roff = rs + jnp.arange(win_r) - r + (WIN_R - 1)
        bias = rpb[:, roff[None, :, None], coff[:, None, :]]
        s_win = jnp.einsum('bqhd,brqjhd->bhqrj', q_r, k_win) * scale + bias[None]
        s_ctx = jnp.einsum('bqhd,bkhd->bhqk', q_r, k_ctx) * scale
        s = jnp.concatenate([s_win.reshape(batch, heads, GRID_W, n_win), s_ctx], axis=-1)
        p = jax.nn.softmax(s.astype(jnp.float32), axis=-1).astype(q.dtype)
        p_win = p[..., :n_win].reshape(batch, heads, GRID_W, win_r, win_c)
        return (jnp.einsum('bhqrj,brqjhd->bqhd', p_win, v_win)
                + jnp.einsum('bhqk,bkhd->bqhd', p[..., n_win:], v_ctx))

    out = lax.map(row_block, jnp.arange(rows))
    return jnp.moveaxis(out, 0, 1).reshape(batch, length, heads * hd)


def context_attention(q, k, v):
    batch, length = q.shape[0], q.shape[1]
    s = jnp.einsum('bqhd,bkhd->bhqk', q, k) * (HEAD_DIM ** -0.5)
    p = jax.nn.softmax(s.astype(jnp.float32), axis=-1).astype(q.dtype)
    return jnp.einsum('bhqk,bkhd->bqhd', p, v).reshape(batch, length, ATT_W)


def fourier_mix(u):
    batch, length, _ = u.shape
    ug = u.reshape(batch, length, FOURIER_GROUPS, FOURIER_GROUP_W).astype(jnp.float32)
    f = jnp.fft.fft2(ug, axes=(1, 3), norm='ortho').real
    return f.reshape(batch, length, FOURIER_W).astype(u.dtype)


def pooling_mix(u, w_pool, pool_scale):
    batch, length, _ = u.shape
    ug = u.reshape(batch, length, len(POOL_WINDOWS), POOL_GROUP_W)
    t = jnp.arange(length)
    outs = []
    for gi, w in enumerate(POOL_WINDOWS):
        ui = ug[:, :, gi].astype(jnp.float32)
        cs = jnp.concatenate([jnp.zeros_like(ui[:, :1]), jnp.cumsum(ui, axis=1)], axis=1)
        lo = jnp.clip(t - w // 2, 0, length)
        hi = jnp.clip(t + w - w // 2, 0, length)
        mean = (cs[:, hi] - cs[:, lo]) / (hi - lo).astype(jnp.float32)[:, None]
        outs.append(mean - ui)
    pooled = jnp.stack(outs, axis=2).astype(u.dtype)
    y = jnp.einsum('blgc,gcf->blgf', pooled, w_pool).reshape(batch, length, D_MODEL)
    return y * pool_scale


def merge_branches(att, p, w_att_o, w_fourier, w_pool, pool_scale, w_out):
    y_att = att @ w_att_o
    y_four = fourier_mix(p[..., F_OFF:P_OFF]) @ w_fourier
    y_pool = pooling_mix(p[..., P_OFF:G_OFF], w_pool, pool_scale)
    g = jax.nn.sigmoid(p[..., G_OFF:IN_W])
    merged = (g[..., :D_MODEL] * y_att
              + g[..., D_MODEL:2 * D_MODEL] * y_four
              + g[..., 2 * D_MODEL:] * y_pool)
    return merged @ w_out


def expert_choice_ffn(h, w_router, w_gate, w_up, w_down):
    batch, length, _ = h.shape
    cap = EC_CAPACITY_FACTOR * length // N_EXPERTS
    affinity = jax.nn.softmax((h @ w_router).astype(jnp.float32), axis=-1)
    gates, idx = lax.top_k(jnp.swapaxes(affinity, 1, 2), cap)
    b_idx = jnp.arange(batch)[:, None, None]
    xg = h[b_idx, idx]
    a = jnp.einsum('becd,edf->becf', xg, w_gate)
    u = jnp.einsum('becd,edf->becf', xg, w_up)
    y = jnp.einsum('becf,efd->becd', jax.nn.silu(a) * u, w_down) * gates[..., None].astype(h.dtype)
    return jnp.zeros_like(h).at[b_idx, idx].add(y)


def trunk_layer(x, ctx, c, c_ctx, ada_w, ada_b, norm1_g, norm2_g, w_in, rpb, w_att_o, w_fourier,
                w_pool, pool_scale, w_out, w_router, w_gate, w_up, w_down, update_ctx):
    sh1, sc1, g1, sh2, sc2, g2 = [m[:, None, :] for m in adaln_params(c, ada_w, ada_b)]
    sh1c, sc1c, g1c, sh2c, sc2c, g2c = adaln_params(c_ctx, ada_w, ada_b)

    h = modulate(x, norm1_g, sh1, sc1)
    hc = modulate(ctx, norm1_g, sh1c, sc1c)
    p = h @ w_in
    pc = hc @ (w_in if update_ctx else w_in[:, :Q_OFF])
    k_ctx = split_heads(pc[..., K_OFF:V_OFF])
    v_ctx = split_heads(pc[..., V_OFF:Q_OFF])
    att = neighbourhood_attention(split_heads(p[..., Q_OFF:F_OFF]), split_heads(p[..., K_OFF:V_OFF]),
                                  split_heads(p[..., V_OFF:Q_OFF]), k_ctx, v_ctx, rpb)
    x = x + g1 * merge_branches(att, p, w_att_o, w_fourier, w_pool, pool_scale, w_out)

    h2 = modulate(x, norm2_g, sh2, sc2)
    x = x + g2 * expert_choice_ffn(h2, w_router, w_gate, w_up, w_down)

    if update_ctx:
        att_c = context_attention(split_heads(pc[..., Q_OFF:F_OFF]), k_ctx, v_ctx)
        ctx = ctx + g1c * merge_branches(att_c, pc, w_att_o, w_fourier, w_pool, pool_scale, w_out)
        hc2 = modulate(ctx, norm2_g, sh2c, sc2c)
        ctx = ctx + g2c * expert_choice_ffn(hc2, w_router, w_gate, w_up, w_down)
    return x, ctx


def _normal(k, shape, scale):
    return jax.random.normal(k, shape, jnp.float32) * scale


def setup_inputs(seed: int = 0) -> dict:
    key = jax.random.key(seed)
    ks = jax.random.split(key, 21)
    d = D_MODEL
    return {
        'x': _normal(ks[0], (BATCH, SEQ, d), 1.0),
        'c': _normal(ks[1], (BATCH, d), 1.0),
        'ctx': _normal(ks[2], (BATCH, CTX_LEN, d), 1.0),
        'c_ctx': _normal(ks[3], (d,), 1.0),
        'ada_w': _normal(ks[4], (DEPTH, d, 6 * d), 0.5 * d ** -0.5),
        'ada_b': _normal(ks[5], (DEPTH, 6 * d), 0.02),
        'norm1_g': 1.0 + _normal(ks[6], (DEPTH, d), 0.02),
        'norm2_g': 1.0 + _normal(ks[7], (DEPTH, d), 0.02),
        'w_in': _normal(ks[8], (DEPTH, d, IN_W), d ** -0.5),
        'rpb': _normal(ks[9], (DEPTH, N_HEADS, 2 * WIN_R - 1, 2 * WIN_C - 1), 0.1),
        'w_att_o': _normal(ks[10], (DEPTH, ATT_W, d), ATT_W ** -0.5),
        'w_fourier': _normal(ks[11], (DEPTH, FOURIER_W, d), FOURIER_W ** -0.5),
        'w_pool': _normal(ks[12], (DEPTH, len(POOL_WINDOWS), POOL_GROUP_W, POOL_OUT_GROUP), POOL_GROUP_W ** -0.5),
        'pool_scale': 1.0 + _normal(ks[13], (DEPTH, d), 0.02),
        'w_out': _normal(ks[14], (DEPTH, d, d), d ** -0.5),
        'w_router': _normal(ks[15], (DEPTH, d, N_EXPERTS), d ** -0.5),
        'w_exp_gate': _normal(ks[16], (DEPTH, N_EXPERTS, d, EXPERT_FF), d ** -0.5),
        'w_exp_up': _normal(ks[17], (DEPTH, N_EXPERTS, d, EXPERT_FF), d ** -0.5),
        'w_exp_down': _normal(ks[18], (DEPTH, N_EXPERTS, EXPERT_FF, d), EXPERT_FF ** -0.5),
        'final_norm_g': 1.0 + _normal(ks[19], (d,), 0.02),
    }


def reference(x, c, ctx, c_ctx, ada_w, ada_b, norm1_g, norm2_g, w_in, rpb, w_att_o, w_fourier,
              w_pool, pool_scale, w_out, w_router, w_exp_gate, w_exp_up, w_exp_down, final_norm_g):
    for i in range(DEPTH):
        x, ctx = trunk_layer(x, ctx, c, c_ctx, ada_w[i], ada_b[i], norm1_g[i], norm2_g[i], w_in[i], rpb[i],
                             w_att_o[i], w_fourier[i], w_pool[i], pool_scale[i], w_out[i], w_router[i],
                             w_exp_gate[i], w_exp_up[i], w_exp_down[i], update_ctx=(i < DEPTH - 1))
    return rmsnorm(x, final_norm_g)
```

```python
import functools
import math

import jax
import jax.numpy as jnp
from jax import lax
from jax.experimental import pallas as pl
from jax.experimental.pallas import tpu as pltpu

F32 = jnp.float32
BF16 = jnp.bfloat16
I32 = jnp.int32
HIGHEST = lax.Precision.HIGHEST

D_MODEL = 1024
DEPTH = 2
GRID_W = 64
N_HEADS = 16
HEAD_DIM = 64
WIN_R = 8
WIN_C = 16
FOURIER_GROUPS = 4
FOURIER_GROUP_W = 128
FOURIER_W = 512
POOL_WINDOWS = (2, 4, 8, 16)
POOL_GROUP_W = 128
POOL_W = 512
POOL_OUT_GROUP = 256
K_OFF, V_OFF, Q_OFF, F_OFF, P_OFF, G_OFF = 0, 1024, 2048, 3072, 3584, 4096
IN_W = 7168
N_EXPERTS = 16
EC_CAPACITY_FACTOR = 2
EXPERT_FF = 2816
RMS_EPS = 1e-6

LANE = 128
HEAD_PAIR_W = 2 * HEAD_DIM
N_HEAD_PAIRS = N_HEADS // 2
MOD_ROWS = 16
NEG_BIG = -1e30
ATT_QROWS = 2
ATT_KROWS = 10
ATT_CLASSES = 5
SELECT_ITERS = 64
MiB = 1024 * 1024


def _cparams(sem, vmem_mib):
    return pltpu.CompilerParams(dimension_semantics=sem, vmem_limit_bytes=vmem_mib * MiB)


def _adaln_kernel(c_ref, w_ref, b_ref, o_ref):
    c = c_ref[...]
    s = c * jax.nn.sigmoid(c)
    o_ref[...] = jnp.dot(s, w_ref[...], precision=HIGHEST, preferred_element_type=F32) + b_ref[...]


def _adaln(cond_rows, ada_w, ada_b):
    n = ada_w.shape[1]
    tn = 1024
    return pl.pallas_call(
        _adaln_kernel,
        grid=(n // tn,),
        in_specs=[
            pl.BlockSpec((MOD_ROWS, D_MODEL), lambda j: (0, 0)),
            pl.BlockSpec((D_MODEL, tn), lambda j: (0, j)),
            pl.BlockSpec((1, tn), lambda j: (0, j)),
        ],
        out_specs=pl.BlockSpec((MOD_ROWS, tn), lambda j: (0, j)),
        out_shape=jax.ShapeDtypeStruct((MOD_ROWS, n), F32),
        compiler_params=_cparams(("arbitrary",), 32),
        name="adaln",
    )(cond_rows, ada_w, ada_b.reshape(1, n))


def _modulate(x, g, shift, scale):
    ms = jnp.mean(x * x, axis=-1, keepdims=True)
    y = x * lax.rsqrt(ms + RMS_EPS)
    return (y * g) * (1.0 + scale) + shift


def _modproj_kernel(x_ref, sh_ref, sc_ref, g_ref, w_ref, *rest, pool_tile):
    if pool_tile is None:
        o_ref, h_ref = rest
        pp_ref = None
    else:
        o_ref, pp_ref, h_ref = rest
    j = pl.program_id(1)

    @pl.when(j == 0)
    def _():
        h = _modulate(x_ref[...], g_ref[...], sh_ref[0], sc_ref[0])
        h_ref[...] = h.astype(BF16)

    acc = jnp.dot(h_ref[...], w_ref[...], preferred_element_type=F32)
    o_ref[...] = acc.astype(BF16)
    if pool_tile is not None:
        @pl.when(j == pool_tile)
        def _():
            pp_ref[...] = acc[:, P_OFF % 1024:]


def _modproj(x2, mod3, gain, w, *, seq, ctx_row, n_out, with_pool):
    rows = x2.shape[0]
    tm = min(seq, 1024)
    tn = 1024
    tiles_per_seq = seq // tm
    if ctx_row is None:
        mrow = lambda i: i // tiles_per_seq
    else:
        mrow = lambda i: ctx_row
    pool_tile = (P_OFF // tn) if with_pool else None
    out_shape = [jax.ShapeDtypeStruct((rows, n_out), BF16)]
    out_specs = [pl.BlockSpec((tm, tn), lambda i, j: (i, j))]
    if with_pool:
        out_shape.append(jax.ShapeDtypeStruct((rows, POOL_W), F32))
        out_specs.append(pl.BlockSpec((tm, POOL_W), lambda i, j: (i, 0)))
    res = pl.pallas_call(
        functools.partial(_modproj_kernel, pool_tile=pool_tile),
        grid=(rows // tm, n_out // tn),
        in_specs=[
            pl.BlockSpec((tm, D_MODEL), lambda i, j: (i, 0)),
            pl.BlockSpec((1, 1, D_MODEL), lambda i, j: (mrow(i), 0, 0)),
            pl.BlockSpec((1, 1, D_MODEL), lambda i, j: (mrow(i), 0, 1)),
            pl.BlockSpec((1, D_MODEL), lambda i, j: (0, 0)),
            pl.BlockSpec((D_MODEL, tn), lambda i, j: (0, j)),
        ],
        out_specs=out_specs,
        out_shape=out_shape,
        scratch_shapes=[pltpu.VMEM((tm, D_MODEL), BF16)],
        compiler_params=_cparams(("parallel", "arbitrary"), 48),
        name="modproj",
    )(x2, mod3, mod3, gain, w)
    return res if with_pool else (res[0], None)


def _bias_kernel(rpb_ref, o_ref, *, rows):
    h = pl.program_id(0)
    cls = pl.program_id(1)
    i_rep = jnp.where(cls < 3, cls, cls + (rows // ATT_QROWS - ATT_CLASSES))
    s = jnp.clip(ATT_QROWS * i_rep - WIN_R // 2, 0, rows - ATT_KROWS)
    qc = lax.broadcasted_iota(I32, (GRID_W, LANE), 0)
    lane = lax.broadcasted_iota(I32, (GRID_W, LANE), 1)
    kc = lane & (GRID_W - 1)
    first_half = lane < GRID_W
    cs = jnp.clip(qc - WIN_C // 2, 0, GRID_W - WIN_C)
    col_valid = (kc >= cs) & (kc < cs + WIN_C)
    dcol = kc - qc + (WIN_C - 1)
    n_coff = 2 * WIN_C - 1
    n_roff = 2 * WIN_R - 1
    for ri in range(ATT_QROWS):
        r = ATT_QROWS * i_rep + ri
        rs = jnp.clip(r - WIN_R // 2, 0, rows - WIN_R)
        for m in range(ATT_KROWS // 2):
            krow_a = s + 2 * m
            krow_b = krow_a + 1
            va = ((krow_a >= rs) & (krow_a < rs + WIN_R)).astype(I32)
            vb = ((krow_b >= rs) & (krow_b < rs + WIN_R)).astype(I32)
            base_a = (h * n_roff + jnp.clip(krow_a - r + WIN_R - 1, 0, n_roff - 1)) * n_coff
            base_b = (h * n_roff + jnp.clip(krow_b - r + WIN_R - 1, 0, n_roff - 1)) * n_coff
            acc = jnp.zeros((GRID_W, LANE), F32)
            for c in range(n_coff):
                val = jnp.where(first_half, rpb_ref[base_a + c], rpb_ref[base_b + c])
                acc = jnp.where(dcol == c, val, acc)
            row_valid = jnp.where(first_half, va, vb) > 0
            tile = jnp.where(col_valid & row_valid, acc, NEG_BIG)
            o_ref[0, 0, ri * GRID_W:(ri + 1) * GRID_W, m * LANE:(m + 1) * LANE] = tile


def _bias_table(rpb, rows):
    nq = ATT_QROWS * GRID_W
    nk = ATT_KROWS * GRID_W
    return pl.pallas_call(
        functools.partial(_bias_kernel, rows=rows),
        grid_spec=pltpu.PrefetchScalarGridSpec(
            num_scalar_prefetch=1,
            grid=(N_HEADS, ATT_CLASSES),
            in_specs=[],
            out_specs=pl.BlockSpec((1, 1, nq, nk), lambda h, c, rpb: (h, c, 0, 0)),
        ),
        out_shape=jax.ShapeDtypeStruct((N_HEADS, ATT_CLASSES, nq, nk), F32),
        compiler_params=_cparams(("arbitrary", "arbitrary"), 32),
        name="bias_table",
    )(rpb.reshape(-1))


def _softmax_pv(qm, kw, vw, kc, vc, bias):
    nt = (((1,), (1,)), ((), ()))
    sw = lax.dot_general(qm, kw, nt, preferred_element_type=F32)
    if bias is not None:
        sw = sw + bias
    sc = lax.dot_general(qm, kc, nt, preferred_element_type=F32) if kc is not None else None
    m = jnp.max(sw, axis=-1, keepdims=True)
    if sc is not None:
        m = jnp.maximum(m, jnp.max(sc, axis=-1, keepdims=True))
    pw = jnp.exp(sw - m)
    l = jnp.sum(pw, axis=-1, keepdims=True)
    o = jnp.dot(pw.astype(BF16), vw, preferred_element_type=F32)
    if sc is not None:
        pc = jnp.exp(sc - m)
        l = l + jnp.sum(pc, axis=-1, keepdims=True)
        o = o + jnp.dot(pc.astype(BF16), vc, preferred_element_type=F32)
    return o / l


def _nattn_kernel(k_ref, v_ref, q_ref, kc_ref, vc_ref, b_ref, o_ref, *, rows):
    nq = ATT_QROWS * GRID_W
    nk = ATT_KROWS * GRID_W
    n_steps = rows // ATT_QROWS
    lane = lax.broadcasted_iota(I32, (nq, HEAD_PAIR_W), 1)
    second = lane >= HEAD_DIM
    scale = HEAD_DIM ** -0.5

    def step(i, carry):
        s = jnp.clip(ATT_QROWS * i - WIN_R // 2, 0, rows - ATT_KROWS)
        kstart = pl.multiple_of(s * GRID_W, LANE)
        qstart = pl.multiple_of(i * nq, LANE)
        cls = jnp.where(i < 2, i, jnp.where(i > n_steps - 3, i - (n_steps - ATT_CLASSES), 2))
        q2 = (q_ref[0, pl.ds(qstart, nq), :].astype(F32) * scale).astype(BF16)
        kw = k_ref[0, pl.ds(kstart, nk), :]
        vw = v_ref[0, pl.ds(kstart, nk), :]
        kc = kc_ref[0]
        vc = vc_ref[0]
        outs = []
        for hh in range(2):
            head_lanes = second if hh else jnp.logical_not(second)
            qm = jnp.where(head_lanes, q2, jnp.zeros_like(q2))
            outs.append(_softmax_pv(qm, kw, vw, kc, vc, b_ref[hh, cls]))
        o = jnp.where(second, outs[1], outs[0])
        o_ref[0, pl.ds(qstart, nq), :] = o.astype(BF16)
        return carry

    lax.fori_loop(0, n_steps, step, 0)


def _nattn(pb3, pcb3, bias):
    b, seq, _ = pb3.shape
    lc = pcb3.shape[1]
    rows = seq // GRID_W
    nq = ATT_QROWS * GRID_W
    nk = ATT_KROWS * GRID_W
    kblk, vblk, qblk = K_OFF // LANE, V_OFF // LANE, Q_OFF // LANE
    return pl.pallas_call(
        functools.partial(_nattn_kernel, rows=rows),
        grid=(N_HEAD_PAIRS, b),
        in_specs=[
            pl.BlockSpec((1, seq, HEAD_PAIR_W), lambda hp, bi: (bi, 0, kblk + hp)),
            pl.BlockSpec((1, seq, HEAD_PAIR_W), lambda hp, bi: (bi, 0, vblk + hp)),
            pl.BlockSpec((1, seq, HEAD_PAIR_W), lambda hp, bi: (bi, 0, qblk + hp)),
            pl.BlockSpec((1, lc, HEAD_PAIR_W), lambda hp, bi: (bi, 0, kblk + hp)),
            pl.BlockSpec((1, lc, HEAD_PAIR_W), lambda hp, bi: (bi, 0, vblk + hp)),
            pl.BlockSpec((2, ATT_CLASSES, nq, nk), lambda hp, bi: (hp, 0, 0, 0)),
        ],
        out_specs=pl.BlockSpec((1, seq, HEAD_PAIR_W), lambda hp, bi: (bi, 0, hp)),
        out_shape=jax.ShapeDtypeStruct((b, seq, N_HEADS * HEAD_DIM), BF16),
        compiler_params=_cparams(("parallel", "parallel"), 40),
        name="nattn",
    )(pb3, pb3, pb3, pcb3, pcb3, bias)


def _cattn_kernel(k_ref, v_ref, q_ref, o_ref):
    lc = q_ref.shape[1]
    lane = lax.broadcasted_iota(I32, (lc, HEAD_PAIR_W), 1)
    second = lane >= HEAD_DIM
    q2 = (q_ref[0].astype(F32) * (HEAD_DIM ** -0.5)).astype(BF16)
    k = k_ref[0]
    v = v_ref[0]
    outs = []
    for hh in range(2):
        head_lanes = second if hh else jnp.logical_not(second)
        qm = jnp.where(head_lanes, q2, jnp.zeros_like(q2))
        outs.append(_softmax_pv(qm, k, v, None, None, None))
    o_ref[0] = jnp.where(second, outs[1], outs[0]).astype(BF16)


def _cattn(pcb3):
    b, lc, _ = pcb3.shape
    kblk, vblk, qblk = K_OFF // LANE, V_OFF // LANE, Q_OFF // LANE
    return pl.pallas_call(
        _cattn_kernel,
        grid=(b, N_HEAD_PAIRS),
        in_specs=[
            pl.BlockSpec((1, lc, HEAD_PAIR_W), lambda bi, hp: (bi, 0, kblk + hp)),
            pl.BlockSpec((1, lc, HEAD_PAIR_W), lambda bi, hp: (bi, 0, vblk + hp)),
            pl.BlockSpec((1, lc, HEAD_PAIR_W), lambda bi, hp: (bi, 0, qblk + hp)),
        ],
        out_specs=pl.BlockSpec((1, lc, HEAD_PAIR_W), lambda bi, hp: (bi, 0, hp)),
        out_shape=jax.ShapeDtypeStruct((b, lc, N_HEADS * HEAD_DIM), BF16),
        compiler_params=_cparams(("parallel", "parallel"), 32),
        name="cattn",
    )(pcb3, pcb3, pcb3)


def _dft_kernel(c_ref, s_ref, *, n):
    tk = c_ref.shape[0]
    k = pl.program_id(0) * tk + lax.broadcasted_iota(I32, (tk, n), 0)
    t = lax.broadcasted_iota(I32, (tk, n), 1)
    m = (k * t) & (n - 1)
    ang = m.astype(F32) * (2.0 * math.pi / n)
    c_ref[...] = jnp.cos(ang).astype(BF16)
    s_ref[...] = jnp.sin(ang).astype(BF16)


def _dft_mats(n):
    tk = min(n, 256)
    return pl.pallas_call(
        functools.partial(_dft_kernel, n=n),
        grid=(n // tk,),
        in_specs=[],
        out_specs=[pl.BlockSpec((tk, n), lambda i: (i, 0))] * 2,
        out_shape=[jax.ShapeDtypeStruct((n, n), BF16)] * 2,
        compiler_params=_cparams(("parallel",), 48),
        name=f"dft_mats_{n}",
    )()


def _fourier_kernel(u_ref, cc_ref, sc_ref, cl_ref, sl_ref, o_ref, a_ref, b_ref, *, seq):
    @pl.when(pl.program_id(1) == 0)
    def _():
        for g in range(FOURIER_GROUPS):
            sl = slice(g * FOURIER_GROUP_W, (g + 1) * FOURIER_GROUP_W)
            ug = u_ref[0, :, sl]
            a_ref[:, sl] = jnp.dot(ug, cc_ref[...], preferred_element_type=F32).astype(BF16)
            b_ref[:, sl] = jnp.dot(ug, sc_ref[...], preferred_element_type=F32).astype(BF16)

    y = (jnp.dot(cl_ref[...], a_ref[...], preferred_element_type=F32)
         - jnp.dot(sl_ref[...], b_ref[...], preferred_element_type=F32))
    o_ref[0] = (y * (1.0 / math.sqrt(seq * FOURIER_GROUP_W))).astype(BF16)


def _fourier(pb3, cc, sc, cl, sl):
    b, seq, _ = pb3.shape
    tk = min(seq, 512)
    return pl.pallas_call(
        functools.partial(_fourier_kernel, seq=seq),
        grid=(b, seq // tk),
        in_specs=[
            pl.BlockSpec((1, seq, FOURIER_W), lambda bi, k: (bi, 0, F_OFF // FOURIER_W)),
            pl.BlockSpec((FOURIER_GROUP_W, FOURIER_GROUP_W), lambda bi, k: (0, 0)),
            pl.BlockSpec((FOURIER_GROUP_W, FOURIER_GROUP_W), lambda bi, k: (0, 0)),
            pl.BlockSpec((tk, seq), lambda bi, k: (k, 0)),
            pl.BlockSpec((tk, seq), lambda bi, k: (k, 0)),
        ],
        out_specs=pl.BlockSpec((1, tk, FOURIER_W), lambda bi, k: (bi, k, 0)),
        out_shape=jax.ShapeDtypeStruct((b, seq, FOURIER_W), BF16),
        scratch_shapes=[pltpu.VMEM((seq, FOURIER_W), BF16), pltpu.VMEM((seq, FOURIER_W), BF16)],
        compiler_params=_cparams(("parallel", "arbitrary"), 48),
        name="fourier",
    )(pb3, cc, sc, cl, sl)


POOL_PAD = 8


def _pool_kernel(u_ref, o_ref, pad_ref, *, seq):
    t = lax.broadcasted_iota(I32, (seq, POOL_GROUP_W), 0)
    zeros = jnp.zeros((POOL_PAD, POOL_GROUP_W), F32)
    pad_ref[0:POOL_PAD, :] = zeros
    pad_ref[seq + POOL_PAD:seq + 2 * POOL_PAD, :] = zeros
    pad_ref[POOL_PAD:seq + POOL_PAD, :] = u_ref[0]
    for g, w in enumerate(POOL_WINDOWS):
        @pl.when(pl.program_id(1) == g)
        def _(w=w):
            acc = None
            for d in range(-(w // 2), w - w // 2):
                term = pad_ref[pl.ds(POOL_PAD + d, seq), :]
                acc = term if acc is None else acc + term
            cnt = (jnp.minimum(t + (w - w // 2), seq) - jnp.maximum(t - w // 2, 0)).astype(F32)
            o_ref[0] = (acc / cnt - u_ref[0]).astype(BF16)


def _pool(pp3):
    b, seq, _ = pp3.shape
    spec = pl.BlockSpec((1, seq, POOL_GROUP_W), lambda bi, g: (bi, 0, g))
    return pl.pallas_call(
        functools.partial(_pool_kernel, seq=seq),
        grid=(b, len(POOL_WINDOWS)),
        in_specs=[spec],
        out_specs=spec,
        out_shape=jax.ShapeDtypeStruct((b, seq, POOL_W), BF16),
        scratch_shapes=[pltpu.VMEM((seq + 2 * POOL_PAD, POOL_GROUP_W), F32)],
        compiler_params=_cparams(("parallel", "parallel"), 32),
        name="pool",
    )(pp3)


def _merge_kernel(att_ref, four_ref, pool_ref, ga_ref, gf_ref, gp_ref, x_ref, g1_ref,
                  wao_ref, wf_ref, wp_ref, ps_ref, wo_ref, o_ref):
    y_att = jnp.dot(att_ref[0], wao_ref[...], preferred_element_type=F32)
    y_four = jnp.dot(four_ref[0], wf_ref[...], preferred_element_type=F32)
    pooled = pool_ref[0]
    y_pool = jnp.concatenate(
        [jnp.dot(pooled[:, g * POOL_GROUP_W:(g + 1) * POOL_GROUP_W], wp_ref[g], preferred_element_type=F32)
         for g in range(len(POOL_WINDOWS))], axis=-1) * ps_ref[...]
    merged = (jax.nn.sigmoid(ga_ref[0].astype(F32)) * y_att
              + jax.nn.sigmoid(gf_ref[0].astype(F32)) * y_four
              + jax.nn.sigmoid(gp_ref[0].astype(F32)) * y_pool)
    y = jnp.dot(merged.astype(BF16), wo_ref[...], preferred_element_type=F32)
    o_ref[0] = x_ref[0] + g1_ref[0] * y


def _merge(att, four, pooled, pb3, x3, mod3, wao, wf, wp, ps, wo, *, ctx_row):
    b, seq, _ = x3.shape
    tm = min(seq, 512)
    gblk = G_OFF // D_MODEL
    mrow = (lambda bi: bi) if ctx_row is None else (lambda bi: ctx_row)
    tok = lambda w: pl.BlockSpec((1, tm, w), lambda bi, i: (bi, i, 0))
    gate = lambda k: pl.BlockSpec((1, tm, D_MODEL), lambda bi, i: (bi, i, gblk + k))
    full = lambda a: pl.BlockSpec(a.shape, lambda bi, i: (0,) * a.ndim)
    return pl.pallas_call(
        _merge_kernel,
        grid=(b, seq // tm),
        in_specs=[
            tok(D_MODEL), tok(FOURIER_W), tok(POOL_W), gate(0), gate(1), gate(2), tok(D_MODEL),
            pl.BlockSpec((1, 1, D_MODEL), lambda bi, i: (mrow(bi), 0, 2)),
            full(wao), full(wf), full(wp), full(ps), full(wo),
        ],
        out_specs=tok(D_MODEL),
        out_shape=jax.ShapeDtypeStruct((b, seq, D_MODEL), F32),
        compiler_params=_cparams(("parallel", "parallel"), 48),
        name="merge",
    )(att, four, pooled, pb3, pb3, pb3, x3, mod3, wao, wf, wp, ps, wo)


def _router_kernel(x_ref, sh_ref, sc_ref, g_ref, wr_ref, h_ref, lg_ref):
    h = _modulate(x_ref[0], g_ref[...], sh_ref[0], sc_ref[0])
    h_ref[0] = h.astype(BF16)
    lg_ref[0] = lax.dot_general(wr_ref[...], h, (((1,), (1,)), ((), ())),
                                precision=HIGHEST, preferred_element_type=F32)


def _router(x3, mod3, gain, wr_t, *, ctx_row):
    b, seq, _ = x3.shape
    tm = min(seq, 512)
    mrow = (lambda bi: bi) if ctx_row is None else (lambda bi: ctx_row)
    return pl.pallas_call(
        _router_kernel,
        grid=(b, seq // tm),
        in_specs=[
            pl.BlockSpec((1, tm, D_MODEL), lambda bi, i: (bi, i, 0)),
            pl.BlockSpec((1, 1, D_MODEL), lambda bi, i: (mrow(bi), 0, 3)),
            pl.BlockSpec((1, 1, D_MODEL), lambda bi, i: (mrow(bi), 0, 4)),
            pl.BlockSpec((1, D_MODEL), lambda bi, i: (0, 0)),
            pl.BlockSpec((N_EXPERTS, D_MODEL), lambda bi, i: (0, 0)),
        ],
        out_specs=[
            pl.BlockSpec((1, tm, D_MODEL), lambda bi, i: (bi, i, 0)),
            pl.BlockSpec((1, N_EXPERTS, tm), lambda bi, i: (bi, 0, i)),
        ],
        out_shape=[
            jax.ShapeDtypeStruct((b, seq, D_MODEL), BF16),
            jax.ShapeDtypeStruct((b, N_EXPERTS, seq), F32),
        ],
        compiler_params=_cparams(("parallel", "parallel"), 32),
        name="router",
    )(x3, mod3, mod3, gain, wr_t)


def _exclusive_prefix(mask, tri):
    e, seq = mask.shape
    ones = jnp.where(mask, 1.0, 0.0)
    offs = jnp.zeros((e, 1), F32)
    pieces = []
    for k in range(seq // LANE):
        blk = ones[:, k * LANE:(k + 1) * LANE]
        local = jnp.dot(blk.astype(BF16), tri, preferred_element_type=F32)
        pieces.append(local + offs)
        offs = offs + jnp.sum(blk, axis=1, keepdims=True)
    return jnp.concatenate(pieces, axis=1)


def _select_kernel(lg_ref, slot_ref, aff_ref, *, cap):
    z = lg_ref[0]
    z = z - jnp.max(z, axis=0, keepdims=True)
    ez = jnp.exp(z)
    a = ez / jnp.sum(ez, axis=0, keepdims=True)
    aff_ref[0] = a
    capf = float(cap)

    def bisect(_, lohi):
        lo, hi = lohi
        mid = (lo + hi) * 0.5
        cnt = jnp.sum(jnp.where(a >= mid, 1.0, 0.0), axis=1, keepdims=True)
        ge = cnt >= capf
        return jnp.where(ge, mid, lo), jnp.where(ge, hi, mid)

    e = a.shape[0]
    lo, hi = lax.fori_loop(0, SELECT_ITERS, bisect,
                           (jnp.zeros((e, 1), F32), jnp.full((e, 1), 2.0, F32)))
    r_i = lax.broadcasted_iota(I32, (LANE, LANE), 0)
    c_i = lax.broadcasted_iota(I32, (LANE, LANE), 1)
    tri = jnp.where(r_i < c_i, 1.0, 0.0).astype(BF16)
    above = a >= hi
    n_above = jnp.sum(jnp.where(above, 1.0, 0.0), axis=1, keepdims=True)
    tied = (a >= lo) & jnp.logical_not(above)
    tie_rank = _exclusive_prefix(tied, tri)
    sel = above | (tied & (tie_rank < capf - n_above))
    pos = _exclusive_prefix(sel, tri)
    slot_ref[0] = jnp.where(sel, pos, -1.0).astype(I32)


def _select(logits_t, cap):
    b, e, seq = logits_t.shape
    spec = pl.BlockSpec((1, e, seq), lambda bi: (bi, 0, 0))
    return pl.pallas_call(
        functools.partial(_select_kernel, cap=cap),
        grid=(b,),
        in_specs=[spec],
        out_specs=[spec, spec],
        out_shape=[jax.ShapeDtypeStruct((b, e, seq), I32), jax.ShapeDtypeStruct((b, e, seq), F32)],
        compiler_params=_cparams(("parallel",), 32),
        name="select",
    )(logits_t)


def _gather_kernel(h_ref, slot_ref, aff_ref, xg_ref, gate_ref, *, cap, chunk):
    seq = h_ref.shape[1]
    j = lax.broadcasted_iota(I32, (cap, chunk), 0)
    xg = jnp.zeros((cap, D_MODEL), F32)
    gate = jnp.zeros((cap, 1), F32)
    for c in range(seq // chunk):
        sl = slice(c * chunk, (c + 1) * chunk)
        hit = slot_ref[0, 0, :, sl] == j
        onehot = jnp.where(hit, 1.0, 0.0).astype(BF16)
        xg = xg + jnp.dot(onehot, h_ref[0, sl, :], preferred_element_type=F32)
        gate = gate + jnp.sum(jnp.where(hit, aff_ref[0, 0, :, sl], 0.0), axis=1, keepdims=True)
    xg_ref[0, 0] = xg.astype(BF16)
    gate_ref[0, 0] = gate


def _gather(h2, slots4, aff4, cap):
    b, seq, _ = h2.shape
    e = slots4.shape[1]
    chunk = min(seq, 1024)
    row = pl.BlockSpec((1, 1, 1, seq), lambda bi, ei: (bi, ei, 0, 0))
    return pl.pallas_call(
        functools.partial(_gather_kernel, cap=cap, chunk=chunk),
        grid=(b, e),
        in_specs=[pl.BlockSpec((1, seq, D_MODEL), lambda bi, ei: (bi, 0, 0)), row, row],
        out_specs=[
            pl.BlockSpec((1, 1, cap, D_MODEL), lambda bi, ei: (ei, bi, 0, 0)),
            pl.BlockSpec((1, 1, cap, 1), lambda bi, ei: (ei, bi, 0, 0)),
        ],
        out_shape=[
            jax.ShapeDtypeStruct((e, b, cap, D_MODEL), BF16),
            jax.ShapeDtypeStruct((e, b, cap, 1), F32),
        ],
        compiler_params=_cparams(("parallel", "arbitrary"), 48),
        name="gather",
    )(h2, slots4, aff4)


FF_CHUNK = EXPERT_FF // 2


def _ffn_kernel(x_ref, gate_ref, wg_ref, wu_ref, wd_ref, o_ref):
    x = x_ref[0]
    y = jnp.zeros((x.shape[0], D_MODEL), F32)
    for c in range(EXPERT_FF // FF_CHUNK):
        sl = slice(c * FF_CHUNK, (c + 1) * FF_CHUNK)
        a = jnp.dot(x, wg_ref[0, :, sl], preferred_element_type=F32)
        u = jnp.dot(x, wu_ref[0, :, sl], preferred_element_type=F32)
        hmid = (a * jax.nn.sigmoid(a) * u).astype(BF16)
        y = y + jnp.dot(hmid, wd_ref[0, sl, :], preferred_element_type=F32)
    o_ref[0] = (y * gate_ref[0]).astype(BF16)


def _ffn(xg3, gate3, wg, wu, wd):
    e, m, _ = xg3.shape
    tm = min(m, 512)
    return pl.pallas_call(
        _ffn_kernel,
        grid=(e, m // tm),
        in_specs=[
            pl.BlockSpec((1, tm, D_MODEL), lambda ei, i: (ei, i, 0)),
            pl.BlockSpec((1, tm, 1), lambda ei, i: (ei, i, 0)),
            pl.BlockSpec((1, D_MODEL, EXPERT_FF), lambda ei, i: (ei, 0, 0)),
            pl.BlockSpec((1, D_MODEL, EXPERT_FF), lambda ei, i: (ei, 0, 0)),
            pl.BlockSpec((1, EXPERT_FF, D_MODEL), lambda ei, i: (ei, 0, 0)),
        ],
        out_specs=pl.BlockSpec((1, tm, D_MODEL), lambda ei, i: (ei, i, 0)),
        out_shape=jax.ShapeDtypeStruct((e, m, D_MODEL), BF16),
        compiler_params=_cparams(("parallel", "arbitrary"), 58),
        name="ffn",
    )(xg3, gate3, wg, wu, wd)


COMBINE_EXPERTS = 4


def _combine_kernel(slot_ref, gy_ref, x_ref, g2_ref, gf_ref, o_ref, acc_ref, *, cap, final):
    ec = pl.program_id(2)
    tt = x_ref.shape[1]

    @pl.when(ec == 0)
    def _():
        acc_ref[...] = jnp.zeros_like(acc_ref)

    j = lax.broadcasted_iota(I32, (cap, tt), 0)
    onehot = jnp.concatenate(
        [jnp.where(slot_ref[0, k] == j, 1.0, 0.0).astype(BF16) for k in range(COMBINE_EXPERTS)], axis=0)
    gy = gy_ref[:, 0].reshape(COMBINE_EXPERTS * cap, D_MODEL)
    acc_ref[...] += lax.dot_general(onehot, gy, (((0,), (0,)), ((), ())), preferred_element_type=F32)

    @pl.when(ec == pl.num_programs(2) - 1)
    def _():
        xn = x_ref[0] + g2_ref[0] * acc_ref[...]
        if final:
            ms = jnp.mean(xn * xn, axis=-1, keepdims=True)
            xn = (xn * lax.rsqrt(ms + RMS_EPS)) * gf_ref[...]
        o_ref[0] = xn


def _combine(slots4, gy4, x3, mod3, gfinal, *, cap, ctx_row, final):
    b, seq, _ = x3.shape
    e = slots4.shape[1]
    tt = min(seq, 1024)
    mrow = (lambda bi: bi) if ctx_row is None else (lambda bi: ctx_row)
    return pl.pallas_call(
        functools.partial(_combine_kernel, cap=cap, final=final),
        grid=(b, seq // tt, e // COMBINE_EXPERTS),
        in_specs=[
            pl.BlockSpec((1, COMBINE_EXPERTS, 1, tt), lambda bi, i, ec: (bi, ec, 0, i)),
            pl.BlockSpec((COMBINE_EXPERTS, 1, cap, D_MODEL), lambda bi, i, ec: (ec, bi, 0, 0)),
            pl.BlockSpec((1, tt, D_MODEL), lambda bi, i, ec: (bi, i, 0)),
            pl.BlockSpec((1, 1, D_MODEL), lambda bi, i, ec: (mrow(bi), 0, 5)),
            pl.BlockSpec((1, D_MODEL), lambda bi, i, ec: (0, 0)),
        ],
        out_specs=pl.BlockSpec((1, tt, D_MODEL), lambda bi, i, ec: (bi, i, 0)),
        out_shape=jax.ShapeDtypeStruct((b, seq, D_MODEL), F32),
        scratch_shapes=[pltpu.VMEM((tt, D_MODEL), F32)],
        compiler_params=_cparams(("parallel", "parallel", "arbitrary"), 48),
        name="combine",
    )(slots4, gy4, x3, mod3, gfinal)


def _moe(x3, mod3, gain2, wr_t, wg, wu, wd, gfinal, *, ctx_row, final):
    b, seq, _ = x3.shape
    cap = EC_CAPACITY_FACTOR * seq // N_EXPERTS
    h2, logits_t = _router(x3, mod3, gain2, wr_t, ctx_row=ctx_row)
    slots, aff = _select(logits_t, cap)
    slots4 = slots.reshape(b, N_EXPERTS, 1, seq)
    aff4 = aff.reshape(b, N_EXPERTS, 1, seq)
    xg, gate = _gather(h2, slots4, aff4, cap)
    gy = _ffn(xg.reshape(N_EXPERTS, b * cap, D_MODEL), gate.reshape(N_EXPERTS, b * cap, 1), wg, wu, wd)
    gy4 = gy.reshape(N_EXPERTS, b, cap, D_MODEL)
    return _combine(slots4, gy4, x3, mod3, gfinal, cap=cap, ctx_row=ctx_row, final=final)


def kernel(x, c, ctx, c_ctx, ada_w, ada_b, norm1_g, norm2_g, w_in, rpb, w_att_o, w_fourier, w_pool,
           pool_scale, w_out, w_router, w_exp_gate, w_exp_up, w_exp_down, final_norm_g):
    b, seq, d = x.shape
    lc = ctx.shape[1]
    assert d == D_MODEL and seq % (GRID_W * ATT_QROWS) == 0 and b + 1 <= MOD_ROWS
    rows = seq // GRID_W
    ctx_row = b

    cond = jnp.concatenate([c, c_ctx[None, :], jnp.zeros((MOD_ROWS - b - 1, d), F32)], axis=0)
    cl, sl = _dft_mats(seq)
    clc, slc = _dft_mats(lc)
    cc, sc = _dft_mats(FOURIER_GROUP_W)
    gfinal = final_norm_g.reshape(1, d)

    for i in range(DEPTH):
        update_ctx = i < DEPTH - 1
        mod3 = _adaln(cond, ada_w[i], ada_b[i]).reshape(MOD_ROWS, 1, 6 * d)
        g1n = norm1_g[i].reshape(1, d)
        g2n = norm2_g[i].reshape(1, d)
        w_in_b = w_in[i].astype(BF16)
        wao = w_att_o[i].astype(BF16)
        wf = w_fourier[i].astype(BF16)
        wp = w_pool[i].astype(BF16)
        ps = pool_scale[i].reshape(1, d)
        wo = w_out[i].astype(BF16)
        wr_t = w_router[i].T
        wg = w_exp_gate[i].astype(BF16)
        wu = w_exp_up[i].astype(BF16)
        wd = w_exp_down[i].astype(BF16)
        bias = _bias_table(rpb[i], rows)

        pb, pp = _modproj(x.reshape(b * seq, d), mod3, g1n, w_in_b, seq=seq, ctx_row=None,
                          n_out=IN_W, with_pool=True)
        n_ctx = IN_W if update_ctx else Q_OFF
        pcb, pcp = _modproj(ctx.reshape(b * lc, d), mod3, g1n, w_in_b[:, :n_ctx], seq=lc, ctx_row=ctx_row,
                            n_out=n_ctx, with_pool=update_ctx)
        pb3 = pb.reshape(b, seq, IN_W)
        pcb3 = pcb.reshape(b, lc, n_ctx)

        att = _nattn(pb3, pcb3, bias)
        four = _fourier(pb3, cc, sc, cl, sl)
        pooled = _pool(pp.reshape(b, seq, POOL_W))
        x = _merge(att, four, pooled, pb3, x, mod3, wao, wf, wp, ps, wo, ctx_row=None)
        x = _moe(x, mod3, g2n, wr_t, wg, wu, wd, gfinal, ctx_row=None, final=not update_ctx)

        if update_ctx:
            att_c = _cattn(pcb3)
            four_c = _fourier(pcb3, cc, sc, clc, slc)
            pooled_c = _pool(pcp.reshape(b, lc, POOL_W))
            ctx = _merge(att_c, four_c, pooled_c, pcb3, ctx, mod3, wao, wf, wp, ps, wo, ctx_row=ctx_row)
            ctx = _moe(ctx, mod3, g2n, wr_t, wg, wu, wd, gfinal, ctx_row=ctx_row, final=False)
    return x
```

```python
import functools
import math

import jax
import jax.numpy as jnp
from jax import lax
from jax.experimental import pallas as pl
from jax.experimental.pallas import tpu as pltpu

F32 = jnp.float32
BF16 = jnp.bfloat16
I32 = jnp.int32
HIGHEST = lax.Precision.HIGHEST

D_MODEL = 1024
DEPTH = 2
GRID_W = 64
N_HEADS = 16
HEAD_DIM = 64
WIN_R = 8
WIN_C = 16
FOURIER_GROUPS = 4
FOURIER_GROUP_W = 128
FOURIER_W = 512
POOL_WINDOWS = (2, 4, 8, 16)
POOL_GROUP_W = 128
POOL_W = 512
POOL_OUT_GROUP = 256
K_OFF, V_OFF, Q_OFF, F_OFF, P_OFF, G_OFF = 0, 1024, 2048, 3072, 3584, 4096
IN_W = 7168
N_EXPERTS = 16
EC_CAPACITY_FACTOR = 2
EXPERT_FF = 2816
RMS_EPS = 1e-6

LANE = 128
HEAD_PAIR_W = 2 * HEAD_DIM
N_HEAD_PAIRS = N_HEADS // 2
MOD_ROWS = 16
NEG_BIG = -1e30
ATT_QROWS = 2
ATT_KROWS = 10
ATT_CLASSES = 5
SELECT_ITERS = 64
MiB = 1024 * 1024


def _cparams(sem, vmem_mib):
    return pltpu.CompilerParams(dimension_semantics=sem, vmem_limit_bytes=vmem_mib * MiB)


def _adaln_kernel(c_ref, w_ref, b_ref, o_ref):
    c = c_ref[...]
    s = c * jax.nn.sigmoid(c)
    o_ref[...] = jnp.dot(s, w_ref[...], precision=HIGHEST, preferred_element_type=F32) + b_ref[...]


def _adaln(cond_rows, ada_w, ada_b):
    n = ada_w.shape[1]
    tn = 1024
    return pl.pallas_call(
        _adaln_kernel,
        grid=(n // tn,),
        in_specs=[
            pl.BlockSpec((MOD_ROWS, D_MODEL), lambda j: (0, 0)),
            pl.BlockSpec((D_MODEL, tn), lambda j: (0, j)),
            pl.BlockSpec((1, tn), lambda j: (0, j)),
        ],
        out_specs=pl.BlockSpec((MOD_ROWS, tn), lambda j: (0, j)),
        out_shape=jax.ShapeDtypeStruct((MOD_ROWS, n), F32),
        compiler_params=_cparams(("arbitrary",), 32),
        name="adaln",
    )(cond_rows, ada_w, ada_b.reshape(1, n))


def _modulate(x, g, shift, scale):
    ms = jnp.mean(x * x, axis=-1, keepdims=True)
    y = x * lax.rsqrt(ms + RMS_EPS)
    return (y * g) * (1.0 + scale) + shift


def _modproj_kernel(x_ref, sh_ref, sc_ref, g_ref, w_ref, *rest, pool_tile):
    if pool_tile is None:
        o_ref, h_ref = rest
        pp_ref = None
    else:
        o_ref, pp_ref, h_ref = rest
    j = pl.program_id(1)

    @pl.when(j == 0)
    def _():
        h = _modulate(x_ref[...], g_ref[...], sh_ref[0], sc_ref[0])
        h_ref[...] = h.astype(BF16)

    acc = jnp.dot(h_ref[...], w_ref[...], preferred_element_type=F32)
    o_ref[...] = acc.astype(BF16)
    if pool_tile is not None:
        @pl.when(j == pool_tile)
        def _():
            pp_ref[...] = acc[:, P_OFF % 1024:]


def _modproj(x2, mod3, gain, w, *, seq, ctx_row, n_out, with_pool):
    rows = x2.shape[0]
    tm = min(seq, 1024)
    tn = 1024
    tiles_per_seq = seq // tm
    if ctx_row is None:
        mrow = lambda i: i // tiles_per_seq
    else:
        mrow = lambda i: ctx_row
    pool_tile = (P_OFF // tn) if with_pool else None
    out_shape = [jax.ShapeDtypeStruct((rows, n_out), BF16)]
    out_specs = [pl.BlockSpec((tm, tn), lambda i, j: (i, j))]
    if with_pool:
        out_shape.append(jax.ShapeDtypeStruct((rows, POOL_W), F32))
        out_specs.append(pl.BlockSpec((tm, POOL_W), lambda i, j: (i, 0)))
    res = pl.pallas_call(
        functools.partial(_modproj_kernel, pool_tile=pool_tile),
        grid=(rows // tm, n_out // tn),
        in_specs=[
            pl.BlockSpec((tm, D_MODEL), lambda i, j: (i, 0)),
            pl.BlockSpec((1, 1, D_MODEL), lambda i, j: (mrow(i), 0, 0)),
            pl.BlockSpec((1, 1, D_MODEL), lambda i, j: (mrow(i), 0, 1)),
            pl.BlockSpec((1, D_MODEL), lambda i, j: (0, 0)),
            pl.BlockSpec((D_MODEL, tn), lambda i, j: (0, j)),
        ],
        out_specs=out_specs,
        out_shape=out_shape,
        scratch_shapes=[pltpu.VMEM((tm, D_MODEL), BF16)],
        compiler_params=_cparams(("parallel", "arbitrary"), 48),
        name="modproj",
    )(x2, mod3, mod3, gain, w)
    return res if with_pool else (res[0], None)


def _bias_kernel(rpb_ref, o_ref, *, rows):
    h = pl.program_id(0)
    cls = pl.program_id(1)
    i_rep = jnp.where(cls < 3, cls, cls + (rows // ATT_QROWS - ATT_CLASSES))
    s = jnp.clip(ATT_QROWS * i_rep - WIN_R // 2, 0, rows - ATT_KROWS)
    qc = lax.broadcasted_iota(I32, (GRID_W, LANE), 0)
    lane = lax.broadcasted_iota(I32, (GRID_W, LANE), 1)
    kc = lane & (GRID_W - 1)
    first_half = lane < GRID_W
    cs = jnp.clip(qc - WIN_C // 2, 0, GRID_W - WIN_C)
    col_valid = (kc >= cs) & (kc < cs + WIN_C)
    dcol = kc - qc + (WIN_C - 1)
    n_coff = 2 * WIN_C - 1
    n_roff = 2 * WIN_R - 1
    for ri in range(ATT_QROWS):
        r = ATT_QROWS * i_rep + ri
        rs = jnp.clip(r - WIN_R // 2, 0, rows - WIN_R)
        for m in range(ATT_KROWS // 2):
            krow_a = s + 2 * m
            krow_b = krow_a + 1
            va = ((krow_a >= rs) & (krow_a < rs + WIN_R)).astype(I32)
            vb = ((krow_b >= rs) & (krow_b < rs + WIN_R)).astype(I32)
            base_a = (h * n_roff + jnp.clip(krow_a - r + WIN_R - 1, 0, n_roff - 1)) * n_coff
            base_b = (h * n_roff + jnp.clip(krow_b - r + WIN_R - 1, 0, n_roff - 1)) * n_coff
            acc = jnp.zeros((GRID_W, LANE), F32)
            for c in range(n_coff):
                val = jnp.where(first_half, rpb_ref[base_a + c], rpb_ref[base_b + c])
                acc = jnp.where(dcol == c, val, acc)
            row_valid = jnp.where(first_half, va, vb) > 0
            tile = jnp.where(col_valid & row_valid, acc, NEG_BIG)
            o_ref[0, 0, ri * GRID_W:(ri + 1) * GRID_W, m * LANE:(m + 1) * LANE] = tile


def _bias_table(rpb, rows):
    nq = ATT_QROWS * GRID_W
    nk = ATT_KROWS * GRID_W
    return pl.pallas_call(
        functools.partial(_bias_kernel, rows=rows),
        grid_spec=pltpu.PrefetchScalarGridSpec(
            num_scalar_prefetch=1,
            grid=(N_HEADS, ATT_CLASSES),
            in_specs=[],
            out_specs=pl.BlockSpec((1, 1, nq, nk), lambda h, c, rpb: (h, c, 0, 0)),
        ),
        out_shape=jax.ShapeDtypeStruct((N_HEADS, ATT_CLASSES, nq, nk), F32),
        compiler_params=_cparams(("arbitrary", "arbitrary"), 32),
        name="bias_table",
    )(rpb.reshape(-1))


_NT = (((1,), (1,)), ((), ()))


def _scores(qm, kw, kc, bias):
    sw = lax.dot_general(qm, kw, _NT, preferred_element_type=F32)
    if bias is not None:
        sw = sw + bias
    sc = lax.dot_general(qm, kc, _NT, preferred_element_type=F32) if kc is not None else None
    return sw, sc


def _probs(sw, sc):
    m = jnp.max(sw, axis=-1, keepdims=True)
    if sc is not None:
        m = jnp.maximum(m, jnp.max(sc, axis=-1, keepdims=True))
    pw = jnp.exp(sw - m)
    l = jnp.sum(pw, axis=-1, keepdims=True)
    pc = None
    if sc is not None:
        pc = jnp.exp(sc - m)
        l = l + jnp.sum(pc, axis=-1, keepdims=True)
        pc = pc.astype(BF16)
    return pw.astype(BF16), pc, l


def _pv(pw, pc, l, vw, vc):
    o = jnp.dot(pw, vw, preferred_element_type=F32)
    if pc is not None:
        o = o + jnp.dot(pc, vc, preferred_element_type=F32)
    return o / l


ATT_UNROLL = 4


def _nattn_kernel(k_ref, v_ref, q_ref, kc_ref, vc_ref, b_ref, o_ref, *, rows):
    nq = ATT_QROWS * GRID_W
    nk = ATT_KROWS * GRID_W
    n_steps = rows // ATT_QROWS
    lane = lax.broadcasted_iota(I32, (nq, HEAD_PAIR_W), 1)
    second = lane >= HEAD_DIM
    scale = HEAD_DIM ** -0.5

    def body(ii, carry):
        kc = kc_ref[0]
        vc = vc_ref[0]
        chains = []
        for u in range(ATT_UNROLL):
            i = ii * ATT_UNROLL + u
            s = jnp.clip(ATT_QROWS * i - WIN_R // 2, 0, rows - ATT_KROWS)
            kstart = pl.multiple_of(s * GRID_W, LANE)
            qstart = pl.multiple_of(i * nq, LANE)
            cls = jnp.where(i < 2, i, jnp.where(i > n_steps - 3, i - (n_steps - ATT_CLASSES), 2))
            q2 = (q_ref[0, pl.ds(qstart, nq), :].astype(F32) * scale).astype(BF16)
            kw = k_ref[0, pl.ds(kstart, nk), :]
            for hh in range(2):
                head_lanes = second if hh else jnp.logical_not(second)
                qm = jnp.where(head_lanes, q2, jnp.zeros_like(q2))
                chains.append((u, kstart, qstart) + _scores(qm, kw, kc, b_ref[hh, cls]))
        probs = [_probs(sw, sc) for (_, _, _, sw, sc) in chains]
        outs = []
        for (u, kstart, qstart, _, _), (pw, pc, l) in zip(chains, probs):
            outs.append(_pv(pw, pc, l, v_ref[0, pl.ds(kstart, nk), :], vc))
        for u in range(ATT_UNROLL):
            qstart = chains[2 * u][2]
            o = jnp.where(second, outs[2 * u + 1], outs[2 * u])
            o_ref[0, pl.ds(qstart, nq), :] = o.astype(BF16)
        return carry

    lax.fori_loop(0, n_steps // ATT_UNROLL, body, 0)


def _nattn(pb3, pcb3, bias):
    b, seq, _ = pb3.shape
    lc = pcb3.shape[1]
    rows = seq // GRID_W
    nq = ATT_QROWS * GRID_W
    nk = ATT_KROWS * GRID_W
    kblk, vblk, qblk = K_OFF // LANE, V_OFF // LANE, Q_OFF // LANE
    return pl.pallas_call(
        functools.partial(_nattn_kernel, rows=rows),
        grid=(N_HEAD_PAIRS, b),
        in_specs=[
            pl.BlockSpec((1, seq, HEAD_PAIR_W), lambda hp, bi: (bi, 0, kblk + hp)),
            pl.BlockSpec((1, seq, HEAD_PAIR_W), lambda hp, bi: (bi, 0, vblk + hp)),
            pl.BlockSpec((1, seq, HEAD_PAIR_W), lambda hp, bi: (bi, 0, qblk + hp)),
            pl.BlockSpec((1, lc, HEAD_PAIR_W), lambda hp, bi: (bi, 0, kblk + hp)),
            pl.BlockSpec((1, lc, HEAD_PAIR_W), lambda hp, bi: (bi, 0, vblk + hp)),
            pl.BlockSpec((2, ATT_CLASSES, nq, nk), lambda hp, bi: (hp, 0, 0, 0)),
        ],
        out_specs=pl.BlockSpec((1, seq, HEAD_PAIR_W), lambda hp, bi: (bi, 0, hp)),
        out_shape=jax.ShapeDtypeStruct((b, seq, N_HEADS * HEAD_DIM), BF16),
        compiler_params=_cparams(("parallel", "parallel"), 40),
        name="nattn",
    )(pb3, pb3, pb3, pcb3, pcb3, bias)


def _cattn_kernel(k_ref, v_ref, q_ref, o_ref):
    lc = q_ref.shape[1]
    lane = lax.broadcasted_iota(I32, (lc, HEAD_PAIR_W), 1)
    second = lane >= HEAD_DIM
    q2 = (q_ref[0].astype(F32) * (HEAD_DIM ** -0.5)).astype(BF16)
    k = k_ref[0]
    v = v_ref[0]
    outs = []
    for hh in range(2):
        head_lanes = second if hh else jnp.logical_not(second)
        qm = jnp.where(head_lanes, q2, jnp.zeros_like(q2))
        pw, _, l = _probs(*_scores(qm, k, None, None))
        outs.append(_pv(pw, None, l, v, None))
    o_ref[0] = jnp.where(second, outs[1], outs[0]).astype(BF16)


def _cattn(pcb3):
    b, lc, _ = pcb3.shape
    kblk, vblk, qblk = K_OFF // LANE, V_OFF // LANE, Q_OFF // LANE
    return pl.pallas_call(
        _cattn_kernel,
        grid=(b, N_HEAD_PAIRS),
        in_specs=[
            pl.BlockSpec((1, lc, HEAD_PAIR_W), lambda bi, hp: (bi, 0, kblk + hp)),
            pl.BlockSpec((1, lc, HEAD_PAIR_W), lambda bi, hp: (bi, 0, vblk + hp)),
            pl.BlockSpec((1, lc, HEAD_PAIR_W), lambda bi, hp: (bi, 0, qblk + hp)),
        ],
        out_specs=pl.BlockSpec((1, lc, HEAD_PAIR_W), lambda bi, hp: (bi, 0, hp)),
        out_shape=jax.ShapeDtypeStruct((b, lc, N_HEADS * HEAD_DIM), BF16),
        compiler_params=_cparams(("parallel", "parallel"), 32),
        name="cattn",
    )(pcb3, pcb3, pcb3)


def _dft_kernel(c_ref, s_ref, *, n):
    tk = c_ref.shape[0]
    k = pl.program_id(0) * tk + lax.broadcasted_iota(I32, (tk, n), 0)
    t = lax.broadcasted_iota(I32, (tk, n), 1)
    m = (k * t) & (n - 1)
    ang = m.astype(F32) * (2.0 * math.pi / n)
    c_ref[...] = jnp.cos(ang).astype(BF16)
    s_ref[...] = jnp.sin(ang).astype(BF16)


def _dft_mats(n):
    tk = min(n, 256)
    return pl.pallas_call(
        functools.partial(_dft_kernel, n=n),
        grid=(n // tk,),
        in_specs=[],
        out_specs=[pl.BlockSpec((tk, n), lambda i: (i, 0))] * 2,
        out_shape=[jax.ShapeDtypeStruct((n, n), BF16)] * 2,
        compiler_params=_cparams(("parallel",), 48),
        name=f"dft_mats_{n}",
    )()


def _fourier_kernel(u_ref, cc_ref, sc_ref, cl_ref, sl_ref, o_ref, a_ref, b_ref, *, seq):
    @pl.when(pl.program_id(1) == 0)
    def _():
        for g in range(FOURIER_GROUPS):
            sl = slice(g * FOURIER_GROUP_W, (g + 1) * FOURIER_GROUP_W)
            ug = u_ref[0, :, sl]
            a_ref[:, sl] = jnp.dot(ug, cc_ref[...], preferred_element_type=F32).astype(BF16)
            b_ref[:, sl] = jnp.dot(ug, sc_ref[...], preferred_element_type=F32).astype(BF16)

    y = (jnp.dot(cl_ref[...], a_ref[...], preferred_element_type=F32)
         - jnp.dot(sl_ref[...], b_ref[...], preferred_element_type=F32))
    o_ref[0] = (y * (1.0 / math.sqrt(seq * FOURIER_GROUP_W))).astype(BF16)


def _fourier(pb3, cc, sc, cl, sl):
    b, seq, _ = pb3.shape
    tk = min(seq, 512)
    return pl.pallas_call(
        functools.partial(_fourier_kernel, seq=seq),
        grid=(b, seq // tk),
        in_specs=[
            pl.BlockSpec((1, seq, FOURIER_W), lambda bi, k: (bi, 0, F_OFF // FOURIER_W)),
            pl.BlockSpec((FOURIER_GROUP_W, FOURIER_GROUP_W), lambda bi, k: (0, 0)),
            pl.BlockSpec((FOURIER_GROUP_W, FOURIER_GROUP_W), lambda bi, k: (0, 0)),
            pl.BlockSpec((tk, seq), lambda bi, k: (k, 0)),
            pl.BlockSpec((tk, seq), lambda bi, k: (k, 0)),
        ],
        out_specs=pl.BlockSpec((1, tk, FOURIER_W), lambda bi, k: (bi, k, 0)),
        out_shape=jax.ShapeDtypeStruct((b, seq, FOURIER_W), BF16),
        scratch_shapes=[pltpu.VMEM((seq, FOURIER_W), BF16), pltpu.VMEM((seq, FOURIER_W), BF16)],
        compiler_params=_cparams(("parallel", "arbitrary"), 48),
        name="fourier",
    )(pb3, cc, sc, cl, sl)


POOL_PAD = 8


def _pool_kernel(u_ref, o_ref, pad_ref, *, seq):
    t = lax.broadcasted_iota(I32, (seq, POOL_GROUP_W), 0)
    zeros = jnp.zeros((POOL_PAD, POOL_GROUP_W), F32)
    pad_ref[0:POOL_PAD, :] = zeros
    pad_ref[seq + POOL_PAD:seq + 2 * POOL_PAD, :] = zeros
    pad_ref[POOL_PAD:seq + POOL_PAD, :] = u_ref[0]
    for g, w in enumerate(POOL_WINDOWS):
        @pl.when(pl.program_id(1) == g)
        def _(w=w):
            acc = None
            for d in range(-(w // 2), w - w // 2):
                term = pad_ref[pl.ds(POOL_PAD + d, seq), :]
                acc = term if acc is None else acc + term
            cnt = (jnp.minimum(t + (w - w // 2), seq) - jnp.maximum(t - w // 2, 0)).astype(F32)
            o_ref[0] = (acc / cnt - u_ref[0]).astype(BF16)


def _pool(pp3):
    b, seq, _ = pp3.shape
    spec = pl.BlockSpec((1, seq, POOL_GROUP_W), lambda bi, g: (bi, 0, g))
    return pl.pallas_call(
        functools.partial(_pool_kernel, seq=seq),
        grid=(b, len(POOL_WINDOWS)),
        in_specs=[spec],
        out_specs=spec,
        out_shape=jax.ShapeDtypeStruct((b, seq, POOL_W), BF16),
        scratch_shapes=[pltpu.VMEM((seq + 2 * POOL_PAD, POOL_GROUP_W), F32)],
        compiler_params=_cparams(("parallel", "parallel"), 32),
        name="pool",
    )(pp3)


def _merge_kernel(att_ref, four_ref, pool_ref, ga_ref, gf_ref, gp_ref, x_ref, g1_ref,
                  wao_ref, wf_ref, wp_ref, ps_ref, wo_ref, o_ref):
    y_att = jnp.dot(att_ref[0], wao_ref[...], preferred_element_type=F32)
    y_four = jnp.dot(four_ref[0], wf_ref[...], preferred_element_type=F32)
    pooled = pool_ref[0]
    y_pool = jnp.concatenate(
        [jnp.dot(pooled[:, g * POOL_GROUP_W:(g + 1) * POOL_GROUP_W], wp_ref[g], preferred_element_type=F32)
         for g in range(len(POOL_WINDOWS))], axis=-1) * ps_ref[...]
    merged = (jax.nn.sigmoid(ga_ref[0].astype(F32)) * y_att
              + jax.nn.sigmoid(gf_ref[0].astype(F32)) * y_four
              + jax.nn.sigmoid(gp_ref[0].astype(F32)) * y_pool)
    y = jnp.dot(merged.astype(BF16), wo_ref[...], preferred_element_type=F32)
    o_ref[0] = x_ref[0] + g1_ref[0] * y


def _merge(att, four, pooled, pb3, x3, mod3, wao, wf, wp, ps, wo, *, ctx_row):
    b, seq, _ = x3.shape
    tm = min(seq, 512)
    gblk = G_OFF // D_MODEL
    mrow = (lambda bi: bi) if ctx_row is None else (lambda bi: ctx_row)
    tok = lambda w: pl.BlockSpec((1, tm, w), lambda bi, i: (bi, i, 0))
    gate = lambda k: pl.BlockSpec((1, tm, D_MODEL), lambda bi, i: (bi, i, gblk + k))
    full = lambda a: pl.BlockSpec(a.shape, lambda bi, i: (0,) * a.ndim)
    return pl.pallas_call(
        _merge_kernel,
        grid=(b, seq // tm),
        in_specs=[
            tok(D_MODEL), tok(FOURIER_W), tok(POOL_W), gate(0), gate(1), gate(2), tok(D_MODEL),
            pl.BlockSpec((1, 1, D_MODEL), lambda bi, i: (mrow(bi), 0, 2)),
            full(wao), full(wf), full(wp), full(ps), full(wo),
        ],
        out_specs=tok(D_MODEL),
        out_shape=jax.ShapeDtypeStruct((b, seq, D_MODEL), F32),
        compiler_params=_cparams(("parallel", "parallel"), 48),
        name="merge",
    )(att, four, pooled, pb3, pb3, pb3, x3, mod3, wao, wf, wp, ps, wo)


def _router_kernel(x_ref, sh_ref, sc_ref, g_ref, wr_ref, h_ref, lg_ref):
    h = _modulate(x_ref[0], g_ref[...], sh_ref[0], sc_ref[0])
    h_ref[0] = h.astype(BF16)
    lg_ref[0] = lax.dot_general(wr_ref[...], h, (((1,), (1,)), ((), ())),
                                precision=HIGHEST, preferred_element_type=F32)


def _router(x3, mod3, gain, wr_t, *, ctx_row):
    b, seq, _ = x3.shape
    tm = min(seq, 512)
    mrow = (lambda bi: bi) if ctx_row is None else (lambda bi: ctx_row)
    return pl.pallas_call(
        _router_kernel,
        grid=(b, seq // tm),
        in_specs=[
            pl.BlockSpec((1, tm, D_MODEL), lambda bi, i: (bi, i, 0)),
            pl.BlockSpec((1, 1, D_MODEL), lambda bi, i: (mrow(bi), 0, 3)),
            pl.BlockSpec((1, 1, D_MODEL), lambda bi, i: (mrow(bi), 0, 4)),
            pl.BlockSpec((1, D_MODEL), lambda bi, i: (0, 0)),
            pl.BlockSpec((N_EXPERTS, D_MODEL), lambda bi, i: (0, 0)),
        ],
        out_specs=[
            pl.BlockSpec((1, tm, D_MODEL), lambda bi, i: (bi, i, 0)),
            pl.BlockSpec((1, N_EXPERTS, tm), lambda bi, i: (bi, 0, i)),
        ],
        out_shape=[
            jax.ShapeDtypeStruct((b, seq, D_MODEL), BF16),
            jax.ShapeDtypeStruct((b, N_EXPERTS, seq), F32),
        ],
        compiler_params=_cparams(("parallel", "parallel"), 32),
        name="router",
    )(x3, mod3, mod3, gain, wr_t)


def _exclusive_prefix(mask, tri):
    e, seq = mask.shape
    ones = jnp.where(mask, 1.0, 0.0)
    offs = jnp.zeros((e, 1), F32)
    pieces = []
    for k in range(seq // LANE):
        blk = ones[:, k * LANE:(k + 1) * LANE]
        local = jnp.dot(blk.astype(BF16), tri, preferred_element_type=F32)
        pieces.append(local + offs)
        offs = offs + jnp.sum(blk, axis=1, keepdims=True)
    return jnp.concatenate(pieces, axis=1)


def _select_kernel(lg_ref, slot_ref, aff_ref, *, cap):
    z = lg_ref[0]
    z = z - jnp.max(z, axis=0, keepdims=True)
    ez = jnp.exp(z)
    a = ez / jnp.sum(ez, axis=0, keepdims=True)
    aff_ref[0] = a
    capf = float(cap)

    def bisect(_, lohi):
        lo, hi = lohi
        mid = (lo + hi) * 0.5
        cnt = jnp.sum(jnp.where(a >= mid, 1.0, 0.0), axis=1, keepdims=True)
        ge = cnt >= capf
        return jnp.where(ge, mid, lo), jnp.where(ge, hi, mid)

    e = a.shape[0]
    lo, hi = lax.fori_loop(0, SELECT_ITERS, bisect,
                           (jnp.zeros((e, 1), F32), jnp.full((e, 1), 2.0, F32)))
    r_i = lax.broadcasted_iota(I32, (LANE, LANE), 0)
    c_i = lax.broadcasted_iota(I32, (LANE, LANE), 1)
    tri = jnp.where(r_i < c_i, 1.0, 0.0).astype(BF16)
    above = a >= hi
    n_above = jnp.sum(jnp.where(above, 1.0, 0.0), axis=1, keepdims=True)
    tied = (a >= lo) & jnp.logical_not(above)
    tie_rank = _exclusive_prefix(tied, tri)
    sel = above | (tied & (tie_rank < capf - n_above))
    pos = _exclusive_prefix(sel, tri)
    slot_ref[0] = jnp.where(sel, pos, -1.0).astype(I32)


def _select(logits_t, cap):
    b, e, seq = logits_t.shape
    spec = pl.BlockSpec((1, e, seq), lambda bi: (bi, 0, 0))
    return pl.pallas_call(
        functools.partial(_select_kernel, cap=cap),
        grid=(b,),
        in_specs=[spec],
        out_specs=[spec, spec],
        out_shape=[jax.ShapeDtypeStruct((b, e, seq), I32), jax.ShapeDtypeStruct((b, e, seq), F32)],
        compiler_params=_cparams(("parallel",), 32),
        name="select",
    )(logits_t)


def _gather_kernel(h_ref, slot_ref, aff_ref, xg_ref, gate_ref, *, cap, chunk):
    seq = h_ref.shape[1]
    j = lax.broadcasted_iota(I32, (cap, chunk), 0)
    xg = jnp.zeros((cap, D_MODEL), F32)
    gate = jnp.zeros((cap, 1), F32)
    for c in range(seq // chunk):
        sl = slice(c * chunk, (c + 1) * chunk)
        hit = slot_ref[0, 0, :, sl] == j
        onehot = jnp.where(hit, 1.0, 0.0).astype(BF16)
        xg = xg + jnp.dot(onehot, h_ref[0, sl, :], preferred_element_type=F32)
        gate = gate + jnp.sum(jnp.where(hit, aff_ref[0, 0, :, sl], 0.0), axis=1, keepdims=True)
    xg_ref[0, 0] = xg.astype(BF16)
    gate_ref[0, 0] = gate


def _gather(h2, slots4, aff4, cap):
    b, seq, _ = h2.shape
    e = slots4.shape[1]
    chunk = min(seq, 1024)
    row = pl.BlockSpec((1, 1, 1, seq), lambda bi, ei: (bi, ei, 0, 0))
    return pl.pallas_call(
        functools.partial(_gather_kernel, cap=cap, chunk=chunk),
        grid=(b, e),
        in_specs=[pl.BlockSpec((1, seq, D_MODEL), lambda bi, ei: (bi, 0, 0)), row, row],
        out_specs=[
            pl.BlockSpec((1, 1, cap, D_MODEL), lambda bi, ei: (ei, bi, 0, 0)),
            pl.BlockSpec((1, 1, cap, 1), lambda bi, ei: (ei, bi, 0, 0)),
        ],
        out_shape=[
            jax.ShapeDtypeStruct((e, b, cap, D_MODEL), BF16),
            jax.ShapeDtypeStruct((e, b, cap, 1), F32),
        ],
        compiler_params=_cparams(("parallel", "arbitrary"), 48),
        name="gather",
    )(h2, slots4, aff4)


FF_CHUNK = 256
FFN_ROW_TILES = 2


def _ffn_kernel(x_ref, gate_ref, wg_ref, wu_ref, wd_ref, o_ref, acc_ref):
    fc = pl.program_id(2)

    @pl.when(fc == 0)
    def _():
        acc_ref[...] = jnp.zeros_like(acc_ref)

    x = x_ref[0]
    a = jnp.dot(x, wg_ref[0, 0].astype(BF16), preferred_element_type=F32)
    u = jnp.dot(x, wu_ref[0, 0].astype(BF16), preferred_element_type=F32)
    hmid = (a * jax.nn.sigmoid(a) * u).astype(BF16)
    acc_ref[...] += jnp.dot(hmid, wd_ref[0, 0].astype(BF16), preferred_element_type=F32)

    @pl.when(fc == pl.num_programs(2) - 1)
    def _():
        o_ref[0] = (acc_ref[...] * gate_ref[0]).astype(BF16)


def _ffn(xg3, gate3, wg_all, wu_all, wd_all, layer):
    e, m, _ = xg3.shape
    tm = m // FFN_ROW_TILES if m >= 1024 else m
    assert m % tm == 0 and tm % 16 == 0
    return pl.pallas_call(
        _ffn_kernel,
        grid=(e, m // tm, EXPERT_FF // FF_CHUNK),
        in_specs=[
            pl.BlockSpec((1, tm, D_MODEL), lambda ei, i, fc: (ei, i, 0)),
            pl.BlockSpec((1, tm, 1), lambda ei, i, fc: (ei, i, 0)),
            pl.BlockSpec((1, 1, D_MODEL, FF_CHUNK), lambda ei, i, fc: (layer, ei, 0, fc)),
            pl.BlockSpec((1, 1, D_MODEL, FF_CHUNK), lambda ei, i, fc: (layer, ei, 0, fc)),
            pl.BlockSpec((1, 1, FF_CHUNK, D_MODEL), lambda ei, i, fc: (layer, ei, fc, 0)),
        ],
        out_specs=pl.BlockSpec((1, tm, D_MODEL), lambda ei, i, fc: (ei, i, 0)),
        out_shape=jax.ShapeDtypeStruct((e, m, D_MODEL), BF16),
        scratch_shapes=[pltpu.VMEM((tm, D_MODEL), F32)],
        compiler_params=_cparams(("parallel", "parallel", "arbitrary"), 56),
        name="ffn",
    )(xg3, gate3, wg_all, wu_all, wd_all)


COMBINE_EXPERTS = 4


def _combine_kernel(slot_ref, gy_ref, x_ref, g2_ref, gf_ref, o_ref, acc_ref, *, cap, final):
    ec = pl.program_id(2)
    tt = x_ref.shape[1]

    @pl.when(ec == 0)
    def _():
        acc_ref[...] = jnp.zeros_like(acc_ref)

    j = lax.broadcasted_iota(I32, (cap, tt), 0)
    onehot = jnp.concatenate(
        [jnp.where(slot_ref[0, k] == j, 1.0, 0.0).astype(BF16) for k in range(COMBINE_EXPERTS)], axis=0)
    gy = gy_ref[...].reshape(COMBINE_EXPERTS * cap, D_MODEL)
    acc_ref[...] += lax.dot_general(onehot, gy, (((0,), (0,)), ((), ())), preferred_element_type=F32)

    @pl.when(ec == pl.num_programs(2) - 1)
    def _():
        xn = x_ref[0] + g2_ref[0] * acc_ref[...]
        if final:
            ms = jnp.mean(xn * xn, axis=-1, keepdims=True)
            xn = (xn * lax.rsqrt(ms + RMS_EPS)) * gf_ref[...]
        o_ref[0] = xn


def _combine(slots4, gy3, row_off, x3, mod3, gfinal, *, cap, ctx_row, final):
    b, seq, _ = x3.shape
    e = slots4.shape[1]
    tt = min(seq, 1024)
    assert row_off % cap == 0
    blk_off = row_off // cap
    mrow = (lambda bi: bi) if ctx_row is None else (lambda bi: ctx_row)
    return pl.pallas_call(
        functools.partial(_combine_kernel, cap=cap, final=final),
        grid=(b, seq // tt, e // COMBINE_EXPERTS),
        in_specs=[
            pl.BlockSpec((1, COMBINE_EXPERTS, 1, tt), lambda bi, i, ec: (bi, ec, 0, i)),
            pl.BlockSpec((COMBINE_EXPERTS, cap, D_MODEL), lambda bi, i, ec: (ec, blk_off + bi, 0)),
            pl.BlockSpec((1, tt, D_MODEL), lambda bi, i, ec: (bi, i, 0)),
            pl.BlockSpec((1, 1, D_MODEL), lambda bi, i, ec: (mrow(bi), 0, 5)),
            pl.BlockSpec((1, D_MODEL), lambda bi, i, ec: (0, 0)),
        ],
        out_specs=pl.BlockSpec((1, tt, D_MODEL), lambda bi, i, ec: (bi, i, 0)),
        out_shape=jax.ShapeDtypeStruct((b, seq, D_MODEL), F32),
        scratch_shapes=[pltpu.VMEM((tt, D_MODEL), F32)],
        compiler_params=_cparams(("parallel", "parallel", "arbitrary"), 48),
        name="combine",
    )(slots4, gy3, x3, mod3, gfinal)


def _moe_route(x3, mod3, gain2, wr_t, *, ctx_row):
    b, seq, _ = x3.shape
    cap = EC_CAPACITY_FACTOR * seq // N_EXPERTS
    h2, logits_t = _router(x3, mod3, gain2, wr_t, ctx_row=ctx_row)
    slots, aff = _select(logits_t, cap)
    slots4 = slots.reshape(b, N_EXPERTS, 1, seq)
    aff4 = aff.reshape(b, N_EXPERTS, 1, seq)
    xg, gate = _gather(h2, slots4, aff4, cap)
    return slots4, xg.reshape(N_EXPERTS, b * cap, D_MODEL), gate.reshape(N_EXPERTS, b * cap, 1), cap


def kernel(x, c, ctx, c_ctx, ada_w, ada_b, norm1_g, norm2_g, w_in, rpb, w_att_o, w_fourier, w_pool,
           pool_scale, w_out, w_router, w_exp_gate, w_exp_up, w_exp_down, final_norm_g):
    b, seq, d = x.shape
    lc = ctx.shape[1]
    assert d == D_MODEL and seq % (GRID_W * ATT_QROWS) == 0 and b + 1 <= MOD_ROWS
    rows = seq // GRID_W
    ctx_row = b

    cond = jnp.concatenate([c, c_ctx[None, :], jnp.zeros((MOD_ROWS - b - 1, d), F32)], axis=0)
    cl, sl = _dft_mats(seq)
    clc, slc = _dft_mats(lc)
    cc, sc = _dft_mats(FOURIER_GROUP_W)
    gfinal = final_norm_g.reshape(1, d)

    for i in range(DEPTH):
        update_ctx = i < DEPTH - 1
        mod3 = _adaln(cond, ada_w[i], ada_b[i]).reshape(MOD_ROWS, 1, 6 * d)
        g1n = norm1_g[i].reshape(1, d)
        g2n = norm2_g[i].reshape(1, d)
        w_in_b = w_in[i].astype(BF16)
        wao = w_att_o[i].astype(BF16)
        wf = w_fourier[i].astype(BF16)
        wp = w_pool[i].astype(BF16)
        ps = pool_scale[i].reshape(1, d)
        wo = w_out[i].astype(BF16)
        wr_t = w_router[i].T
        bias = _bias_table(rpb[i], rows)

        pb, pp = _modproj(x.reshape(b * seq, d), mod3, g1n, w_in_b, seq=seq, ctx_row=None,
                          n_out=IN_W, with_pool=True)
        n_ctx = IN_W if update_ctx else Q_OFF
        pcb, pcp = _modproj(ctx.reshape(b * lc, d), mod3, g1n, w_in_b[:, :n_ctx], seq=lc, ctx_row=ctx_row,
                            n_out=n_ctx, with_pool=update_ctx)
        pb3 = pb.reshape(b, seq, IN_W)
        pcb3 = pcb.reshape(b, lc, n_ctx)

        att = _nattn(pb3, pcb3, bias)
        four = _fourier(pb3, cc, sc, cl, sl)
        pooled = _pool(pp.reshape(b, seq, POOL_W))
        x = _merge(att, four, pooled, pb3, x, mod3, wao, wf, wp, ps, wo, ctx_row=None)
        slots_x, xg, gate, cap = _moe_route(x, mod3, g2n, wr_t, ctx_row=None)

        if update_ctx:
            att_c = _cattn(pcb3)
            four_c = _fourier(pcb3, cc, sc, clc, slc)
            pooled_c = _pool(pcp.reshape(b, lc, POOL_W))
            ctx = _merge(att_c, four_c, pooled_c, pcb3, ctx, mod3, wao, wf, wp, ps, wo, ctx_row=ctx_row)
            slots_c, xg_c, gate_c, cap_c = _moe_route(ctx, mod3, g2n, wr_t, ctx_row=ctx_row)
            xg = jnp.concatenate([xg, xg_c], axis=1)
            gate = jnp.concatenate([gate, gate_c], axis=1)

        gy = _ffn(xg, gate, w_exp_gate, w_exp_up, w_exp_down, i)
        if update_ctx:
            ctx = _combine(slots_c, gy, b * cap, ctx, mod3, gfinal, cap=cap_c, ctx_row=ctx_row, final=False)
        x = _combine(slots_x, gy, 0, x, mod3, gfinal, cap=cap, ctx_row=None, final=not update_ctx)
    return x
```

```python
import functools
import math

import jax
import jax.numpy as jnp
from jax import lax
from jax.experimental import pallas as pl
from jax.experimental.pallas import tpu as pltpu

F32 = jnp.float32
BF16 = jnp.bfloat16
I32 = jnp.int32
HIGHEST = lax.Precision.HIGHEST

D_MODEL = 1024
DEPTH = 2
GRID_W = 64
N_HEADS = 16
HEAD_DIM = 64
WIN_R = 8
WIN_C = 16
FOURIER_GROUPS = 4
FOURIER_GROUP_W = 128
FOURIER_W = 512
POOL_WINDOWS = (2, 4, 8, 16)
POOL_GROUP_W = 128
POOL_W = 512
POOL_OUT_GROUP = 256
K_OFF, V_OFF, Q_OFF, F_OFF, P_OFF, G_OFF = 0, 1024, 2048, 3072, 3584, 4096
IN_W = 7168
N_EXPERTS = 16
EC_CAPACITY_FACTOR = 2
EXPERT_FF = 2816
RMS_EPS = 1e-6

LANE = 128
HEAD_PAIR_W = 2 * HEAD_DIM
N_HEAD_PAIRS = N_HEADS // 2
MOD_ROWS = 16
NEG_BIG = -1e30
ATT_QROWS = 2
ATT_KROWS = 10
ATT_CLASSES = 5
SELECT_ITERS = 64
MiB = 1024 * 1024


def _cparams(sem, vmem_mib):
    return pltpu.CompilerParams(dimension_semantics=sem, vmem_limit_bytes=vmem_mib * MiB)


def _adaln_kernel(c_ref, w_ref, b_ref, o_ref):
    c = c_ref[...]
    s = c * jax.nn.sigmoid(c)
    o_ref[...] = jnp.dot(s, w_ref[...], precision=HIGHEST, preferred_element_type=F32) + b_ref[...]


def _adaln(cond_rows, ada_w, ada_b):
    n = ada_w.shape[1]
    tn = 1024
    return pl.pallas_call(
        _adaln_kernel,
        grid=(n // tn,),
        in_specs=[
            pl.BlockSpec((MOD_ROWS, D_MODEL), lambda j: (0, 0)),
            pl.BlockSpec((D_MODEL, tn), lambda j: (0, j)),
            pl.BlockSpec((1, tn), lambda j: (0, j)),
        ],
        out_specs=pl.BlockSpec((MOD_ROWS, tn), lambda j: (0, j)),
        out_shape=jax.ShapeDtypeStruct((MOD_ROWS, n), F32),
        compiler_params=_cparams(("arbitrary",), 32),
        name="adaln",
    )(cond_rows, ada_w, ada_b.reshape(1, n))


def _modulate(x, g, shift, scale):
    ms = jnp.mean(x * x, axis=-1, keepdims=True)
    y = x * lax.rsqrt(ms + RMS_EPS)
    return (y * g) * (1.0 + scale) + shift


def _modproj_kernel(x_ref, sh_ref, sc_ref, g_ref, w_ref, *rest, pool_tile):
    if pool_tile is None:
        o_ref, h_ref = rest
        pp_ref = None
    else:
        o_ref, pp_ref, h_ref = rest
    j = pl.program_id(1)

    @pl.when(j == 0)
    def _():
        h = _modulate(x_ref[...], g_ref[...], sh_ref[0], sc_ref[0])
        h_ref[...] = h.astype(BF16)

    acc = jnp.dot(h_ref[...], w_ref[...], preferred_element_type=F32)
    o_ref[...] = acc.astype(BF16)
    if pool_tile is not None:
        @pl.when(j == pool_tile)
        def _():
            pp_ref[...] = acc[:, P_OFF % 1024:]


def _modproj(x2, mod3, gain, w, *, seq, ctx_row, n_out, with_pool):
    rows = x2.shape[0]
    tm = min(seq, 1024)
    tn = 1024
    tiles_per_seq = seq // tm
    if ctx_row is None:
        mrow = lambda i: i // tiles_per_seq
    else:
        mrow = lambda i: ctx_row
    pool_tile = (P_OFF // tn) if with_pool else None
    out_shape = [jax.ShapeDtypeStruct((rows, n_out), BF16)]
    out_specs = [pl.BlockSpec((tm, tn), lambda i, j: (i, j))]
    if with_pool:
        out_shape.append(jax.ShapeDtypeStruct((rows, POOL_W), F32))
        out_specs.append(pl.BlockSpec((tm, POOL_W), lambda i, j: (i, 0)))
    res = pl.pallas_call(
        functools.partial(_modproj_kernel, pool_tile=pool_tile),
        grid=(rows // tm, n_out // tn),
        in_specs=[
            pl.BlockSpec((tm, D_MODEL), lambda i, j: (i, 0)),
            pl.BlockSpec((1, 1, D_MODEL), lambda i, j: (mrow(i), 0, 0)),
            pl.BlockSpec((1, 1, D_MODEL), lambda i, j: (mrow(i), 0, 1)),
            pl.BlockSpec((1, D_MODEL), lambda i, j: (0, 0)),
            pl.BlockSpec((D_MODEL, tn), lambda i, j: (0, j)),
        ],
        out_specs=out_specs,
        out_shape=out_shape,
        scratch_shapes=[pltpu.VMEM((tm, D_MODEL), BF16)],
        compiler_params=_cparams(("parallel", "arbitrary"), 48),
        name="modproj",
    )(x2, mod3, mod3, gain, w)
    return res if with_pool else (res[0], None)


def _bias_kernel(rpb_ref, o_ref, *, rows):
    h = pl.program_id(0)
    cls = pl.program_id(1)
    i_rep = jnp.where(cls < 3, cls, cls + (rows // ATT_QROWS - ATT_CLASSES))
    s = jnp.clip(ATT_QROWS * i_rep - WIN_R // 2, 0, rows - ATT_KROWS)
    qc = lax.broadcasted_iota(I32, (GRID_W, LANE), 0)
    lane = lax.broadcasted_iota(I32, (GRID_W, LANE), 1)
    kc = lane & (GRID_W - 1)
    first_half = lane < GRID_W
    cs = jnp.clip(qc - WIN_C // 2, 0, GRID_W - WIN_C)
    col_valid = (kc >= cs) & (kc < cs + WIN_C)
    dcol = kc - qc + (WIN_C - 1)
    n_coff = 2 * WIN_C - 1
    n_roff = 2 * WIN_R - 1
    for ri in range(ATT_QROWS):
        r = ATT_QROWS * i_rep + ri
        rs = jnp.clip(r - WIN_R // 2, 0, rows - WIN_R)
        for m in range(ATT_KROWS // 2):
            krow_a = s + 2 * m
            krow_b = krow_a + 1
            va = ((krow_a >= rs) & (krow_a < rs + WIN_R)).astype(I32)
            vb = ((krow_b >= rs) & (krow_b < rs + WIN_R)).astype(I32)
            base_a = (h * n_roff + jnp.clip(krow_a - r + WIN_R - 1, 0, n_roff - 1)) * n_coff
            base_b = (h * n_roff + jnp.clip(krow_b - r + WIN_R - 1, 0, n_roff - 1)) * n_coff
            acc = jnp.zeros((GRID_W, LANE), F32)
            for c in range(n_coff):
                val = jnp.where(first_half, rpb_ref[base_a + c], rpb_ref[base_b + c])
                acc = jnp.where(dcol == c, val, acc)
            row_valid = jnp.where(first_half, va, vb) > 0
            tile = jnp.where(col_valid & row_valid, acc, NEG_BIG)
            o_ref[0, 0, ri * GRID_W:(ri + 1) * GRID_W, m * LANE:(m + 1) * LANE] = tile


def _bias_table(rpb, rows):
    nq = ATT_QROWS * GRID_W
    nk = ATT_KROWS * GRID_W
    return pl.pallas_call(
        functools.partial(_bias_kernel, rows=rows),
        grid_spec=pltpu.PrefetchScalarGridSpec(
            num_scalar_prefetch=1,
            grid=(N_HEADS, ATT_CLASSES),
            in_specs=[],
            out_specs=pl.BlockSpec((1, 1, nq, nk), lambda h, c, rpb: (h, c, 0, 0)),
        ),
        out_shape=jax.ShapeDtypeStruct((N_HEADS, ATT_CLASSES, nq, nk), F32),
        compiler_params=_cparams(("arbitrary", "arbitrary"), 32),
        name="bias_table",
    )(rpb.reshape(-1))


_NT = (((1,), (1,)), ((), ()))


def _scores(qm, kw, kc, bias):
    sw = lax.dot_general(qm, kw, _NT, preferred_element_type=F32)
    if bias is not None:
        sw = sw + bias
    sc = lax.dot_general(qm, kc, _NT, preferred_element_type=F32) if kc is not None else None
    return sw, sc


def _probs(sw, sc):
    m = jnp.max(sw, axis=-1, keepdims=True)
    if sc is not None:
        m = jnp.maximum(m, jnp.max(sc, axis=-1, keepdims=True))
    pw = jnp.exp(sw - m)
    l = jnp.sum(pw, axis=-1, keepdims=True)
    pc = None
    if sc is not None:
        pc = jnp.exp(sc - m)
        l = l + jnp.sum(pc, axis=-1, keepdims=True)
        pc = pc.astype(BF16)
    return pw.astype(BF16), pc, l


def _pv(pw, pc, l, vw, vc):
    o = jnp.dot(pw, vw, preferred_element_type=F32)
    if pc is not None:
        o = o + jnp.dot(pc, vc, preferred_element_type=F32)
    return o / l


ATT_UNROLL = 4


def _nattn_kernel(k_ref, v_ref, q_ref, kc_ref, vc_ref, b_ref, o_ref, *, rows):
    nq = ATT_QROWS * GRID_W
    nk = ATT_KROWS * GRID_W
    n_steps = rows // ATT_QROWS
    lane = lax.broadcasted_iota(I32, (nq, HEAD_PAIR_W), 1)
    second = lane >= HEAD_DIM
    scale = HEAD_DIM ** -0.5

    def body(ii, carry):
        kc = kc_ref[0]
        vc = vc_ref[0]
        chains = []
        for u in range(ATT_UNROLL):
            i = ii * ATT_UNROLL + u
            s = jnp.clip(ATT_QROWS * i - WIN_R // 2, 0, rows - ATT_KROWS)
            kstart = pl.multiple_of(s * GRID_W, LANE)
            qstart = pl.multiple_of(i * nq, LANE)
            cls = jnp.where(i < 2, i, jnp.where(i > n_steps - 3, i - (n_steps - ATT_CLASSES), 2))
            q2 = (q_ref[0, pl.ds(qstart, nq), :].astype(F32) * scale).astype(BF16)
            kw = k_ref[0, pl.ds(kstart, nk), :]
            for hh in range(2):
                head_lanes = second if hh else jnp.logical_not(second)
                qm = jnp.where(head_lanes, q2, jnp.zeros_like(q2))
                chains.append((u, kstart, qstart) + _scores(qm, kw, kc, b_ref[hh, cls]))
        probs = [_probs(sw, sc) for (_, _, _, sw, sc) in chains]
        outs = []
        for (u, kstart, qstart, _, _), (pw, pc, l) in zip(chains, probs):
            outs.append(_pv(pw, pc, l, v_ref[0, pl.ds(kstart, nk), :], vc))
        for u in range(ATT_UNROLL):
            qstart = chains[2 * u][2]
            o = jnp.where(second, outs[2 * u + 1], outs[2 * u])
            o_ref[0, pl.ds(qstart, nq), :] = o.astype(BF16)
        return carry

    lax.fori_loop(0, n_steps // ATT_UNROLL, body, 0)


def _nattn(pb3, pcb3, bias):
    b, seq, _ = pb3.shape
    lc = pcb3.shape[1]
    rows = seq // GRID_W
    nq = ATT_QROWS * GRID_W
    nk = ATT_KROWS * GRID_W
    kblk, vblk, qblk = K_OFF // LANE, V_OFF // LANE, Q_OFF // LANE
    return pl.pallas_call(
        functools.partial(_nattn_kernel, rows=rows),
        grid=(N_HEAD_PAIRS, b),
        in_specs=[
            pl.BlockSpec((1, seq, HEAD_PAIR_W), lambda hp, bi: (bi, 0, kblk + hp)),
            pl.BlockSpec((1, seq, HEAD_PAIR_W), lambda hp, bi: (bi, 0, vblk + hp)),
            pl.BlockSpec((1, seq, HEAD_PAIR_W), lambda hp, bi: (bi, 0, qblk + hp)),
            pl.BlockSpec((1, lc, HEAD_PAIR_W), lambda hp, bi: (bi, 0, kblk + hp)),
            pl.BlockSpec((1, lc, HEAD_PAIR_W), lambda hp, bi: (bi, 0, vblk + hp)),
            pl.BlockSpec((2, ATT_CLASSES, nq, nk), lambda hp, bi: (hp, 0, 0, 0)),
        ],
        out_specs=pl.BlockSpec((1, seq, HEAD_PAIR_W), lambda hp, bi: (bi, 0, hp)),
        out_shape=jax.ShapeDtypeStruct((b, seq, N_HEADS * HEAD_DIM), BF16),
        compiler_params=_cparams(("parallel", "parallel"), 40),
        name="nattn",
    )(pb3, pb3, pb3, pcb3, pcb3, bias)


def _cattn_kernel(k_ref, v_ref, q_ref, o_ref):
    lc = q_ref.shape[1]
    lane = lax.broadcasted_iota(I32, (lc, HEAD_PAIR_W), 1)
    second = lane >= HEAD_DIM
    q2 = (q_ref[0].astype(F32) * (HEAD_DIM ** -0.5)).astype(BF16)
    k = k_ref[0]
    v = v_ref[0]
    outs = []
    for hh in range(2):
        head_lanes = second if hh else jnp.logical_not(second)
        qm = jnp.where(head_lanes, q2, jnp.zeros_like(q2))
        pw, _, l = _probs(*_scores(qm, k, None, None))
        outs.append(_pv(pw, None, l, v, None))
    o_ref[0] = jnp.where(second, outs[1], outs[0]).astype(BF16)


def _cattn(pcb3):
    b, lc, _ = pcb3.shape
    kblk, vblk, qblk = K_OFF // LANE, V_OFF // LANE, Q_OFF // LANE
    return pl.pallas_call(
        _cattn_kernel,
        grid=(b, N_HEAD_PAIRS),
        in_specs=[
            pl.BlockSpec((1, lc, HEAD_PAIR_W), lambda bi, hp: (bi, 0, kblk + hp)),
            pl.BlockSpec((1, lc, HEAD_PAIR_W), lambda bi, hp: (bi, 0, vblk + hp)),
            pl.BlockSpec((1, lc, HEAD_PAIR_W), lambda bi, hp: (bi, 0, qblk + hp)),
        ],
        out_specs=pl.BlockSpec((1, lc, HEAD_PAIR_W), lambda bi, hp: (bi, 0, hp)),
        out_shape=jax.ShapeDtypeStruct((b, lc, N_HEADS * HEAD_DIM), BF16),
        compiler_params=_cparams(("parallel", "parallel"), 32),
        name="cattn",
    )(pcb3, pcb3, pcb3)


def _dft_kernel(c_ref, s_ref, *, n):
    tk = c_ref.shape[0]
    k = pl.program_id(0) * tk + lax.broadcasted_iota(I32, (tk, n), 0)
    t = lax.broadcasted_iota(I32, (tk, n), 1)
    m = (k * t) & (n - 1)
    ang = m.astype(F32) * (2.0 * math.pi / n)
    c_ref[...] = jnp.cos(ang).astype(BF16)
    s_ref[...] = jnp.sin(ang).astype(BF16)


def _dft_mats(n):
    tk = min(n, 256)
    return pl.pallas_call(
        functools.partial(_dft_kernel, n=n),
        grid=(n // tk,),
        in_specs=[],
        out_specs=[pl.BlockSpec((tk, n), lambda i: (i, 0))] * 2,
        out_shape=[jax.ShapeDtypeStruct((n, n), BF16)] * 2,
        compiler_params=_cparams(("parallel",), 48),
        name=f"dft_mats_{n}",
    )()


def _fourier_kernel(u_ref, cc_ref, sc_ref, cl_ref, sl_ref, o_ref, a_ref, b_ref, *, seq):
    @pl.when(pl.program_id(1) == 0)
    def _():
        for g in range(FOURIER_GROUPS):
            sl = slice(g * FOURIER_GROUP_W, (g + 1) * FOURIER_GROUP_W)
            ug = u_ref[0, :, sl]
            a_ref[:, sl] = jnp.dot(ug, cc_ref[...], preferred_element_type=F32).astype(BF16)
            b_ref[:, sl] = jnp.dot(ug, sc_ref[...], preferred_element_type=F32).astype(BF16)

    y = (jnp.dot(cl_ref[...], a_ref[...], preferred_element_type=F32)
         - jnp.dot(sl_ref[...], b_ref[...], preferred_element_type=F32))
    o_ref[0] = (y * (1.0 / math.sqrt(seq * FOURIER_GROUP_W))).astype(BF16)


def _fourier(pb3, cc, sc, cl, sl):
    b, seq, _ = pb3.shape
    tk = min(seq, 512)
    return pl.pallas_call(
        functools.partial(_fourier_kernel, seq=seq),
        grid=(b, seq // tk),
        in_specs=[
            pl.BlockSpec((1, seq, FOURIER_W), lambda bi, k: (bi, 0, F_OFF // FOURIER_W)),
            pl.BlockSpec((FOURIER_GROUP_W, FOURIER_GROUP_W), lambda bi, k: (0, 0)),
            pl.BlockSpec((FOURIER_GROUP_W, FOURIER_GROUP_W), lambda bi, k: (0, 0)),
            pl.BlockSpec((tk, seq), lambda bi, k: (k, 0)),
            pl.BlockSpec((tk, seq), lambda bi, k: (k, 0)),
        ],
        out_specs=pl.BlockSpec((1, tk, FOURIER_W), lambda bi, k: (bi, k, 0)),
        out_shape=jax.ShapeDtypeStruct((b, seq, FOURIER_W), BF16),
        scratch_shapes=[pltpu.VMEM((seq, FOURIER_W), BF16), pltpu.VMEM((seq, FOURIER_W), BF16)],
        compiler_params=_cparams(("parallel", "arbitrary"), 48),
        name="fourier",
    )(pb3, cc, sc, cl, sl)


POOL_PAD = 8


def _pool_kernel(u_ref, o_ref, pad_ref, *, seq):
    t = lax.broadcasted_iota(I32, (seq, POOL_GROUP_W), 0)
    zeros = jnp.zeros((POOL_PAD, POOL_GROUP_W), F32)
    pad_ref[0:POOL_PAD, :] = zeros
    pad_ref[seq + POOL_PAD:seq + 2 * POOL_PAD, :] = zeros
    pad_ref[POOL_PAD:seq + POOL_PAD, :] = u_ref[0]
    for g, w in enumerate(POOL_WINDOWS):
        @pl.when(pl.program_id(1) == g)
        def _(w=w):
            acc = None
            for d in range(-(w // 2), w - w // 2):
                term = pad_ref[pl.ds(POOL_PAD + d, seq), :]
                acc = term if acc is None else acc + term
            cnt = (jnp.minimum(t + (w - w // 2), seq) - jnp.maximum(t - w // 2, 0)).astype(F32)
            o_ref[0] = (acc / cnt - u_ref[0]).astype(BF16)


def _pool(pp3):
    b, seq, _ = pp3.shape
    spec = pl.BlockSpec((1, seq, POOL_GROUP_W), lambda bi, g: (bi, 0, g))
    return pl.pallas_call(
        functools.partial(_pool_kernel, seq=seq),
        grid=(b, len(POOL_WINDOWS)),
        in_specs=[spec],
        out_specs=spec,
        out_shape=jax.ShapeDtypeStruct((b, seq, POOL_W), BF16),
        scratch_shapes=[pltpu.VMEM((seq + 2 * POOL_PAD, POOL_GROUP_W), F32)],
        compiler_params=_cparams(("parallel", "parallel"), 32),
        name="pool",
    )(pp3)


def _merge_kernel(att_ref, four_ref, pool_ref, ga_ref, gf_ref, gp_ref, x_ref, g1_ref,
                  wao_ref, wf_ref, wp_ref, ps_ref, wo_ref, o_ref):
    y_att = jnp.dot(att_ref[0], wao_ref[...], preferred_element_type=F32)
    y_four = jnp.dot(four_ref[0], wf_ref[...], preferred_element_type=F32)
    pooled = pool_ref[0]
    y_pool = jnp.concatenate(
        [jnp.dot(pooled[:, g * POOL_GROUP_W:(g + 1) * POOL_GROUP_W], wp_ref[g], preferred_element_type=F32)
         for g in range(len(POOL_WINDOWS))], axis=-1) * ps_ref[...]
    merged = (jax.nn.sigmoid(ga_ref[0].astype(F32)) * y_att
              + jax.nn.sigmoid(gf_ref[0].astype(F32)) * y_four
              + jax.nn.sigmoid(gp_ref[0].astype(F32)) * y_pool)
    y = jnp.dot(merged.astype(BF16), wo_ref[...], preferred_element_type=F32)
    o_ref[0] = x_ref[0] + g1_ref[0] * y


def _merge(att, four, pooled, pb3, x3, mod3, wao, wf, wp, ps, wo, *, ctx_row):
    b, seq, _ = x3.shape
    tm = min(seq, 512)
    gblk = G_OFF // D_MODEL
    mrow = (lambda bi: bi) if ctx_row is None else (lambda bi: ctx_row)
    tok = lambda w: pl.BlockSpec((1, tm, w), lambda bi, i: (bi, i, 0))
    gate = lambda k: pl.BlockSpec((1, tm, D_MODEL), lambda bi, i: (bi, i, gblk + k))
    full = lambda a: pl.BlockSpec(a.shape, lambda bi, i: (0,) * a.ndim)
    return pl.pallas_call(
        _merge_kernel,
        grid=(b, seq // tm),
        in_specs=[
            tok(D_MODEL), tok(FOURIER_W), tok(POOL_W), gate(0), gate(1), gate(2), tok(D_MODEL),
            pl.BlockSpec((1, 1, D_MODEL), lambda bi, i: (mrow(bi), 0, 2)),
            full(wao), full(wf), full(wp), full(ps), full(wo),
        ],
        out_specs=tok(D_MODEL),
        out_shape=jax.ShapeDtypeStruct((b, seq, D_MODEL), F32),
        compiler_params=_cparams(("parallel", "parallel"), 48),
        name="merge",
    )(att, four, pooled, pb3, pb3, pb3, x3, mod3, wao, wf, wp, ps, wo)


def _router_kernel(x_ref, sh_ref, sc_ref, g_ref, wr_ref, h_ref, lg_ref):
    h = _modulate(x_ref[0], g_ref[...], sh_ref[0], sc_ref[0])
    h_ref[0] = h.astype(BF16)
    lg_ref[0] = lax.dot_general(wr_ref[...], h, (((1,), (1,)), ((), ())),
                                precision=HIGHEST, preferred_element_type=F32)


def _router(x3, mod3, gain, wr_t, *, ctx_row):
    b, seq, _ = x3.shape
    tm = min(seq, 512)
    mrow = (lambda bi: bi) if ctx_row is None else (lambda bi: ctx_row)
    return pl.pallas_call(
        _router_kernel,
        grid=(b, seq // tm),
        in_specs=[
            pl.BlockSpec((1, tm, D_MODEL), lambda bi, i: (bi, i, 0)),
            pl.BlockSpec((1, 1, D_MODEL), lambda bi, i: (mrow(bi), 0, 3)),
            pl.BlockSpec((1, 1, D_MODEL), lambda bi, i: (mrow(bi), 0, 4)),
            pl.BlockSpec((1, D_MODEL), lambda bi, i: (0, 0)),
            pl.BlockSpec((N_EXPERTS, D_MODEL), lambda bi, i: (0, 0)),
        ],
        out_specs=[
            pl.BlockSpec((1, tm, D_MODEL), lambda bi, i: (bi, i, 0)),
            pl.BlockSpec((1, N_EXPERTS, tm), lambda bi, i: (bi, 0, i)),
        ],
        out_shape=[
            jax.ShapeDtypeStruct((b, seq, D_MODEL), BF16),
            jax.ShapeDtypeStruct((b, N_EXPERTS, seq), F32),
        ],
        compiler_params=_cparams(("parallel", "parallel"), 32),
        name="router",
    )(x3, mod3, mod3, gain, wr_t)


def _exclusive_prefix(mask, tri):
    e, seq = mask.shape
    ones = jnp.where(mask, 1.0, 0.0)
    offs = jnp.zeros((e, 1), F32)
    pieces = []
    for k in range(seq // LANE):
        blk = ones[:, k * LANE:(k + 1) * LANE]
        local = jnp.dot(blk.astype(BF16), tri, preferred_element_type=F32)
        pieces.append(local + offs)
        offs = offs + jnp.sum(blk, axis=1, keepdims=True)
    return jnp.concatenate(pieces, axis=1)


META_TSTART, META_TEND, META_PSTART, META_PEND, META_W = 0, 4, 8, 12, 16
GATHER_SLOT_BLOCK = 128
GATHER_WIN = 1408
GATHER_ALIGN = 128
COMBINE_TILE = 1024
COMBINE_WIN = 192
COMBINE_ALIGN = 16


def _select_kernel(lg_ref, slot_ref, aff_ref, meta_ref, *, cap):
    z = lg_ref[0]
    z = z - jnp.max(z, axis=0, keepdims=True)
    ez = jnp.exp(z)
    a = ez / jnp.sum(ez, axis=0, keepdims=True)
    aff_ref[0] = a
    capf = float(cap)

    def bisect(_, lohi):
        lo, hi = lohi
        mid = (lo + hi) * 0.5
        cnt = jnp.sum(jnp.where(a >= mid, 1.0, 0.0), axis=1, keepdims=True)
        ge = cnt >= capf
        return jnp.where(ge, mid, lo), jnp.where(ge, hi, mid)

    e = a.shape[0]
    lo, hi = lax.fori_loop(0, SELECT_ITERS, bisect,
                           (jnp.zeros((e, 1), F32), jnp.full((e, 1), 2.0, F32)))
    r_i = lax.broadcasted_iota(I32, (LANE, LANE), 0)
    c_i = lax.broadcasted_iota(I32, (LANE, LANE), 1)
    tri = jnp.where(r_i < c_i, 1.0, 0.0).astype(BF16)
    above = a >= hi
    n_above = jnp.sum(jnp.where(above, 1.0, 0.0), axis=1, keepdims=True)
    tied = (a >= lo) & jnp.logical_not(above)
    tie_rank = _exclusive_prefix(tied, tri)
    sel = above | (tied & (tie_rank < capf - n_above))
    pos = _exclusive_prefix(sel, tri)
    slot_ref[0] = jnp.where(sel, pos, -1.0).astype(I32)

    seq = a.shape[1]
    t = lax.broadcasted_iota(I32, (e, seq), 1).astype(F32)
    lane = lax.broadcasted_iota(I32, (e, LANE), 1)
    meta = jnp.zeros((e, LANE), F32)
    sb = min(GATHER_SLOT_BLOCK, cap)
    for s in range(cap // sb):
        first = jnp.min(jnp.where(sel & (pos >= float(s * sb)), t, float(seq)), axis=1, keepdims=True)
        last = jnp.max(jnp.where(sel & (pos < float((s + 1) * sb)), t, -1.0), axis=1, keepdims=True)
        meta = jnp.where(lane == META_TSTART + s, first, meta)
        meta = jnp.where(lane == META_TEND + s, last, meta)
    tt = min(COMBINE_TILE, seq)
    for i in range(seq // tt):
        before = jnp.sum(jnp.where(sel & (t < float(i * tt)), 1.0, 0.0), axis=1, keepdims=True)
        upto = jnp.sum(jnp.where(sel & (t < float((i + 1) * tt)), 1.0, 0.0), axis=1, keepdims=True)
        meta = jnp.where(lane == META_PSTART + i, before, meta)
        meta = jnp.where(lane == META_PEND + i, upto, meta)
    meta_ref[0] = meta.astype(I32)


def _select(logits_t, cap):
    b, e, seq = logits_t.shape
    assert cap // min(GATHER_SLOT_BLOCK, cap) <= 4 and seq // min(COMBINE_TILE, seq) <= 4
    spec = pl.BlockSpec((1, e, seq), lambda bi: (bi, 0, 0))
    mspec = pl.BlockSpec((1, e, LANE), lambda bi: (bi, 0, 0))
    return pl.pallas_call(
        functools.partial(_select_kernel, cap=cap),
        grid=(b,),
        in_specs=[spec],
        out_specs=[spec, spec, mspec],
        out_shape=[jax.ShapeDtypeStruct((b, e, seq), I32), jax.ShapeDtypeStruct((b, e, seq), F32),
                   jax.ShapeDtypeStruct((b, e, LANE), I32)],
        compiler_params=_cparams(("parallel",), 32),
        name="select",
    )(logits_t)


def _align_down(v, align):
    shift = align.bit_length() - 1
    return lax.shift_left(lax.shift_right_logical(v, shift), shift)


def _gather_kernel(meta_ref, h_ref, slot_ref, aff_ref, xg_ref, gate_ref, *, cap, sb, win):
    seq = h_ref.shape[1]
    base = (pl.program_id(0) * N_EXPERTS + pl.program_id(1)) * META_W

    def gather_block(s, start, width):
        tok = pl.ds(start, width)
        j = lax.broadcasted_iota(I32, (sb, width), 0) + s * sb
        hit = slot_ref[0, 0, :, tok] == j
        onehot = jnp.where(hit, 1.0, 0.0).astype(BF16)
        rows = slice(s * sb, (s + 1) * sb)
        xg_ref[0, 0, rows, :] = jnp.dot(onehot, h_ref[0, tok, :], preferred_element_type=F32).astype(BF16)
        gate_ref[0, 0, rows, :] = jnp.sum(jnp.where(hit, aff_ref[0, 0, :, tok], 0.0), axis=1, keepdims=True)

    for s in range(cap // sb):
        if win >= seq:
            gather_block(s, 0, seq)
            continue
        first = meta_ref[base + META_TSTART + s]
        last = meta_ref[base + META_TEND + s]
        start = pl.multiple_of(jnp.minimum(_align_down(first, GATHER_ALIGN), seq - win), GATHER_ALIGN)
        fits = last < start + win

        @pl.when(fits)
        def _(s=s, start=start):
            gather_block(s, start, win)

        @pl.when(jnp.logical_not(fits))
        def _(s=s):
            gather_block(s, 0, seq)


def _gather(meta, h2, slots4, aff4, cap):
    b, seq, _ = h2.shape
    e = slots4.shape[1]
    sb = min(GATHER_SLOT_BLOCK, cap)
    win = GATHER_WIN if seq > GATHER_WIN else seq
    assert (seq - win) % GATHER_ALIGN == 0
    row = pl.BlockSpec((1, 1, 1, seq), lambda bi, ei, m: (bi, ei, 0, 0))
    return pl.pallas_call(
        functools.partial(_gather_kernel, cap=cap, sb=sb, win=win),
        grid_spec=pltpu.PrefetchScalarGridSpec(
            num_scalar_prefetch=1,
            grid=(b, e),
            in_specs=[pl.BlockSpec((1, seq, D_MODEL), lambda bi, ei, m: (bi, 0, 0)), row, row],
            out_specs=[
                pl.BlockSpec((1, 1, cap, D_MODEL), lambda bi, ei, m: (ei, bi, 0, 0)),
                pl.BlockSpec((1, 1, cap, 1), lambda bi, ei, m: (ei, bi, 0, 0)),
            ],
        ),
        out_shape=[
            jax.ShapeDtypeStruct((e, b, cap, D_MODEL), BF16),
            jax.ShapeDtypeStruct((e, b, cap, 1), F32),
        ],
        compiler_params=_cparams(("parallel", "arbitrary"), 48),
        name="gather",
    )(meta, h2, slots4, aff4)


FF_CHUNK = 256
FFN_ROW_TILES = 2


def _ffn_kernel(x_ref, gate_ref, wg_ref, wu_ref, wd_ref, o_ref, acc_ref):
    fc = pl.program_id(2)

    @pl.when(fc == 0)
    def _():
        acc_ref[...] = jnp.zeros_like(acc_ref)

    x = x_ref[0]
    a = jnp.dot(x, wg_ref[0, 0].astype(BF16), preferred_element_type=F32)
    u = jnp.dot(x, wu_ref[0, 0].astype(BF16), preferred_element_type=F32)
    hmid = (a * jax.nn.sigmoid(a) * u).astype(BF16)
    acc_ref[...] += jnp.dot(hmid, wd_ref[0, 0].astype(BF16), preferred_element_type=F32)

    @pl.when(fc == pl.num_programs(2) - 1)
    def _():
        o_ref[0] = (acc_ref[...] * gate_ref[0]).astype(BF16)


def _ffn(xg3, gate3, wg_all, wu_all, wd_all, layer):
    e, m, _ = xg3.shape
    tm = m // FFN_ROW_TILES if m >= 1024 else m
    assert m % tm == 0 and tm % 16 == 0
    return pl.pallas_call(
        _ffn_kernel,
        grid=(e, m // tm, EXPERT_FF // FF_CHUNK),
        in_specs=[
            pl.BlockSpec((1, tm, D_MODEL), lambda ei, i, fc: (ei, i, 0)),
            pl.BlockSpec((1, tm, 1), lambda ei, i, fc: (ei, i, 0)),
            pl.BlockSpec((1, 1, D_MODEL, FF_CHUNK), lambda ei, i, fc: (layer, ei, 0, fc)),
            pl.BlockSpec((1, 1, D_MODEL, FF_CHUNK), lambda ei, i, fc: (layer, ei, 0, fc)),
            pl.BlockSpec((1, 1, FF_CHUNK, D_MODEL), lambda ei, i, fc: (layer, ei, fc, 0)),
        ],
        out_specs=pl.BlockSpec((1, tm, D_MODEL), lambda ei, i, fc: (ei, i, 0)),
        out_shape=jax.ShapeDtypeStruct((e, m, D_MODEL), BF16),
        scratch_shapes=[pltpu.VMEM((tm, D_MODEL), F32)],
        compiler_params=_cparams(("parallel", "parallel", "arbitrary"), 56),
        name="ffn",
    )(xg3, gate3, wg_all, wu_all, wd_all)


COMBINE_EXPERTS = 4


def _combine_kernel(meta_ref, slot_ref, gy_ref, x_ref, g2_ref, gf_ref, o_ref, acc_ref, *, cap, win, final):
    bi = pl.program_id(0)
    ti = pl.program_id(1)
    ec = pl.program_id(2)
    tt = x_ref.shape[1]
    tn = (((0,), (0,)), ((), ()))

    @pl.when(ec == 0)
    def _():
        acc_ref[...] = jnp.zeros_like(acc_ref)

    def scatter(starts, width):
        j = lax.broadcasted_iota(I32, (width, tt), 0)
        onehot = jnp.concatenate(
            [jnp.where(slot_ref[0, k] == j + starts[k], 1.0, 0.0).astype(BF16) for k in range(COMBINE_EXPERTS)],
            axis=0)
        gy = jnp.concatenate([gy_ref[k, pl.ds(starts[k], width), :] for k in range(COMBINE_EXPERTS)], axis=0)
        acc_ref[...] += lax.dot_general(onehot, gy, tn, preferred_element_type=F32)

    if win >= cap:
        scatter([0] * COMBINE_EXPERTS, cap)
    else:
        starts = []
        fits = None
        for k in range(COMBINE_EXPERTS):
            base = (bi * N_EXPERTS + ec * COMBINE_EXPERTS + k) * META_W
            before = meta_ref[base + META_PSTART + ti]
            upto = meta_ref[base + META_PEND + ti]
            start = pl.multiple_of(jnp.minimum(_align_down(before, COMBINE_ALIGN), cap - win), COMBINE_ALIGN)
            starts.append(start)
            ok = upto <= start + win
            fits = ok if fits is None else jnp.logical_and(fits, ok)

        @pl.when(fits)
        def _():
            scatter(starts, win)

        @pl.when(jnp.logical_not(fits))
        def _():
            scatter([0] * COMBINE_EXPERTS, cap)

    @pl.when(ec == pl.num_programs(2) - 1)
    def _():
        xn = x_ref[0] + g2_ref[0] * acc_ref[...]
        if final:
            ms = jnp.mean(xn * xn, axis=-1, keepdims=True)
            xn = (xn * lax.rsqrt(ms + RMS_EPS)) * gf_ref[...]
        o_ref[0] = xn


def _combine(meta, slots4, gy3, row_off, x3, mod3, gfinal, *, cap, ctx_row, final):
    b, seq, _ = x3.shape
    e = slots4.shape[1]
    tt = min(seq, COMBINE_TILE)
    win = COMBINE_WIN if cap > COMBINE_WIN else cap
    assert row_off % cap == 0 and (cap - win) % COMBINE_ALIGN == 0
    blk_off = row_off // cap
    mrow = (lambda bi: bi) if ctx_row is None else (lambda bi: ctx_row)
    return pl.pallas_call(
        functools.partial(_combine_kernel, cap=cap, win=win, final=final),
        grid_spec=pltpu.PrefetchScalarGridSpec(
            num_scalar_prefetch=1,
            grid=(b, seq // tt, e // COMBINE_EXPERTS),
            in_specs=[
                pl.BlockSpec((1, COMBINE_EXPERTS, 1, tt), lambda bi, i, ec, m: (bi, ec, 0, i)),
                pl.BlockSpec((COMBINE_EXPERTS, cap, D_MODEL), lambda bi, i, ec, m: (ec, blk_off + bi, 0)),
                pl.BlockSpec((1, tt, D_MODEL), lambda bi, i, ec, m: (bi, i, 0)),
                pl.BlockSpec((1, 1, D_MODEL), lambda bi, i, ec, m: (mrow(bi), 0, 5)),
                pl.BlockSpec((1, D_MODEL), lambda bi, i, ec, m: (0, 0)),
            ],
            out_specs=pl.BlockSpec((1, tt, D_MODEL), lambda bi, i, ec, m: (bi, i, 0)),
            scratch_shapes=[pltpu.VMEM((tt, D_MODEL), F32)],
        ),
        out_shape=jax.ShapeDtypeStruct((b, seq, D_MODEL), F32),
        compiler_params=_cparams(("parallel", "parallel", "arbitrary"), 48),
        name="combine",
    )(meta, slots4, gy3, x3, mod3, gfinal)


def _moe_route(x3, mod3, gain2, wr_t, *, ctx_row):
    b, seq, _ = x3.shape
    cap = EC_CAPACITY_FACTOR * seq // N_EXPERTS
    h2, logits_t = _router(x3, mod3, gain2, wr_t, ctx_row=ctx_row)
    slots, aff, meta = _select(logits_t, cap)
    meta = meta[:, :, :META_W].reshape(-1)
    slots4 = slots.reshape(b, N_EXPERTS, 1, seq)
    aff4 = aff.reshape(b, N_EXPERTS, 1, seq)
    xg, gate = _gather(meta, h2, slots4, aff4, cap)
    return meta, slots4, xg.reshape(N_EXPERTS, b * cap, D_MODEL), gate.reshape(N_EXPERTS, b * cap, 1), cap


def kernel(x, c, ctx, c_ctx, ada_w, ada_b, norm1_g, norm2_g, w_in, rpb, w_att_o, w_fourier, w_pool,
           pool_scale, w_out, w_router, w_exp_gate, w_exp_up, w_exp_down, final_norm_g):
    b, seq, d = x.shape
    lc = ctx.shape[1]
    assert d == D_MODEL and seq % (GRID_W * ATT_QROWS) == 0 and b + 1 <= MOD_ROWS
    rows = seq // GRID_W
    ctx_row = b

    cond = jnp.concatenate([c, c_ctx[None, :], jnp.zeros((MOD_ROWS - b - 1, d), F32)], axis=0)
    cl, sl = _dft_mats(seq)
    clc, slc = _dft_mats(lc)
    cc, sc = _dft_mats(FOURIER_GROUP_W)
    gfinal = final_norm_g.reshape(1, d)

    for i in range(DEPTH):
        update_ctx = i < DEPTH - 1
        mod3 = _adaln(cond, ada_w[i], ada_b[i]).reshape(MOD_ROWS, 1, 6 * d)
        g1n = norm1_g[i].reshape(1, d)
        g2n = norm2_g[i].reshape(1, d)
        w_in_b = w_in[i].astype(BF16)
        wao = w_att_o[i].astype(BF16)
        wf = w_fourier[i].astype(BF16)
        wp = w_pool[i].astype(BF16)
        ps = pool_scale[i].reshape(1, d)
        wo = w_out[i].astype(BF16)
        wr_t = w_router[i].T
        bias = _bias_table(rpb[i], rows)

        pb, pp = _modproj(x.reshape(b * seq, d), mod3, g1n, w_in_b, seq=seq, ctx_row=None,
                          n_out=IN_W, with_pool=True)
        n_ctx = IN_W if update_ctx else Q_OFF
        pcb, pcp = _modproj(ctx.reshape(b * lc, d), mod3, g1n, w_in_b[:, :n_ctx], seq=lc, ctx_row=ctx_row,
                            n_out=n_ctx, with_pool=update_ctx)
        pb3 = pb.reshape(b, seq, IN_W)
        pcb3 = pcb.reshape(b, lc, n_ctx)

        att = _nattn(pb3, pcb3, bias)
        four = _fourier(pb3, cc, sc, cl, sl)
        pooled = _pool(pp.reshape(b, seq, POOL_W))
        x = _merge(att, four, pooled, pb3, x, mod3, wao, wf, wp, ps, wo, ctx_row=None)
        meta_x, slots_x, xg, gate, cap = _moe_route(x, mod3, g2n, wr_t, ctx_row=None)

        if update_ctx:
            att_c = _cattn(pcb3)
            four_c = _fourier(pcb3, cc, sc, clc, slc)
            pooled_c = _pool(pcp.reshape(b, lc, POOL_W))
            ctx = _merge(att_c, four_c, pooled_c, pcb3, ctx, mod3, wao, wf, wp, ps, wo, ctx_row=ctx_row)
            meta_c, slots_c, xg_c, gate_c, cap_c = _moe_route(ctx, mod3, g2n, wr_t, ctx_row=ctx_row)
            xg = jnp.concatenate([xg, xg_c], axis=1)
            gate = jnp.concatenate([gate, gate_c], axis=1)

        gy = _ffn(xg, gate, w_exp_gate, w_exp_up, w_exp_down, i)
        if update_ctx:
            ctx = _combine(meta_c, slots_c, gy, b * cap, ctx, mod3, gfinal, cap=cap_c, ctx_row=ctx_row,
                           final=False)
        x = _combine(meta_x, slots_x, gy, 0, x, mod3, gfinal, cap=cap, ctx_row=None, final=not update_ctx)
    return x
```

```python
import functools
import math

import jax
import jax.numpy as jnp
from jax import lax
from jax.experimental import pallas as pl
from jax.experimental.pallas import tpu as pltpu

F32 = jnp.float32
BF16 = jnp.bfloat16
I32 = jnp.int32
HIGHEST = lax.Precision.HIGHEST

D_MODEL = 1024
DEPTH = 2
GRID_W = 64
N_HEADS = 16
HEAD_DIM = 64
WIN_R = 8
WIN_C = 16
FOURIER_GROUPS = 4
FOURIER_GROUP_W = 128
FOURIER_W = 512
POOL_WINDOWS = (2, 4, 8, 16)
POOL_GROUP_W = 128
POOL_W = 512
POOL_OUT_GROUP = 256
K_OFF, V_OFF, Q_OFF, F_OFF, P_OFF, G_OFF = 0, 1024, 2048, 3072, 3584, 4096
IN_W = 7168
N_EXPERTS = 16
EC_CAPACITY_FACTOR = 2
EXPERT_FF = 2816
RMS_EPS = 1e-6

LANE = 128
HEAD_PAIR_W = 2 * HEAD_DIM
N_HEAD_PAIRS = N_HEADS // 2
MOD_ROWS = 16
NEG_BIG = -1e30
ATT_QROWS = 2
ATT_KROWS = 10
ATT_CLASSES = 5
SELECT_ITERS = 64
MiB = 1024 * 1024


def _cparams(sem, vmem_mib):
    return pltpu.CompilerParams(dimension_semantics=sem, vmem_limit_bytes=vmem_mib * MiB)


def _adaln_kernel(c_ref, w_ref, b_ref, o_ref):
    c = c_ref[...]
    s = c * jax.nn.sigmoid(c)
    o_ref[...] = jnp.dot(s, w_ref[...], precision=HIGHEST, preferred_element_type=F32) + b_ref[...]


def _adaln(cond_rows, ada_w, ada_b):
    n = ada_w.shape[1]
    tn = 1024
    return pl.pallas_call(
        _adaln_kernel,
        grid=(n // tn,),
        in_specs=[
            pl.BlockSpec((MOD_ROWS, D_MODEL), lambda j: (0, 0)),
            pl.BlockSpec((D_MODEL, tn), lambda j: (0, j)),
            pl.BlockSpec((1, tn), lambda j: (0, j)),
        ],
        out_specs=pl.BlockSpec((MOD_ROWS, tn), lambda j: (0, j)),
        out_shape=jax.ShapeDtypeStruct((MOD_ROWS, n), F32),
        compiler_params=_cparams(("arbitrary",), 32),
        name="adaln",
    )(cond_rows, ada_w, ada_b.reshape(1, n))


def _modulate(x, g, shift, scale):
    ms = jnp.mean(x * x, axis=-1, keepdims=True)
    y = x * lax.rsqrt(ms + RMS_EPS)
    return (y * g) * (1.0 + scale) + shift


def _modproj_kernel(x_ref, sh_ref, sc_ref, g_ref, w_ref, *rest, pool_tile):
    if pool_tile is None:
        o_ref, h_ref = rest
        pp_ref = None
    else:
        o_ref, pp_ref, h_ref = rest
    j = pl.program_id(1)

    @pl.when(j == 0)
    def _():
        h = _modulate(x_ref[...], g_ref[...], sh_ref[0], sc_ref[0])
        h_ref[...] = h.astype(BF16)

    acc = jnp.dot(h_ref[...], w_ref[...], preferred_element_type=F32)
    o_ref[...] = acc.astype(BF16)
    if pool_tile is not None:
        @pl.when(j == pool_tile)
        def _():
            pp_ref[...] = acc[:, P_OFF % 1024:]


def _modproj(x2, mod3, gain, w, *, seq, ctx_row, n_out, with_pool):
    rows = x2.shape[0]
    tm = min(seq, 1024)
    tn = 1024
    tiles_per_seq = seq // tm
    if ctx_row is None:
        mrow = lambda i: i // tiles_per_seq
    else:
        mrow = lambda i: ctx_row
    pool_tile = (P_OFF // tn) if with_pool else None
    out_shape = [jax.ShapeDtypeStruct((rows, n_out), BF16)]
    out_specs = [pl.BlockSpec((tm, tn), lambda i, j: (i, j))]
    if with_pool:
        out_shape.append(jax.ShapeDtypeStruct((rows, POOL_W), F32))
        out_specs.append(pl.BlockSpec((tm, POOL_W), lambda i, j: (i, 0)))
    res = pl.pallas_call(
        functools.partial(_modproj_kernel, pool_tile=pool_tile),
        grid=(rows // tm, n_out // tn),
        in_specs=[
            pl.BlockSpec((tm, D_MODEL), lambda i, j: (i, 0)),
            pl.BlockSpec((1, 1, D_MODEL), lambda i, j: (mrow(i), 0, 0)),
            pl.BlockSpec((1, 1, D_MODEL), lambda i, j: (mrow(i), 0, 1)),
            pl.BlockSpec((1, D_MODEL), lambda i, j: (0, 0)),
            pl.BlockSpec((D_MODEL, tn), lambda i, j: (0, j)),
        ],
        out_specs=out_specs,
        out_shape=out_shape,
        scratch_shapes=[pltpu.VMEM((tm, D_MODEL), BF16)],
        compiler_params=_cparams(("parallel", "arbitrary"), 48),
        name="modproj",
    )(x2, mod3, mod3, gain, w)
    return res if with_pool else (res[0], None)


def _bias_kernel(rpb_ref, o_ref, *, rows):
    h = pl.program_id(0)
    cls = pl.program_id(1)
    i_rep = jnp.where(cls < 3, cls, cls + (rows // ATT_QROWS - ATT_CLASSES))
    s = jnp.clip(ATT_QROWS * i_rep - WIN_R // 2, 0, rows - ATT_KROWS)
    qc = lax.broadcasted_iota(I32, (GRID_W, LANE), 0)
    lane = lax.broadcasted_iota(I32, (GRID_W, LANE), 1)
    kc = lane & (GRID_W - 1)
    first_half = lane < GRID_W
    cs = jnp.clip(qc - WIN_C // 2, 0, GRID_W - WIN_C)
    col_valid = (kc >= cs) & (kc < cs + WIN_C)
    dcol = kc - qc + (WIN_C - 1)
    n_coff = 2 * WIN_C - 1
    n_roff = 2 * WIN_R - 1
    for ri in range(ATT_QROWS):
        r = ATT_QROWS * i_rep + ri
        rs = jnp.clip(r - WIN_R // 2, 0, rows - WIN_R)
        for m in range(ATT_KROWS // 2):
            krow_a = s + 2 * m
            krow_b = krow_a + 1
            va = ((krow_a >= rs) & (krow_a < rs + WIN_R)).astype(I32)
            vb = ((krow_b >= rs) & (krow_b < rs + WIN_R)).astype(I32)
            base_a = (h * n_roff + jnp.clip(krow_a - r + WIN_R - 1, 0, n_roff - 1)) * n_coff
            base_b = (h * n_roff + jnp.clip(krow_b - r + WIN_R - 1, 0, n_roff - 1)) * n_coff
            acc = jnp.zeros((GRID_W, LANE), F32)
            for c in range(n_coff):
                val = jnp.where(first_half, rpb_ref[base_a + c], rpb_ref[base_b + c])
                acc = jnp.where(dcol == c, val, acc)
            row_valid = jnp.where(first_half, va, vb) > 0
            tile = jnp.where(col_valid & row_valid, acc, NEG_BIG)
            o_ref[0, 0, ri * GRID_W:(ri + 1) * GRID_W, m * LANE:(m + 1) * LANE] = tile


def _bias_table(rpb, rows):
    nq = ATT_QROWS * GRID_W
    nk = ATT_KROWS * GRID_W
    return pl.pallas_call(
        functools.partial(_bias_kernel, rows=rows),
        grid_spec=pltpu.PrefetchScalarGridSpec(
            num_scalar_prefetch=1,
            grid=(N_HEADS, ATT_CLASSES),
            in_specs=[],
            out_specs=pl.BlockSpec((1, 1, nq, nk), lambda h, c, rpb: (h, c, 0, 0)),
        ),
        out_shape=jax.ShapeDtypeStruct((N_HEADS, ATT_CLASSES, nq, nk), F32),
        compiler_params=_cparams(("arbitrary", "arbitrary"), 32),
        name="bias_table",
    )(rpb.reshape(-1))


_NT = (((1,), (1,)), ((), ()))


def _scores(qm, kw, kc, bias):
    sw = lax.dot_general(qm, kw, _NT, preferred_element_type=F32)
    if bias is not None:
        sw = sw + bias
    sc = lax.dot_general(qm, kc, _NT, preferred_element_type=F32) if kc is not None else None
    return sw, sc


def _probs(sw, sc):
    m = jnp.max(sw, axis=-1, keepdims=True)
    if sc is not None:
        m = jnp.maximum(m, jnp.max(sc, axis=-1, keepdims=True))
    pw = jnp.exp(sw - m)
    l = jnp.sum(pw, axis=-1, keepdims=True)
    pc = None
    if sc is not None:
        pc = jnp.exp(sc - m)
        l = l + jnp.sum(pc, axis=-1, keepdims=True)
        pc = pc.astype(BF16)
    return pw.astype(BF16), pc, l


def _pv(pw, pc, l, vw, vc):
    o = jnp.dot(pw, vw, preferred_element_type=F32)
    if pc is not None:
        o = o + jnp.dot(pc, vc, preferred_element_type=F32)
    return o / l


ATT_UNROLL = 4


def _nattn_kernel(k_ref, v_ref, q_ref, kc_ref, vc_ref, b_ref, o_ref, *, rows):
    nq = ATT_QROWS * GRID_W
    nk = ATT_KROWS * GRID_W
    n_steps = rows // ATT_QROWS
    lane = lax.broadcasted_iota(I32, (nq, HEAD_PAIR_W), 1)
    second = lane >= HEAD_DIM
    scale = HEAD_DIM ** -0.5

    def body(ii, carry):
        kc = kc_ref[0]
        vc = vc_ref[0]
        chains = []
        for u in range(ATT_UNROLL):
            i = ii * ATT_UNROLL + u
            s = jnp.clip(ATT_QROWS * i - WIN_R // 2, 0, rows - ATT_KROWS)
            kstart = pl.multiple_of(s * GRID_W, LANE)
            qstart = pl.multiple_of(i * nq, LANE)
            cls = jnp.where(i < 2, i, jnp.where(i > n_steps - 3, i - (n_steps - ATT_CLASSES), 2))
            q2 = (q_ref[0, pl.ds(qstart, nq), :].astype(F32) * scale).astype(BF16)
            kw = k_ref[0, pl.ds(kstart, nk), :]
            for hh in range(2):
                head_lanes = second if hh else jnp.logical_not(second)
                qm = jnp.where(head_lanes, q2, jnp.zeros_like(q2))
                chains.append((u, kstart, qstart) + _scores(qm, kw, kc, b_ref[hh, cls]))
        probs = [_probs(sw, sc) for (_, _, _, sw, sc) in chains]
        outs = []
        for (u, kstart, qstart, _, _), (pw, pc, l) in zip(chains, probs):
            outs.append(_pv(pw, pc, l, v_ref[0, pl.ds(kstart, nk), :], vc))
        for u in range(ATT_UNROLL):
            qstart = chains[2 * u][2]
            o = jnp.where(second, outs[2 * u + 1], outs[2 * u])
            o_ref[0, pl.ds(qstart, nq), :] = o.astype(BF16)
        return carry

    lax.fori_loop(0, n_steps // ATT_UNROLL, body, 0)


def _nattn(pb3, pcb3, bias):
    b, seq, _ = pb3.shape
    lc = pcb3.shape[1]
    rows = seq // GRID_W
    nq = ATT_QROWS * GRID_W
    nk = ATT_KROWS * GRID_W
    kblk, vblk, qblk = K_OFF // LANE, V_OFF // LANE, Q_OFF // LANE
    return pl.pallas_call(
        functools.partial(_nattn_kernel, rows=rows),
        grid=(N_HEAD_PAIRS, b),
        in_specs=[
            pl.BlockSpec((1, seq, HEAD_PAIR_W), lambda hp, bi: (bi, 0, kblk + hp)),
            pl.BlockSpec((1, seq, HEAD_PAIR_W), lambda hp, bi: (bi, 0, vblk + hp)),
            pl.BlockSpec((1, seq, HEAD_PAIR_W), lambda hp, bi: (bi, 0, qblk + hp)),
            pl.BlockSpec((1, lc, HEAD_PAIR_W), lambda hp, bi: (bi, 0, kblk + hp)),
            pl.BlockSpec((1, lc, HEAD_PAIR_W), lambda hp, bi: (bi, 0, vblk + hp)),
            pl.BlockSpec((2, ATT_CLASSES, nq, nk), lambda hp, bi: (hp, 0, 0, 0)),
        ],
        out_specs=pl.BlockSpec((1, seq, HEAD_PAIR_W), lambda hp, bi: (bi, 0, hp)),
        out_shape=jax.ShapeDtypeStruct((b, seq, N_HEADS * HEAD_DIM), BF16),
        compiler_params=_cparams(("parallel", "parallel"), 40),
        name="nattn",
    )(pb3, pb3, pb3, pcb3, pcb3, bias)


def _cattn_kernel(k_ref, v_ref, q_ref, o_ref):
    lc = q_ref.shape[1]
    lane = lax.broadcasted_iota(I32, (lc, HEAD_PAIR_W), 1)
    second = lane >= HEAD_DIM
    q2 = (q_ref[0].astype(F32) * (HEAD_DIM ** -0.5)).astype(BF16)
    k = k_ref[0]
    v = v_ref[0]
    outs = []
    for hh in range(2):
        head_lanes = second if hh else jnp.logical_not(second)
        qm = jnp.where(head_lanes, q2, jnp.zeros_like(q2))
        pw, _, l = _probs(*_scores(qm, k, None, None))
        outs.append(_pv(pw, None, l, v, None))
    o_ref[0] = jnp.where(second, outs[1], outs[0]).astype(BF16)


def _cattn(pcb3):
    b, lc, _ = pcb3.shape
    kblk, vblk, qblk = K_OFF // LANE, V_OFF // LANE, Q_OFF // LANE
    return pl.pallas_call(
        _cattn_kernel,
        grid=(b, N_HEAD_PAIRS),
        in_specs=[
            pl.BlockSpec((1, lc, HEAD_PAIR_W), lambda bi, hp: (bi, 0, kblk + hp)),
            pl.BlockSpec((1, lc, HEAD_PAIR_W), lambda bi, hp: (bi, 0, vblk + hp)),
            pl.BlockSpec((1, lc, HEAD_PAIR_W), lambda bi, hp: (bi, 0, qblk + hp)),
        ],
        out_specs=pl.BlockSpec((1, lc, HEAD_PAIR_W), lambda bi, hp: (bi, 0, hp)),
        out_shape=jax.ShapeDtypeStruct((b, lc, N_HEADS * HEAD_DIM), BF16),
        compiler_params=_cparams(("parallel", "parallel"), 32),
        name="cattn",
    )(pcb3, pcb3, pcb3)


def _dft_kernel(c_ref, s_ref, *, n):
    tk, ncols = c_ref.shape
    k = pl.program_id(0) * tk + lax.broadcasted_iota(I32, (tk, ncols), 0)
    t = lax.broadcasted_iota(I32, (tk, ncols), 1)
    m = (k * t) & (n - 1)
    ang = m.astype(F32) * (2.0 * math.pi / n)
    c_ref[...] = jnp.cos(ang).astype(BF16)
    s_ref[...] = jnp.sin(ang).astype(BF16)


def _dft_mats(n, ncols):
    tk = min(n, 256)
    return pl.pallas_call(
        functools.partial(_dft_kernel, n=n),
        grid=(n // tk,),
        in_specs=[],
        out_specs=[pl.BlockSpec((tk, ncols), lambda i: (i, 0))] * 2,
        out_shape=[jax.ShapeDtypeStruct((n, ncols), BF16)] * 2,
        compiler_params=_cparams(("parallel",), 48),
        name=f"dft_mats_{n}",
    )()


REV_BLOCK = 128


def _fourier_kernel(u_ref, cc_ref, sc_ref, cl_ref, sl_ref, o_ref, us_ref, ud_ref, a_ref, b_ref, ah_ref, *, seq):
    half = seq // 2
    nblk = seq // REV_BLOCK
    tk = o_ref.shape[1]

    @pl.when(pl.program_id(1) == 0)
    def _():
        d_i = lax.broadcasted_iota(I32, (REV_BLOCK, REV_BLOCK), 0)
        s_i = lax.broadcasted_iota(I32, (REV_BLOCK, REV_BLOCK), 1)
        flip = jnp.where((d_i >= 1) & (s_i == REV_BLOCK - d_i), 1.0, 0.0).astype(BF16)
        row = lax.broadcasted_iota(I32, (REV_BLOCK, FOURIER_W), 0)
        for blk in range(nblk // 2):
            lo = u_ref[0, blk * REV_BLOCK:(blk + 1) * REV_BLOCK, :].astype(F32)
            src = u_ref[0, (nblk - 1 - blk) * REV_BLOCK:(nblk - blk) * REV_BLOCK, :]
            rev = jnp.dot(flip, src, preferred_element_type=F32)
            if blk > 0:
                head = u_ref[0, (nblk - blk) * REV_BLOCK:(nblk - blk) * REV_BLOCK + 16, :].astype(F32)
                rev = jnp.where(row == 0, head[0:1, :], rev)
            rows = slice(blk * REV_BLOCK, (blk + 1) * REV_BLOCK)
            us_ref[rows, :] = (lo + rev).astype(BF16)
            ud_ref[rows, :] = (lo - rev).astype(BF16)
        mid = u_ref[0, half:half + 16, :]
        for g in range(FOURIER_GROUPS):
            sl = slice(g * FOURIER_GROUP_W, (g + 1) * FOURIER_GROUP_W)
            a_ref[:, sl] = jnp.dot(us_ref[:, sl], cc_ref[...], preferred_element_type=F32).astype(BF16)
            b_ref[:, sl] = jnp.dot(ud_ref[:, sl], sc_ref[...], preferred_element_type=F32).astype(BF16)
            ah_ref[:, sl] = jnp.dot(mid[:, sl], cc_ref[...], preferred_element_type=F32)

    k = pl.program_id(1) * tk + lax.broadcasted_iota(I32, (tk, 1), 0)
    sign = (1 - 2 * (k & 1)).astype(F32)
    y = (jnp.dot(cl_ref[...], a_ref[...], preferred_element_type=F32)
         - jnp.dot(sl_ref[...], b_ref[...], preferred_element_type=F32)
         + sign * ah_ref[0:1, :])
    o_ref[0] = (y * (1.0 / math.sqrt(seq * FOURIER_GROUP_W))).astype(BF16)


def _fourier(pb3, cc, sc, cl, sl):
    b, seq, _ = pb3.shape
    tk = min(seq, 512)
    half = seq // 2
    assert seq % (2 * REV_BLOCK) == 0
    return pl.pallas_call(
        functools.partial(_fourier_kernel, seq=seq),
        grid=(b, seq // tk),
        in_specs=[
            pl.BlockSpec((1, seq, FOURIER_W), lambda bi, k: (bi, 0, F_OFF // FOURIER_W)),
            pl.BlockSpec((FOURIER_GROUP_W, FOURIER_GROUP_W), lambda bi, k: (0, 0)),
            pl.BlockSpec((FOURIER_GROUP_W, FOURIER_GROUP_W), lambda bi, k: (0, 0)),
            pl.BlockSpec((tk, half), lambda bi, k: (k, 0)),
            pl.BlockSpec((tk, half), lambda bi, k: (k, 0)),
        ],
        out_specs=pl.BlockSpec((1, tk, FOURIER_W), lambda bi, k: (bi, k, 0)),
        out_shape=jax.ShapeDtypeStruct((b, seq, FOURIER_W), BF16),
        scratch_shapes=[pltpu.VMEM((half, FOURIER_W), BF16)] * 4 + [pltpu.VMEM((16, FOURIER_W), F32)],
        compiler_params=_cparams(("parallel", "arbitrary"), 48),
        name="fourier",
    )(pb3, cc, sc, cl, sl)


POOL_PAD = 8


def _pool_kernel(u_ref, o_ref, pad_ref, *, seq):
    t = lax.broadcasted_iota(I32, (seq, POOL_GROUP_W), 0)
    zeros = jnp.zeros((POOL_PAD, POOL_GROUP_W), F32)
    pad_ref[0:POOL_PAD, :] = zeros
    pad_ref[seq + POOL_PAD:seq + 2 * POOL_PAD, :] = zeros
    pad_ref[POOL_PAD:seq + POOL_PAD, :] = u_ref[0]
    for g, w in enumerate(POOL_WINDOWS):
        @pl.when(pl.program_id(1) == g)
        def _(w=w):
            acc = None
            for d in range(-(w // 2), w - w // 2):
                term = pad_ref[pl.ds(POOL_PAD + d, seq), :]
                acc = term if acc is None else acc + term
            cnt = (jnp.minimum(t + (w - w // 2), seq) - jnp.maximum(t - w // 2, 0)).astype(F32)
            o_ref[0] = (acc / cnt - u_ref[0]).astype(BF16)


def _pool(pp3):
    b, seq, _ = pp3.shape
    spec = pl.BlockSpec((1, seq, POOL_GROUP_W), lambda bi, g: (bi, 0, g))
    return pl.pallas_call(
        functools.partial(_pool_kernel, seq=seq),
        grid=(b, len(POOL_WINDOWS)),
        in_specs=[spec],
        out_specs=spec,
        out_shape=jax.ShapeDtypeStruct((b, seq, POOL_W), BF16),
        scratch_shapes=[pltpu.VMEM((seq + 2 * POOL_PAD, POOL_GROUP_W), F32)],
        compiler_params=_cparams(("parallel", "parallel"), 32),
        name="pool",
    )(pp3)


MERGE_PARTS = 4


def _merge_kernel(att_ref, four_ref, pool_ref, ga_ref, gf_ref, gp_ref, x_ref, g1_ref,
                  wao_ref, wf_ref, wp_ref, ps_ref, wo_ref, sh2_ref, sc2_ref, g2n_ref, wr_ref,
                  o_ref, h_ref, lg_ref):
    tm = x_ref.shape[1]
    n_parts = min(MERGE_PARTS, tm // LANE)
    parts = [slice(p * (tm // n_parts), (p + 1) * (tm // n_parts)) for p in range(n_parts)]
    wr = wr_ref[...]
    wr_hi = wr.astype(BF16)
    wr_lo = (wr - wr_hi.astype(F32)).astype(BF16)

    def sigmoid(ref, rows):
        return 0.5 * jnp.tanh(0.5 * ref[0, rows, :].astype(F32)) + 0.5

    branches = []
    for rows in parts:
        y_att = jnp.dot(att_ref[0, rows, :], wao_ref[...], preferred_element_type=F32)
        y_four = jnp.dot(four_ref[0, rows, :], wf_ref[...], preferred_element_type=F32)
        pooled = pool_ref[0, rows, :]
        y_pool = jnp.concatenate(
            [jnp.dot(pooled[:, g * POOL_GROUP_W:(g + 1) * POOL_GROUP_W], wp_ref[g], preferred_element_type=F32)
             for g in range(len(POOL_WINDOWS))], axis=-1) * ps_ref[...]
        branches.append((y_att, y_four, y_pool))
    resid = []
    for rows, (y_att, y_four, y_pool) in zip(parts, branches):
        merged = (sigmoid(ga_ref, rows) * y_att + sigmoid(gf_ref, rows) * y_four + sigmoid(gp_ref, rows) * y_pool)
        y = jnp.dot(merged.astype(BF16), wo_ref[...], preferred_element_type=F32)
        xn = x_ref[0, rows, :] + g1_ref[0] * y
        o_ref[0, rows, :] = xn
        resid.append(xn)
    for p, (rows, xn) in enumerate(zip(parts, resid)):
        h = _modulate(xn, g2n_ref[...], sh2_ref[0], sc2_ref[0])
        h_hi = h.astype(BF16)
        h_ref[0, rows, :] = h_hi
        h_lo = (h - h_hi.astype(F32)).astype(BF16)
        lg = (jnp.dot(h_hi, wr_hi, preferred_element_type=F32) + jnp.dot(h_lo, wr_hi, preferred_element_type=F32)
              + jnp.dot(h_hi, wr_lo, preferred_element_type=F32))
        lg_ref[0, :, rows] = lg.T[:N_EXPERTS, :]


def _merge(att, four, pooled, pb3, x3, mod3, wao, wf, wp, ps, wo, g2n, wr_t, *, ctx_row):
    b, seq, _ = x3.shape
    tm = min(seq, 512)
    gblk = G_OFF // D_MODEL
    mrow = (lambda bi: bi) if ctx_row is None else (lambda bi: ctx_row)
    tok = lambda w: pl.BlockSpec((1, tm, w), lambda bi, i: (bi, i, 0))
    gate = lambda k: pl.BlockSpec((1, tm, D_MODEL), lambda bi, i: (bi, i, gblk + k))
    full = lambda a: pl.BlockSpec(a.shape, lambda bi, i: (0,) * a.ndim)
    modc = lambda k: pl.BlockSpec((1, 1, D_MODEL), lambda bi, i: (mrow(bi), 0, k))
    return pl.pallas_call(
        _merge_kernel,
        grid=(b, seq // tm),
        in_specs=[
            tok(D_MODEL), tok(FOURIER_W), tok(POOL_W), gate(0), gate(1), gate(2), tok(D_MODEL), modc(2),
            full(wao), full(wf), full(wp), full(ps), full(wo), modc(3), modc(4), full(g2n), full(wr_t),
        ],
        out_specs=[tok(D_MODEL), tok(D_MODEL), pl.BlockSpec((1, N_EXPERTS, tm), lambda bi, i: (bi, 0, i))],
        out_shape=[
            jax.ShapeDtypeStruct((b, seq, D_MODEL), F32),
            jax.ShapeDtypeStruct((b, seq, D_MODEL), BF16),
            jax.ShapeDtypeStruct((b, N_EXPERTS, seq), F32),
        ],
        compiler_params=_cparams(("parallel", "parallel"), 48),
        name="merge",
    )(att, four, pooled, pb3, pb3, pb3, x3, mod3, wao, wf, wp, ps, wo, mod3, mod3, g2n, wr_t)


def _exclusive_prefix(mask, tri):
    e, seq = mask.shape
    ones = jnp.where(mask, 1.0, 0.0)
    offs = jnp.zeros((e, 1), F32)
    pieces = []
    for k in range(seq // LANE):
        blk = ones[:, k * LANE:(k + 1) * LANE]
        local = jnp.dot(blk.astype(BF16), tri, preferred_element_type=F32)
        pieces.append(local + offs)
        offs = offs + jnp.sum(blk, axis=1, keepdims=True)
    return jnp.concatenate(pieces, axis=1)


META_TSTART, META_TEND, META_PSTART, META_PEND, META_W = 0, 4, 8, 12, 16
GATHER_SLOT_BLOCK = 128
GATHER_WIN = 1408
GATHER_ALIGN = 128
COMBINE_TILE = 1024
COMBINE_WIN = 192
COMBINE_ALIGN = 16


def _select_kernel(lg_ref, slot_ref, aff_ref, meta_ref, *, cap):
    z = lg_ref[0]
    z = z - jnp.max(z, axis=0, keepdims=True)
    ez = jnp.exp(z)
    a = ez / jnp.sum(ez, axis=0, keepdims=True)
    aff_ref[0] = a
    capf = float(cap)

    def bisect(_, lohi):
        lo, hi = lohi
        mid = (lo + hi) * 0.5
        cnt = jnp.sum(jnp.where(a >= mid, 1.0, 0.0), axis=1, keepdims=True)
        ge = cnt >= capf
        return jnp.where(ge, mid, lo), jnp.where(ge, hi, mid)

    e = a.shape[0]
    lo, hi = lax.fori_loop(0, SELECT_ITERS, bisect,
                           (jnp.zeros((e, 1), F32), jnp.full((e, 1), 2.0, F32)))
    r_i = lax.broadcasted_iota(I32, (LANE, LANE), 0)
    c_i = lax.broadcasted_iota(I32, (LANE, LANE), 1)
    tri = jnp.where(r_i < c_i, 1.0, 0.0).astype(BF16)
    above = a >= hi
    n_above = jnp.sum(jnp.where(above, 1.0, 0.0), axis=1, keepdims=True)
    tied = (a >= lo) & jnp.logical_not(above)
    tie_rank = _exclusive_prefix(tied, tri)
    sel = above | (tied & (tie_rank < capf - n_above))
    pos = _exclusive_prefix(sel, tri)
    slot_ref[0] = jnp.where(sel, pos, -1.0).astype(I32)

    seq = a.shape[1]
    t = lax.broadcasted_iota(I32, (e, seq), 1).astype(F32)
    lane = lax.broadcasted_iota(I32, (e, LANE), 1)
    meta = jnp.zeros((e, LANE), F32)
    sb = min(GATHER_SLOT_BLOCK, cap)
    for s in range(cap // sb):
        first = jnp.min(jnp.where(sel & (pos >= float(s * sb)), t, float(seq)), axis=1, keepdims=True)
        last = jnp.max(jnp.where(sel & (pos < float((s + 1) * sb)), t, -1.0), axis=1, keepdims=True)
        meta = jnp.where(lane == META_TSTART + s, first, meta)
        meta = jnp.where(lane == META_TEND + s, last, meta)
    tt = min(COMBINE_TILE, seq)
    for i in range(seq // tt):
        before = jnp.sum(jnp.where(sel & (t < float(i * tt)), 1.0, 0.0), axis=1, keepdims=True)
        upto = jnp.sum(jnp.where(sel & (t < float((i + 1) * tt)), 1.0, 0.0), axis=1, keepdims=True)
        meta = jnp.where(lane == META_PSTART + i, before, meta)
        meta = jnp.where(lane == META_PEND + i, upto, meta)
    meta_ref[0] = meta.astype(I32)


def _select(logits_t, cap):
    b, e, seq = logits_t.shape
    assert cap // min(GATHER_SLOT_BLOCK, cap) <= 4 and seq // min(COMBINE_TILE, seq) <= 4
    spec = pl.BlockSpec((1, e, seq), lambda bi: (bi, 0, 0))
    mspec = pl.BlockSpec((1, e, LANE), lambda bi: (bi, 0, 0))
    return pl.pallas_call(
        functools.partial(_select_kernel, cap=cap),
        grid=(b,),
        in_specs=[spec],
        out_specs=[spec, spec, mspec],
        out_shape=[jax.ShapeDtypeStruct((b, e, seq), I32), jax.ShapeDtypeStruct((b, e, seq), F32),
                   jax.ShapeDtypeStruct((b, e, LANE), I32)],
        compiler_params=_cparams(("parallel",), 32),
        name="select",
    )(logits_t)


def _align_down(v, align):
    shift = align.bit_length() - 1
    return lax.shift_left(lax.shift_right_logical(v, shift), shift)


def _gather_kernel(meta_ref, h_ref, slot_ref, aff_ref, xg_ref, gate_ref, *, cap, sb, win):
    seq = h_ref.shape[1]
    base = (pl.program_id(0) * N_EXPERTS + pl.program_id(1)) * META_W

    def gather_block(s, start, width):
        tok = pl.ds(start, width)
        j = lax.broadcasted_iota(I32, (sb, width), 0) + s * sb
        hit = slot_ref[0, 0, :, tok] == j
        onehot = jnp.where(hit, 1.0, 0.0).astype(BF16)
        rows = slice(s * sb, (s + 1) * sb)
        xg_ref[0, 0, rows, :] = jnp.dot(onehot, h_ref[0, tok, :], preferred_element_type=F32).astype(BF16)
        gate_ref[0, 0, rows, :] = jnp.sum(jnp.where(hit, aff_ref[0, 0, :, tok], 0.0), axis=1, keepdims=True)

    for s in range(cap // sb):
        if win >= seq:
            gather_block(s, 0, seq)
            continue
        first = meta_ref[base + META_TSTART + s]
        last = meta_ref[base + META_TEND + s]
        start = pl.multiple_of(jnp.minimum(_align_down(first, GATHER_ALIGN), seq - win), GATHER_ALIGN)
        fits = last < start + win

        @pl.when(fits)
        def _(s=s, start=start):
            gather_block(s, start, win)

        @pl.when(jnp.logical_not(fits))
        def _(s=s):
            gather_block(s, 0, seq)


def _gather(meta, h2, slots4, aff4, cap):
    b, seq, _ = h2.shape
    e = slots4.shape[1]
    sb = min(GATHER_SLOT_BLOCK, cap)
    win = GATHER_WIN if seq > GATHER_WIN else seq
    assert (seq - win) % GATHER_ALIGN == 0
    row = pl.BlockSpec((1, 1, 1, seq), lambda bi, ei, m: (bi, ei, 0, 0))
    return pl.pallas_call(
        functools.partial(_gather_kernel, cap=cap, sb=sb, win=win),
        grid_spec=pltpu.PrefetchScalarGridSpec(
            num_scalar_prefetch=1,
            grid=(b, e),
            in_specs=[pl.BlockSpec((1, seq, D_MODEL), lambda bi, ei, m: (bi, 0, 0)), row, row],
            out_specs=[
                pl.BlockSpec((1, 1, cap, D_MODEL), lambda bi, ei, m: (ei, bi, 0, 0)),
                pl.BlockSpec((1, 1, cap, 1), lambda bi, ei, m: (ei, bi, 0, 0)),
            ],
        ),
        out_shape=[
            jax.ShapeDtypeStruct((e, b, cap, D_MODEL), BF16),
            jax.ShapeDtypeStruct((e, b, cap, 1), F32),
        ],
        compiler_params=_cparams(("parallel", "arbitrary"), 48),
        name="gather",
    )(meta, h2, slots4, aff4)


FF_CHUNK = 256
FFN_ROW_TILES = 2


def _ffn_kernel(x_ref, gate_ref, wg_ref, wu_ref, wd_ref, o_ref, acc_ref):
    fc = pl.program_id(2)

    @pl.when(fc == 0)
    def _():
        acc_ref[...] = jnp.zeros_like(acc_ref)

    x = x_ref[0]
    a = jnp.dot(x, wg_ref[0, 0].astype(BF16), preferred_element_type=F32)
    u = jnp.dot(x, wu_ref[0, 0].astype(BF16), preferred_element_type=F32)
    hmid = (a * jax.nn.sigmoid(a) * u).astype(BF16)
    acc_ref[...] += jnp.dot(hmid, wd_ref[0, 0].astype(BF16), preferred_element_type=F32)

    @pl.when(fc == pl.num_programs(2) - 1)
    def _():
        o_ref[0] = (acc_ref[...] * gate_ref[0]).astype(BF16)


def _ffn(xg3, gate3, wg_all, wu_all, wd_all, layer):
    e, m, _ = xg3.shape
    tm = m // FFN_ROW_TILES if m >= 1024 else m
    assert m % tm == 0 and tm % 16 == 0
    return pl.pallas_call(
        _ffn_kernel,
        grid=(e, m // tm, EXPERT_FF // FF_CHUNK),
        in_specs=[
            pl.BlockSpec((1, tm, D_MODEL), lambda ei, i, fc: (ei, i, 0)),
            pl.BlockSpec((1, tm, 1), lambda ei, i, fc: (ei, i, 0)),
            pl.BlockSpec((1, 1, D_MODEL, FF_CHUNK), lambda ei, i, fc: (layer, ei, 0, fc)),
            pl.BlockSpec((1, 1, D_MODEL, FF_CHUNK), lambda ei, i, fc: (layer, ei, 0, fc)),
            pl.BlockSpec((1, 1, FF_CHUNK, D_MODEL), lambda ei, i, fc: (layer, ei, fc, 0)),
        ],
        out_specs=pl.BlockSpec((1, tm, D_MODEL), lambda ei, i, fc: (ei, i, 0)),
        out_shape=jax.ShapeDtypeStruct((e, m, D_MODEL), BF16),
        scratch_shapes=[pltpu.VMEM((tm, D_MODEL), F32)],
        compiler_params=_cparams(("parallel", "parallel", "arbitrary"), 56),
        name="ffn",
    )(xg3, gate3, wg_all, wu_all, wd_all)


COMBINE_EXPERTS = 4


def _combine_kernel(meta_ref, slot_ref, gy_ref, x_ref, g2_ref, gf_ref, o_ref, acc_ref, *, cap, win, final):
    bi = pl.program_id(0)
    ti = pl.program_id(1)
    ec = pl.program_id(2)
    tt = x_ref.shape[1]
    tn = (((0,), (0,)), ((), ()))

    @pl.when(ec == 0)
    def _():
        acc_ref[...] = jnp.zeros_like(acc_ref)

    def scatter(starts, width):
        j = lax.broadcasted_iota(I32, (width, tt), 0)
        onehot = jnp.concatenate(
            [jnp.where(slot_ref[0, k] == j + starts[k], 1.0, 0.0).astype(BF16) for k in range(COMBINE_EXPERTS)],
            axis=0)
        gy = jnp.concatenate([gy_ref[k, pl.ds(starts[k], width), :] for k in range(COMBINE_EXPERTS)], axis=0)
        acc_ref[...] += lax.dot_general(onehot, gy, tn, preferred_element_type=F32)

    if win >= cap:
        scatter([0] * COMBINE_EXPERTS, cap)
    else:
        starts = []
        fits = None
        for k in range(COMBINE_EXPERTS):
            base = (bi * N_EXPERTS + ec * COMBINE_EXPERTS + k) * META_W
            before = meta_ref[base + META_PSTART + ti]
            upto = meta_ref[base + META_PEND + ti]
            start = pl.multiple_of(jnp.minimum(_align_down(before, COMBINE_ALIGN), cap - win), COMBINE_ALIGN)
            starts.append(start)
            ok = upto <= start + win
            fits = ok if fits is None else jnp.logical_and(fits, ok)

        @pl.when(fits)
        def _():
            scatter(starts, win)

        @pl.when(jnp.logical_not(fits))
        def _():
            scatter([0] * COMBINE_EXPERTS, cap)

    @pl.when(ec == pl.num_programs(2) - 1)
    def _():
        xn = x_ref[0] + g2_ref[0] * acc_ref[...]
        if final:
            ms = jnp.mean(xn * xn, axis=-1, keepdims=True)
            xn = (xn * lax.rsqrt(ms + RMS_EPS)) * gf_ref[...]
        o_ref[0] = xn


def _combine(meta, slots4, gy3, row_off, x3, mod3, gfinal, *, cap, ctx_row, final):
    b, seq, _ = x3.shape
    e = slots4.shape[1]
    tt = min(seq, COMBINE_TILE)
    win = COMBINE_WIN if cap > COMBINE_WIN else cap
    assert row_off % cap == 0 and (cap - win) % COMBINE_ALIGN == 0
    blk_off = row_off // cap
    mrow = (lambda bi: bi) if ctx_row is None else (lambda bi: ctx_row)
    return pl.pallas_call(
        functools.partial(_combine_kernel, cap=cap, win=win, final=final),
        grid_spec=pltpu.PrefetchScalarGridSpec(
            num_scalar_prefetch=1,
            grid=(b, seq // tt, e // COMBINE_EXPERTS),
            in_specs=[
                pl.BlockSpec((1, COMBINE_EXPERTS, 1, tt), lambda bi, i, ec, m: (bi, ec, 0, i)),
                pl.BlockSpec((COMBINE_EXPERTS, cap, D_MODEL), lambda bi, i, ec, m: (ec, blk_off + bi, 0)),
                pl.BlockSpec((1, tt, D_MODEL), lambda bi, i, ec, m: (bi, i, 0)),
                pl.BlockSpec((1, 1, D_MODEL), lambda bi, i, ec, m: (mrow(bi), 0, 5)),
                pl.BlockSpec((1, D_MODEL), lambda bi, i, ec, m: (0, 0)),
            ],
            out_specs=pl.BlockSpec((1, tt, D_MODEL), lambda bi, i, ec, m: (bi, i, 0)),
            scratch_shapes=[pltpu.VMEM((tt, D_MODEL), F32)],
        ),
        out_shape=jax.ShapeDtypeStruct((b, seq, D_MODEL), F32),
        compiler_params=_cparams(("parallel", "parallel", "arbitrary"), 48),
        name="combine",
    )(meta, slots4, gy3, x3, mod3, gfinal)


def _moe_route(h2, logits_t):
    b, seq, _ = h2.shape
    cap = EC_CAPACITY_FACTOR * seq // N_EXPERTS
    slots, aff, meta = _select(logits_t, cap)
    meta = meta[:, :, :META_W].reshape(-1)
    slots4 = slots.reshape(b, N_EXPERTS, 1, seq)
    aff4 = aff.reshape(b, N_EXPERTS, 1, seq)
    xg, gate = _gather(meta, h2, slots4, aff4, cap)
    return meta, slots4, xg.reshape(N_EXPERTS, b * cap, D_MODEL), gate.reshape(N_EXPERTS, b * cap, 1), cap


def kernel(x, c, ctx, c_ctx, ada_w, ada_b, norm1_g, norm2_g, w_in, rpb, w_att_o, w_fourier, w_pool,
           pool_scale, w_out, w_router, w_exp_gate, w_exp_up, w_exp_down, final_norm_g):
    b, seq, d = x.shape
    lc = ctx.shape[1]
    assert d == D_MODEL and seq % (GRID_W * ATT_QROWS) == 0 and b + 1 <= MOD_ROWS
    rows = seq // GRID_W
    ctx_row = b

    cond = jnp.concatenate([c, c_ctx[None, :], jnp.zeros((MOD_ROWS - b - 1, d), F32)], axis=0)
    cl, sl = _dft_mats(seq, seq // 2)
    clc, slc = _dft_mats(lc, lc // 2)
    cc, sc = _dft_mats(FOURIER_GROUP_W, FOURIER_GROUP_W)
    gfinal = final_norm_g.reshape(1, d)

    for i in range(DEPTH):
        update_ctx = i < DEPTH - 1
        mod3 = _adaln(cond, ada_w[i], ada_b[i]).reshape(MOD_ROWS, 1, 6 * d)
        g1n = norm1_g[i].reshape(1, d)
        g2n = norm2_g[i].reshape(1, d)
        w_in_b = w_in[i].astype(BF16)
        wao = w_att_o[i].astype(BF16)
        wf = w_fourier[i].astype(BF16)
        wp = w_pool[i].astype(BF16)
        ps = pool_scale[i].reshape(1, d)
        wo = w_out[i].astype(BF16)
        wr_t = jnp.pad(w_router[i], ((0, 0), (0, LANE - N_EXPERTS)))
        bias = _bias_table(rpb[i], rows)

        pb, pp = _modproj(x.reshape(b * seq, d), mod3, g1n, w_in_b, seq=seq, ctx_row=None,
                          n_out=IN_W, with_pool=True)
        n_ctx = IN_W if update_ctx else Q_OFF
        pcb, pcp = _modproj(ctx.reshape(b * lc, d), mod3, g1n, w_in_b[:, :n_ctx], seq=lc, ctx_row=ctx_row,
                            n_out=n_ctx, with_pool=update_ctx)
        pb3 = pb.reshape(b, seq, IN_W)
        pcb3 = pcb.reshape(b, lc, n_ctx)

        att = _nattn(pb3, pcb3, bias)
        four = _fourier(pb3, cc, sc, cl, sl)
        pooled = _pool(pp.reshape(b, seq, POOL_W))
        x, h2, logits_t = _merge(att, four, pooled, pb3, x, mod3, wao, wf, wp, ps, wo, g2n, wr_t, ctx_row=None)
        meta_x, slots_x, xg, gate, cap = _moe_route(h2, logits_t)

        if update_ctx:
            att_c = _cattn(pcb3)
            four_c = _fourier(pcb3, cc, sc, clc, slc)
            pooled_c = _pool(pcp.reshape(b, lc, POOL_W))
            ctx, h2_c, logits_c = _merge(att_c, four_c, pooled_c, pcb3, ctx, mod3, wao, wf, wp, ps, wo, g2n, wr_t,
                                         ctx_row=ctx_row)
            meta_c, slots_c, xg_c, gate_c, cap_c = _moe_route(h2_c, logits_c)
            xg = jnp.concatenate([xg, xg_c], axis=1)
            gate = jnp.concatenate([gate, gate_c], axis=1)

        gy = _ffn(xg, gate, w_exp_gate, w_exp_up, w_exp_down, i)
        if update_ctx:
            ctx = _combine(meta_c, slots_c, gy, b * cap, ctx, mod3, gfinal, cap=cap_c, ctx_row=ctx_row,
                           final=False)
        x = _combine(meta_x, slots_x, gy, 0, x, mod3, gfinal, cap=cap, ctx_row=None, final=not update_ctx)
    return x
```

```python
import functools
import math

import jax
import jax.numpy as jnp
from jax import lax
from jax.experimental import pallas as pl
from jax.experimental.pallas import tpu as pltpu

F32 = jnp.float32
BF16 = jnp.bfloat16
I32 = jnp.int32
HIGHEST = lax.Precision.HIGHEST

D_MODEL = 1024
DEPTH = 2
GRID_W = 64
N_HEADS = 16
HEAD_DIM = 64
WIN_R = 8
WIN_C = 16
FOURIER_GROUPS = 4
FOURIER_GROUP_W = 128
FOURIER_W = 512
POOL_WINDOWS = (2, 4, 8, 16)
POOL_GROUP_W = 128
POOL_W = 512
POOL_OUT_GROUP = 256
K_OFF, V_OFF, Q_OFF, F_OFF, P_OFF, G_OFF = 0, 1024, 2048, 3072, 3584, 4096
IN_W = 7168
N_EXPERTS = 16
EC_CAPACITY_FACTOR = 2
EXPERT_FF = 2816
RMS_EPS = 1e-6

LANE = 128
HEAD_PAIR_W = 2 * HEAD_DIM
N_HEAD_PAIRS = N_HEADS // 2
MOD_ROWS = 16
NEG_BIG = -1e30
ATT_QROWS = 2
ATT_KROWS = 10
ATT_CLASSES = 5
SELECT_ITERS = 32
MiB = 1024 * 1024


def _cparams(sem, vmem_mib):
    return pltpu.CompilerParams(dimension_semantics=sem, vmem_limit_bytes=vmem_mib * MiB)


def _adaln_kernel(c_ref, w_ref, b_ref, o_ref):
    c = c_ref[...]
    s = c * jax.nn.sigmoid(c)
    o_ref[...] = jnp.dot(s, w_ref[...], precision=HIGHEST, preferred_element_type=F32) + b_ref[...]


def _adaln(cond_rows, ada_w, ada_b):
    n = ada_w.shape[1]
    tn = 1024
    return pl.pallas_call(
        _adaln_kernel,
        grid=(n // tn,),
        in_specs=[
            pl.BlockSpec((MOD_ROWS, D_MODEL), lambda j: (0, 0)),
            pl.BlockSpec((D_MODEL, tn), lambda j: (0, j)),
            pl.BlockSpec((1, tn), lambda j: (0, j)),
        ],
        out_specs=pl.BlockSpec((MOD_ROWS, tn), lambda j: (0, j)),
        out_shape=jax.ShapeDtypeStruct((MOD_ROWS, n), F32),
        compiler_params=_cparams(("arbitrary",), 32),
        name="adaln",
    )(cond_rows, ada_w, ada_b.reshape(1, n))


def _modulate(x, g, shift, scale):
    ms = jnp.mean(x * x, axis=-1, keepdims=True)
    y = x * lax.rsqrt(ms + RMS_EPS)
    return (y * g) * (1.0 + scale) + shift


MODPROJ_TN = 1792


def _modproj_kernel(x_ref, sh_ref, sc_ref, g_ref, w_ref, *rest, pool_tile):
    if pool_tile is None:
        o_ref, h_ref = rest
        pp_ref = None
    else:
        o_ref, pp_ref, h_ref = rest
    j = pl.program_id(1)

    @pl.when(j == 0)
    def _():
        h = _modulate(x_ref[...], g_ref[...], sh_ref[0], sc_ref[0])
        h_ref[...] = h.astype(BF16)

    acc = jnp.dot(h_ref[...], w_ref[...], preferred_element_type=F32)
    o_ref[...] = acc.astype(BF16)
    if pool_tile is not None:
        @pl.when(j == pool_tile)
        def _():
            lo = P_OFF % acc.shape[1]
            pp_ref[...] = acc[:, lo:lo + POOL_W]


def _modproj(x2, mod3, gain, w, *, seq, ctx_row, n_out, with_pool):
    rows = x2.shape[0]
    tn = MODPROJ_TN if n_out % MODPROJ_TN == 0 else 1024
    if ctx_row is None:
        tm = min(seq, 1024)
        tiles_per_seq = seq // tm
        mrow = lambda i: i // tiles_per_seq
    else:
        tm = min(rows, 1024)
        mrow = lambda i: ctx_row
    assert rows % tm == 0 and n_out % tn == 0 and (not with_pool or P_OFF % tn + POOL_W <= tn)
    pool_tile = (P_OFF // tn) if with_pool else None
    out_shape = [jax.ShapeDtypeStruct((rows, n_out), BF16)]
    out_specs = [pl.BlockSpec((tm, tn), lambda i, j: (i, j))]
    if with_pool:
        out_shape.append(jax.ShapeDtypeStruct((rows, POOL_W), F32))
        out_specs.append(pl.BlockSpec((tm, POOL_W), lambda i, j: (i, 0)))
    res = pl.pallas_call(
        functools.partial(_modproj_kernel, pool_tile=pool_tile),
        grid=(rows // tm, n_out // tn),
        in_specs=[
            pl.BlockSpec((tm, D_MODEL), lambda i, j: (i, 0)),
            pl.BlockSpec((1, 1, D_MODEL), lambda i, j: (mrow(i), 0, 0)),
            pl.BlockSpec((1, 1, D_MODEL), lambda i, j: (mrow(i), 0, 1)),
            pl.BlockSpec((1, D_MODEL), lambda i, j: (0, 0)),
            pl.BlockSpec((D_MODEL, tn), lambda i, j: (0, j)),
        ],
        out_specs=out_specs,
        out_shape=out_shape,
        scratch_shapes=[pltpu.VMEM((tm, D_MODEL), BF16)],
        compiler_params=_cparams(("parallel", "arbitrary"), 48),
        name="modproj",
    )(x2, mod3, mod3, gain, w)
    return res if with_pool else (res[0], None)


def _bias_kernel(rpb_ref, o_ref, *, rows):
    h = pl.program_id(0)
    cls = pl.program_id(1)
    i_rep = jnp.where(cls < 3, cls, cls + (rows // ATT_QROWS - ATT_CLASSES))
    s = jnp.clip(ATT_QROWS * i_rep - WIN_R // 2, 0, rows - ATT_KROWS)
    qc = lax.broadcasted_iota(I32, (GRID_W, LANE), 0)
    lane = lax.broadcasted_iota(I32, (GRID_W, LANE), 1)
    kc = lane & (GRID_W - 1)
    first_half = lane < GRID_W
    cs = jnp.clip(qc - WIN_C // 2, 0, GRID_W - WIN_C)
    col_valid = (kc >= cs) & (kc < cs + WIN_C)
    dcol = kc - qc + (WIN_C - 1)
    n_coff = 2 * WIN_C - 1
    n_roff = 2 * WIN_R - 1
    for ri in range(ATT_QROWS):
        r = ATT_QROWS * i_rep + ri
        rs = jnp.clip(r - WIN_R // 2, 0, rows - WIN_R)
        for m in range(ATT_KROWS // 2):
            krow_a = s + 2 * m
            krow_b = krow_a + 1
            va = ((krow_a >= rs) & (krow_a < rs + WIN_R)).astype(I32)
            vb = ((krow_b >= rs) & (krow_b < rs + WIN_R)).astype(I32)
            base_a = (h * n_roff + jnp.clip(krow_a - r + WIN_R - 1, 0, n_roff - 1)) * n_coff
            base_b = (h * n_roff + jnp.clip(krow_b - r + WIN_R - 1, 0, n_roff - 1)) * n_coff
            acc = jnp.zeros((GRID_W, LANE), F32)
            for c in range(n_coff):
                val = jnp.where(first_half, rpb_ref[base_a + c], rpb_ref[base_b + c])
                acc = jnp.where(dcol == c, val, acc)
            row_valid = jnp.where(first_half, va, vb) > 0
            tile = jnp.where(col_valid & row_valid, acc, NEG_BIG)
            o_ref[0, 0, ri * GRID_W:(ri + 1) * GRID_W, m * LANE:(m + 1) * LANE] = tile


def _bias_table(rpb, rows):
    nq = ATT_QROWS * GRID_W
    nk = ATT_KROWS * GRID_W
    return pl.pallas_call(
        functools.partial(_bias_kernel, rows=rows),
        grid_spec=pltpu.PrefetchScalarGridSpec(
            num_scalar_prefetch=1,
            grid=(N_HEADS, ATT_CLASSES),
            in_specs=[],
            out_specs=pl.BlockSpec((1, 1, nq, nk), lambda h, c, rpb: (h, c, 0, 0)),
        ),
        out_shape=jax.ShapeDtypeStruct((N_HEADS, ATT_CLASSES, nq, nk), F32),
        compiler_params=_cparams(("arbitrary", "arbitrary"), 32),
        name="bias_table",
    )(rpb.reshape(-1))


_NT = (((1,), (1,)), ((), ()))


def _scores(qm, kw, kc, bias):
    sw = lax.dot_general(qm, kw, _NT, preferred_element_type=F32)
    if bias is not None:
        sw = sw + bias
    sc = lax.dot_general(qm, kc, _NT, preferred_element_type=F32) if kc is not None else None
    return sw, sc


def _probs(sw, sc):
    m = jnp.max(sw, axis=-1, keepdims=True)
    if sc is not None:
        m = jnp.maximum(m, jnp.max(sc, axis=-1, keepdims=True))
    pw = jnp.exp(sw - m)
    l = jnp.sum(pw, axis=-1, keepdims=True)
    pc = None
    if sc is not None:
        pc = jnp.exp(sc - m)
        l = l + jnp.sum(pc, axis=-1, keepdims=True)
        pc = pc.astype(BF16)
    return pw.astype(BF16), pc, l


def _pv(pw, pc, l, vw, vc):
    o = jnp.dot(pw, vw, preferred_element_type=F32)
    if pc is not None:
        o = o + jnp.dot(pc, vc, preferred_element_type=F32)
    return o / l


ATT_UNROLL = 4


ATT_KEY_TILE = 256


def _key_tiles(n):
    return [(off, min(ATT_KEY_TILE, n - off)) for off in range(0, n, ATT_KEY_TILE)]


def _lane_fold(x, op):
    out = x[:, :LANE]
    for c in range(1, x.shape[1] // LANE):
        out = op(out, x[:, c * LANE:(c + 1) * LANE])
    return out


def _nattn_kernel(k_ref, v_ref, q_ref, kc_ref, vc_ref, b_ref, o_ref, s_ref, *, rows):
    nq = ATT_QROWS * GRID_W
    nk = ATT_KROWS * GRID_W
    lc = kc_ref.shape[1]
    n_steps = rows // ATT_QROWS
    lane = lax.broadcasted_iota(I32, (nq, HEAD_PAIR_W), 1)
    second = lane >= HEAD_DIM
    scale = HEAD_DIM ** -0.5
    win_tiles = _key_tiles(nk)
    ctx_tiles = _key_tiles(lc)

    ones_cols = jnp.ones((ATT_KEY_TILE, HEAD_PAIR_W), BF16)

    def with_ones(v):
        return jnp.concatenate([v, ones_cols[:v.shape[0]]], axis=1)

    def body(ii, carry):
        steps = []
        row_max = []
        for u in range(ATT_UNROLL):
            i = ii * ATT_UNROLL + u
            s = jnp.clip(ATT_QROWS * i - WIN_R // 2, 0, rows - ATT_KROWS)
            kstart = pl.multiple_of(s * GRID_W, LANE)
            qstart = pl.multiple_of(i * nq, LANE)
            cls = jnp.where(i < 2, i, jnp.where(i > n_steps - 3, i - (n_steps - ATT_CLASSES), 2))
            steps.append((kstart, qstart))
            q2 = (q_ref[0, pl.ds(qstart, nq), :].astype(F32) * scale).astype(BF16)
            for hh in range(2):
                c = 2 * u + hh
                head_lanes = second if hh else jnp.logical_not(second)
                qm = jnp.where(head_lanes, q2, jnp.zeros_like(q2))
                m_run = None
                for off, width in win_tiles:
                    kt = k_ref[0, pl.ds(kstart + off, width), :]
                    st = lax.dot_general(qm, kt, _NT, preferred_element_type=F32) + b_ref[hh, cls, :, off:off + width]
                    s_ref[c, :, off:off + width] = st
                    mt = _lane_fold(st, jnp.maximum)
                    m_run = mt if m_run is None else jnp.maximum(m_run, mt)
                for off, width in ctx_tiles:
                    st = lax.dot_general(qm, kc_ref[0, off:off + width, :], _NT, preferred_element_type=F32)
                    s_ref[c, :, nk + off:nk + off + width] = st
                    m_run = jnp.maximum(m_run, _lane_fold(st, jnp.maximum))
                row_max.append(jnp.max(m_run, axis=-1, keepdims=True))
        outs = []
        for c in range(2 * ATT_UNROLL):
            kstart, _ = steps[c // 2]
            m = row_max[c]
            acc = None
            for off, width in win_tiles:
                pt = jnp.exp(s_ref[c, :, off:off + width] - m).astype(BF16)
                pv = jnp.dot(pt, with_ones(v_ref[0, pl.ds(kstart + off, width), :]), preferred_element_type=F32)
                acc = pv if acc is None else acc + pv
            for off, width in ctx_tiles:
                pt = jnp.exp(s_ref[c, :, nk + off:nk + off + width] - m).astype(BF16)
                acc = acc + jnp.dot(pt, with_ones(vc_ref[0, off:off + width, :]), preferred_element_type=F32)
            outs.append(acc[:, :HEAD_PAIR_W] / acc[:, HEAD_PAIR_W:])
        for u in range(ATT_UNROLL):
            o = jnp.where(second, outs[2 * u + 1], outs[2 * u])
            o_ref[0, pl.ds(steps[u][1], nq), :] = o.astype(BF16)
        return carry

    lax.fori_loop(0, n_steps // ATT_UNROLL, body, 0)


def _nattn(pb3, pcb3, bias):
    b, seq, _ = pb3.shape
    lc = pcb3.shape[1]
    rows = seq // GRID_W
    nq = ATT_QROWS * GRID_W
    nk = ATT_KROWS * GRID_W
    kblk, vblk, qblk = K_OFF // LANE, V_OFF // LANE, Q_OFF // LANE
    return pl.pallas_call(
        functools.partial(_nattn_kernel, rows=rows),
        grid=(N_HEAD_PAIRS, b),
        in_specs=[
            pl.BlockSpec((1, seq, HEAD_PAIR_W), lambda hp, bi: (bi, 0, kblk + hp)),
            pl.BlockSpec((1, seq, HEAD_PAIR_W), lambda hp, bi: (bi, 0, vblk + hp)),
            pl.BlockSpec((1, seq, HEAD_PAIR_W), lambda hp, bi: (bi, 0, qblk + hp)),
            pl.BlockSpec((1, lc, HEAD_PAIR_W), lambda hp, bi: (bi, 0, kblk + hp)),
            pl.BlockSpec((1, lc, HEAD_PAIR_W), lambda hp, bi: (bi, 0, vblk + hp)),
            pl.BlockSpec((2, ATT_CLASSES, nq, nk), lambda hp, bi: (hp, 0, 0, 0)),
        ],
        out_specs=pl.BlockSpec((1, seq, HEAD_PAIR_W), lambda hp, bi: (bi, 0, hp)),
        out_shape=jax.ShapeDtypeStruct((b, seq, N_HEADS * HEAD_DIM), BF16),
        scratch_shapes=[pltpu.VMEM((2 * ATT_UNROLL, nq, nk + lc), F32)],
        compiler_params=_cparams(("parallel", "parallel"), 40),
        name="nattn",
    )(pb3, pb3, pb3, pcb3, pcb3, bias)


def _cattn_kernel(k_ref, v_ref, q_ref, o_ref):
    lc = q_ref.shape[1]
    lane = lax.broadcasted_iota(I32, (lc, HEAD_PAIR_W), 1)
    second = lane >= HEAD_DIM
    q2 = (q_ref[0].astype(F32) * (HEAD_DIM ** -0.5)).astype(BF16)
    k = k_ref[0]
    v = v_ref[0]
    outs = []
    for hh in range(2):
        head_lanes = second if hh else jnp.logical_not(second)
        qm = jnp.where(head_lanes, q2, jnp.zeros_like(q2))
        pw, _, l = _probs(*_scores(qm, k, None, None))
        outs.append(_pv(pw, None, l, v, None))
    o_ref[0] = jnp.where(second, outs[1], outs[0]).astype(BF16)


def _cattn(pcb3):
    b, lc, _ = pcb3.shape
    kblk, vblk, qblk = K_OFF // LANE, V_OFF // LANE, Q_OFF // LANE
    return pl.pallas_call(
        _cattn_kernel,
        grid=(b, N_HEAD_PAIRS),
        in_specs=[
            pl.BlockSpec((1, lc, HEAD_PAIR_W), lambda bi, hp: (bi, 0, kblk + hp)),
            pl.BlockSpec((1, lc, HEAD_PAIR_W), lambda bi, hp: (bi, 0, vblk + hp)),
            pl.BlockSpec((1, lc, HEAD_PAIR_W), lambda bi, hp: (bi, 0, qblk + hp)),
        ],
        out_specs=pl.BlockSpec((1, lc, HEAD_PAIR_W), lambda bi, hp: (bi, 0, hp)),
        out_shape=jax.ShapeDtypeStruct((b, lc, N_HEADS * HEAD_DIM), BF16),
        compiler_params=_cparams(("parallel", "parallel"), 32),
        name="cattn",
    )(pcb3, pcb3, pcb3)


def _dft_kernel(c_ref, s_ref, *, n):
    tk, ncols = c_ref.shape
    k = pl.program_id(0) * tk + lax.broadcasted_iota(I32, (tk, ncols), 0)
    t = lax.broadcasted_iota(I32, (tk, ncols), 1)
    m = (k * t) & (n - 1)
    ang = m.astype(F32) * (2.0 * math.pi / n)
    c_ref[...] = jnp.cos(ang).astype(BF16)
    s_ref[...] = jnp.sin(ang).astype(BF16)


def _dft_mats(n, ncols):
    tk = min(n, 256)
    return pl.pallas_call(
        functools.partial(_dft_kernel, n=n),
        grid=(n // tk,),
        in_specs=[],
        out_specs=[pl.BlockSpec((tk, ncols), lambda i: (i, 0))] * 2,
        out_shape=[jax.ShapeDtypeStruct((n, ncols), BF16)] * 2,
        compiler_params=_cparams(("parallel",), 48),
        name=f"dft_mats_{n}",
    )()


REV_BLOCK = 128


def _fourier_kernel(u_ref, cc_ref, sc_ref, cl_ref, sl_ref, o_ref, us_ref, ud_ref, a_ref, b_ref, ah_ref, *, seq):
    half = seq // 2
    nblk = seq // REV_BLOCK
    tk = o_ref.shape[1]

    @pl.when(pl.program_id(1) == 0)
    def _():
        d_i = lax.broadcasted_iota(I32, (REV_BLOCK, REV_BLOCK), 0)
        s_i = lax.broadcasted_iota(I32, (REV_BLOCK, REV_BLOCK), 1)
        flip = jnp.where((d_i >= 1) & (s_i == REV_BLOCK - d_i), 1.0, 0.0).astype(BF16)
        row = lax.broadcasted_iota(I32, (REV_BLOCK, FOURIER_W), 0)
        for blk in range(nblk // 2):
            lo = u_ref[0, blk * REV_BLOCK:(blk + 1) * REV_BLOCK, :].astype(F32)
            src = u_ref[0, (nblk - 1 - blk) * REV_BLOCK:(nblk - blk) * REV_BLOCK, :]
            rev = jnp.dot(flip, src, preferred_element_type=F32)
            if blk > 0:
                head = u_ref[0, (nblk - blk) * REV_BLOCK:(nblk - blk) * REV_BLOCK + 16, :].astype(F32)
                rev = jnp.where(row == 0, head[0:1, :], rev)
            rows = slice(blk * REV_BLOCK, (blk + 1) * REV_BLOCK)
            us_ref[rows, :] = (lo + rev).astype(BF16)
            ud_ref[rows, :] = (lo - rev).astype(BF16)
        mid = u_ref[0, half:half + 16, :]
        for g in range(FOURIER_GROUPS):
            sl = slice(g * FOURIER_GROUP_W, (g + 1) * FOURIER_GROUP_W)
            a_ref[:, sl] = jnp.dot(us_ref[:, sl], cc_ref[...], preferred_element_type=F32).astype(BF16)
            b_ref[:, sl] = jnp.dot(ud_ref[:, sl], sc_ref[...], preferred_element_type=F32).astype(BF16)
            ah_ref[:, sl] = jnp.dot(mid[:, sl], cc_ref[...], preferred_element_type=F32)

    k = pl.program_id(1) * tk + lax.broadcasted_iota(I32, (tk, 1), 0)
    sign = (1 - 2 * (k & 1)).astype(F32)
    y = (jnp.dot(cl_ref[...], a_ref[...], preferred_element_type=F32)
         - jnp.dot(sl_ref[...], b_ref[...], preferred_element_type=F32)
         + sign * ah_ref[0:1, :])
    o_ref[0] = (y * (1.0 / math.sqrt(seq * FOURIER_GROUP_W))).astype(BF16)


def _fourier(pb3, cc, sc, cl, sl):
    b, seq, _ = pb3.shape
    tk = min(seq, 512)
    half = seq // 2
    assert seq % (2 * REV_BLOCK) == 0
    return pl.pallas_call(
        functools.partial(_fourier_kernel, seq=seq),
        grid=(b, seq // tk),
        in_specs=[
            pl.BlockSpec((1, seq, FOURIER_W), lambda bi, k: (bi, 0, F_OFF // FOURIER_W)),
            pl.BlockSpec((FOURIER_GROUP_W, FOURIER_GROUP_W), lambda bi, k: (0, 0)),
            pl.BlockSpec((FOURIER_GROUP_W, FOURIER_GROUP_W), lambda bi, k: (0, 0)),
            pl.BlockSpec((tk, half), lambda bi, k: (k, 0)),
            pl.BlockSpec((tk, half), lambda bi, k: (k, 0)),
        ],
        out_specs=pl.BlockSpec((1, tk, FOURIER_W), lambda bi, k: (bi, k, 0)),
        out_shape=jax.ShapeDtypeStruct((b, seq, FOURIER_W), BF16),
        scratch_shapes=[pltpu.VMEM((half, FOURIER_W), BF16)] * 4 + [pltpu.VMEM((16, FOURIER_W), F32)],
        compiler_params=_cparams(("parallel", "arbitrary"), 48),
        name="fourier",
    )(pb3, cc, sc, cl, sl)


POOL_PAD = 8


def _pool_kernel(u_ref, o_ref, pad_ref, *, seq):
    t = lax.broadcasted_iota(I32, (seq, POOL_GROUP_W), 0)
    zeros = jnp.zeros((POOL_PAD, POOL_GROUP_W), F32)
    pad_ref[0:POOL_PAD, :] = zeros
    pad_ref[seq + POOL_PAD:seq + 2 * POOL_PAD, :] = zeros
    pad_ref[POOL_PAD:seq + POOL_PAD, :] = u_ref[0]
    for g, w in enumerate(POOL_WINDOWS):
        @pl.when(pl.program_id(1) == g)
        def _(w=w):
            acc = None
            for d in range(-(w // 2), w - w // 2):
                term = pad_ref[pl.ds(POOL_PAD + d, seq), :]
                acc = term if acc is None else acc + term
            cnt = (jnp.minimum(t + (w - w // 2), seq) - jnp.maximum(t - w // 2, 0)).astype(F32)
            o_ref[0] = (acc / cnt - u_ref[0]).astype(BF16)


def _pool(pp3):
    b, seq, _ = pp3.shape
    spec = pl.BlockSpec((1, seq, POOL_GROUP_W), lambda bi, g: (bi, 0, g))
    return pl.pallas_call(
        functools.partial(_pool_kernel, seq=seq),
        grid=(b, len(POOL_WINDOWS)),
        in_specs=[spec],
        out_specs=spec,
        out_shape=jax.ShapeDtypeStruct((b, seq, POOL_W), BF16),
        scratch_shapes=[pltpu.VMEM((seq + 2 * POOL_PAD, POOL_GROUP_W), F32)],
        compiler_params=_cparams(("parallel", "parallel"), 32),
        name="pool",
    )(pp3)


MERGE_PARTS = 4


def _merge_kernel(att_ref, four_ref, pool_ref, ga_ref, gf_ref, gp_ref, x_ref, g1_ref,
                  wao_ref, wf_ref, wp_ref, ps_ref, wo_ref, sh2_ref, sc2_ref, g2n_ref, wr_ref,
                  o_ref, h_ref, lg_ref):
    tm = x_ref.shape[1]
    n_parts = min(MERGE_PARTS, tm // LANE)
    parts = [slice(p * (tm // n_parts), (p + 1) * (tm // n_parts)) for p in range(n_parts)]
    wr = wr_ref[...]
    wr_hi = wr.astype(BF16)
    wr_lo = (wr - wr_hi.astype(F32)).astype(BF16)

    def sigmoid(ref, rows):
        return 0.5 * jnp.tanh(0.5 * ref[0, rows, :].astype(F32)) + 0.5

    branches = []
    for rows in parts:
        y_att = jnp.dot(att_ref[0, rows, :], wao_ref[...], preferred_element_type=F32)
        y_four = jnp.dot(four_ref[0, rows, :], wf_ref[...], preferred_element_type=F32)
        pooled = pool_ref[0, rows, :]
        y_pool = jnp.concatenate(
            [jnp.dot(pooled[:, g * POOL_GROUP_W:(g + 1) * POOL_GROUP_W], wp_ref[g], preferred_element_type=F32)
             for g in range(len(POOL_WINDOWS))], axis=-1) * ps_ref[...]
        branches.append((y_att, y_four, y_pool))
    resid = []
    for rows, (y_att, y_four, y_pool) in zip(parts, branches):
        merged = (sigmoid(ga_ref, rows) * y_att + sigmoid(gf_ref, rows) * y_four + sigmoid(gp_ref, rows) * y_pool)
        y = jnp.dot(merged.astype(BF16), wo_ref[...], preferred_element_type=F32)
        xn = x_ref[0, rows, :] + g1_ref[0] * y
        o_ref[0, rows, :] = xn
        resid.append(xn)
    for p, (rows, xn) in enumerate(zip(parts, resid)):
        h = _modulate(xn, g2n_ref[...], sh2_ref[0], sc2_ref[0])
        h_hi = h.astype(BF16)
        h_ref[0, rows, :] = h_hi
        h_lo = (h - h_hi.astype(F32)).astype(BF16)
        lg = (jnp.dot(h_hi, wr_hi, preferred_element_type=F32) + jnp.dot(h_lo, wr_hi, preferred_element_type=F32)
              + jnp.dot(h_hi, wr_lo, preferred_element_type=F32))
        lg_ref[0, :, rows] = lg.T[:N_EXPERTS, :]


def _merge(att, four, pooled, pb3, x3, mod3, wao, wf, wp, ps, wo, g2n, wr_t, *, ctx_row):
    b, seq, _ = x3.shape
    tm = min(seq, 512)
    gblk = G_OFF // D_MODEL
    mrow = (lambda bi: bi) if ctx_row is None else (lambda bi: ctx_row)
    tok = lambda w: pl.BlockSpec((1, tm, w), lambda bi, i: (bi, i, 0))
    gate = lambda k: pl.BlockSpec((1, tm, D_MODEL), lambda bi, i: (bi, i, gblk + k))
    full = lambda a: pl.BlockSpec(a.shape, lambda bi, i: (0,) * a.ndim)
    modc = lambda k: pl.BlockSpec((1, 1, D_MODEL), lambda bi, i: (mrow(bi), 0, k))
    return pl.pallas_call(
        _merge_kernel,
        grid=(b, seq // tm),
        in_specs=[
            tok(D_MODEL), tok(FOURIER_W), tok(POOL_W), gate(0), gate(1), gate(2), tok(D_MODEL), modc(2),
            full(wao), full(wf), full(wp), full(ps), full(wo), modc(3), modc(4), full(g2n), full(wr_t),
        ],
        out_specs=[tok(D_MODEL), tok(D_MODEL), pl.BlockSpec((1, N_EXPERTS, tm), lambda bi, i: (bi, 0, i))],
        out_shape=[
            jax.ShapeDtypeStruct((b, seq, D_MODEL), F32),
            jax.ShapeDtypeStruct((b, seq, D_MODEL), BF16),
            jax.ShapeDtypeStruct((b, N_EXPERTS, seq), F32),
        ],
        compiler_params=_cparams(("parallel", "parallel"), 48),
        name="merge",
    )(att, four, pooled, pb3, pb3, pb3, x3, mod3, wao, wf, wp, ps, wo, mod3, mod3, g2n, wr_t)


def _exclusive_prefix(mask, tri):
    e, seq = mask.shape
    ones = jnp.where(mask, 1.0, 0.0)
    offs = jnp.zeros((e, 1), F32)
    pieces = []
    for k in range(seq // LANE):
        blk = ones[:, k * LANE:(k + 1) * LANE]
        local = jnp.dot(blk.astype(BF16), tri, preferred_element_type=F32)
        pieces.append(local + offs)
        offs = offs + jnp.sum(blk, axis=1, keepdims=True)
    return jnp.concatenate(pieces, axis=1)


META_TSTART, META_TEND, META_PSTART, META_PEND, META_W = 0, 4, 8, 12, 16
GATHER_SLOT_BLOCK = 128
GATHER_WIN = 1408
GATHER_ALIGN = 128
COMBINE_TILE = 1024
COMBINE_WIN = 192
COMBINE_ALIGN = 16


def _select_kernel(lg_ref, slot_ref, aff_ref, meta_ref, *, cap):
    z = lg_ref[0]
    z = z - jnp.max(z, axis=0, keepdims=True)
    ez = jnp.exp(z)
    a = ez / jnp.sum(ez, axis=0, keepdims=True)
    aff_ref[0] = a
    capf = float(cap)

    def count_ge(th):
        return jnp.sum(jnp.where(a >= th, 1.0, 0.0), axis=1, keepdims=True)

    def bisect(_, lohi):
        lo, hi = lohi
        q2 = (lo + hi) * 0.5
        q1 = (lo + q2) * 0.5
        q3 = (q2 + hi) * 0.5
        g1, g2, g3 = count_ge(q1) >= capf, count_ge(q2) >= capf, count_ge(q3) >= capf
        new_lo = jnp.where(g3, q3, jnp.where(g2, q2, jnp.where(g1, q1, lo)))
        new_hi = jnp.where(g3, hi, jnp.where(g2, q3, jnp.where(g1, q2, q1)))
        return new_lo, new_hi

    e = a.shape[0]
    lo, hi = lax.fori_loop(0, SELECT_ITERS, bisect,
                           (jnp.zeros((e, 1), F32), jnp.full((e, 1), 2.0, F32)))
    r_i = lax.broadcasted_iota(I32, (LANE, LANE), 0)
    c_i = lax.broadcasted_iota(I32, (LANE, LANE), 1)
    tri = jnp.where(r_i < c_i, 1.0, 0.0).astype(BF16)
    above = a >= hi
    n_above = jnp.sum(jnp.where(above, 1.0, 0.0), axis=1, keepdims=True)
    tied = (a >= lo) & jnp.logical_not(above)
    tie_rank = _exclusive_prefix(tied, tri)
    sel = above | (tied & (tie_rank < capf - n_above))
    pos = _exclusive_prefix(sel, tri)
    slot_ref[0] = jnp.where(sel, pos, -1.0).astype(I32)

    seq = a.shape[1]
    t = lax.broadcasted_iota(I32, (e, seq), 1).astype(F32)
    lane = lax.broadcasted_iota(I32, (e, LANE), 1)
    meta = jnp.zeros((e, LANE), F32)
    sb = min(GATHER_SLOT_BLOCK, cap)
    for s in range(cap // sb):
        first = jnp.min(jnp.where(sel & (pos >= float(s * sb)), t, float(seq)), axis=1, keepdims=True)
        last = jnp.max(jnp.where(sel & (pos < float((s + 1) * sb)), t, -1.0), axis=1, keepdims=True)
        meta = jnp.where(lane == META_TSTART + s, first, meta)
        meta = jnp.where(lane == META_TEND + s, last, meta)
    tt = min(COMBINE_TILE, seq)
    for i in range(seq // tt):
        before = jnp.sum(jnp.where(sel & (t < float(i * tt)), 1.0, 0.0), axis=1, keepdims=True)
        upto = jnp.sum(jnp.where(sel & (t < float((i + 1) * tt)), 1.0, 0.0), axis=1, keepdims=True)
        meta = jnp.where(lane == META_PSTART + i, before, meta)
        meta = jnp.where(lane == META_PEND + i, upto, meta)
    meta_ref[0] = meta.astype(I32)


def _select(logits_t, cap):
    b, e, seq = logits_t.shape
    assert cap // min(GATHER_SLOT_BLOCK, cap) <= 4 and seq // min(COMBINE_TILE, seq) <= 4
    spec = pl.BlockSpec((1, e, seq), lambda bi: (bi, 0, 0))
    mspec = pl.BlockSpec((1, e, LANE), lambda bi: (bi, 0, 0))
    return pl.pallas_call(
        functools.partial(_select_kernel, cap=cap),
        grid=(b,),
        in_specs=[spec],
        out_specs=[spec, spec, mspec],
        out_shape=[jax.ShapeDtypeStruct((b, e, seq), I32), jax.ShapeDtypeStruct((b, e, seq), F32),
                   jax.ShapeDtypeStruct((b, e, LANE), I32)],
        compiler_params=_cparams(("parallel",), 32),
        name="select",
    )(logits_t)


def _align_down(v, align):
    shift = align.bit_length() - 1
    return lax.shift_left(lax.shift_right_logical(v, shift), shift)


def _gather_kernel(meta_ref, h_ref, slot_ref, aff_ref, xg_ref, gate_ref, *, cap, sb, win):
    seq = h_ref.shape[1]
    base = (pl.program_id(0) * N_EXPERTS + pl.program_id(1)) * META_W

    def gather_block(s, start, width):
        tok = pl.ds(start, width)
        j = lax.broadcasted_iota(I32, (sb, width), 0) + s * sb
        hit = slot_ref[0, 0, :, tok] == j
        onehot = jnp.where(hit, 1.0, 0.0).astype(BF16)
        rows = slice(s * sb, (s + 1) * sb)
        xg_ref[0, 0, rows, :] = jnp.dot(onehot, h_ref[0, tok, :], preferred_element_type=F32).astype(BF16)
        gate_ref[0, 0, rows, :] = jnp.sum(jnp.where(hit, aff_ref[0, 0, :, tok], 0.0), axis=1, keepdims=True)

    for s in range(cap // sb):
        if win >= seq:
            gather_block(s, 0, seq)
            continue
        first = meta_ref[base + META_TSTART + s]
        last = meta_ref[base + META_TEND + s]
        start = pl.multiple_of(jnp.minimum(_align_down(first, GATHER_ALIGN), seq - win), GATHER_ALIGN)
        fits = last < start + win

        @pl.when(fits)
        def _(s=s, start=start):
            gather_block(s, start, win)

        @pl.when(jnp.logical_not(fits))
        def _(s=s):
            gather_block(s, 0, seq)


def _gather(meta, h2, slots4, aff4, cap):
    b, seq, _ = h2.shape
    e = slots4.shape[1]
    sb = min(GATHER_SLOT_BLOCK, cap)
    win = GATHER_WIN if seq > GATHER_WIN else seq
    assert (seq - win) % GATHER_ALIGN == 0
    row = pl.BlockSpec((1, 1, 1, seq), lambda bi, ei, m: (bi, ei, 0, 0))
    return pl.pallas_call(
        functools.partial(_gather_kernel, cap=cap, sb=sb, win=win),
        grid_spec=pltpu.PrefetchScalarGridSpec(
            num_scalar_prefetch=1,
            grid=(b, e),
            in_specs=[pl.BlockSpec((1, seq, D_MODEL), lambda bi, ei, m: (bi, 0, 0)), row, row],
            out_specs=[
                pl.BlockSpec((1, 1, cap, D_MODEL), lambda bi, ei, m: (ei, bi, 0, 0)),
                pl.BlockSpec((1, 1, cap, 1), lambda bi, ei, m: (ei, bi, 0, 0)),
            ],
        ),
        out_shape=[
            jax.ShapeDtypeStruct((e, b, cap, D_MODEL), BF16),
            jax.ShapeDtypeStruct((e, b, cap, 1), F32),
        ],
        compiler_params=_cparams(("parallel", "arbitrary"), 48),
        name="gather",
    )(meta, h2, slots4, aff4)


FF_CHUNK = 256
FFN_ROW_TILES = 2


def _ffn_kernel(*refs, with_ctx):
    if with_ctx:
        x_ref, gate_ref, xc_ref, gatec_ref, wg_ref, wu_ref, wd_ref, o_ref, oc_ref, acc_ref, accc_ref = refs
    else:
        x_ref, gate_ref, wg_ref, wu_ref, wd_ref, o_ref, acc_ref = refs
    fc = pl.program_id(2)
    last_fc = pl.num_programs(2) - 1

    def swiglu_chunk(x_ref, gate_ref, o_ref, acc_ref):
        @pl.when(fc == 0)
        def _():
            acc_ref[...] = jnp.zeros_like(acc_ref)

        x = x_ref[0]
        a = jnp.dot(x, wg_ref[0, 0].astype(BF16), preferred_element_type=F32)
        u = jnp.dot(x, wu_ref[0, 0].astype(BF16), preferred_element_type=F32)
        hmid = (a * jax.nn.sigmoid(a) * u).astype(BF16)
        acc_ref[...] += jnp.dot(hmid, wd_ref[0, 0].astype(BF16), preferred_element_type=F32)

        @pl.when(fc == last_fc)
        def _():
            o_ref[0] = (acc_ref[...] * gate_ref[0]).astype(BF16)

    swiglu_chunk(x_ref, gate_ref, o_ref, acc_ref)
    if with_ctx:
        @pl.when(pl.program_id(1) == pl.num_programs(1) - 1)
        def _():
            swiglu_chunk(xc_ref, gatec_ref, oc_ref, accc_ref)


def _ffn(xg3, gate3, xc3, gatec3, wg_all, wu_all, wd_all, layer):
    e, m, _ = xg3.shape
    tm = m // FFN_ROW_TILES
    assert m % tm == 0 and tm % 16 == 0
    with_ctx = xc3 is not None
    row = lambda w: pl.BlockSpec((1, tm, w), lambda ei, i, fc: (ei, i, 0))
    in_specs = [row(D_MODEL), row(1)]
    out_specs = [row(D_MODEL)]
    out_shape = [jax.ShapeDtypeStruct((e, m, D_MODEL), BF16)]
    scratch = [pltpu.VMEM((tm, D_MODEL), F32)]
    args = [xg3, gate3]
    if with_ctx:
        mc = xc3.shape[1]
        crow = lambda w: pl.BlockSpec((1, mc, w), lambda ei, i, fc: (ei, 0, 0))
        in_specs += [crow(D_MODEL), crow(1)]
        out_specs.append(crow(D_MODEL))
        out_shape.append(jax.ShapeDtypeStruct((e, mc, D_MODEL), BF16))
        scratch.append(pltpu.VMEM((mc, D_MODEL), F32))
        args += [xc3, gatec3]
    in_specs += [
        pl.BlockSpec((1, 1, D_MODEL, FF_CHUNK), lambda ei, i, fc: (layer, ei, 0, fc)),
        pl.BlockSpec((1, 1, D_MODEL, FF_CHUNK), lambda ei, i, fc: (layer, ei, 0, fc)),
        pl.BlockSpec((1, 1, FF_CHUNK, D_MODEL), lambda ei, i, fc: (layer, ei, fc, 0)),
    ]
    res = pl.pallas_call(
        functools.partial(_ffn_kernel, with_ctx=with_ctx),
        grid=(e, m // tm, EXPERT_FF // FF_CHUNK),
        in_specs=in_specs,
        out_specs=out_specs,
        out_shape=out_shape,
        scratch_shapes=scratch,
        compiler_params=_cparams(("parallel", "arbitrary", "arbitrary"), 56),
        name="ffn",
    )(*args, wg_all, wu_all, wd_all)
    return (res[0], res[1]) if with_ctx else (res[0], None)


COMBINE_EXPERTS = 4


def _combine_kernel(meta_ref, slot_ref, gy_ref, x_ref, g2_ref, gf_ref, o_ref, acc_ref, *, cap, win, final):
    bi = pl.program_id(0)
    ti = pl.program_id(1)
    ec = pl.program_id(2)
    tt = x_ref.shape[1]
    tn = (((0,), (0,)), ((), ()))

    @pl.when(ec == 0)
    def _():
        acc_ref[...] = jnp.zeros_like(acc_ref)

    def scatter(starts, width):
        j = lax.broadcasted_iota(I32, (width, tt), 0)
        onehot = jnp.concatenate(
            [jnp.where(slot_ref[0, k] == j + starts[k], 1.0, 0.0).astype(BF16) for k in range(COMBINE_EXPERTS)],
            axis=0)
        gy = jnp.concatenate([gy_ref[k, pl.ds(starts[k], width), :] for k in range(COMBINE_EXPERTS)], axis=0)
        acc_ref[...] += lax.dot_general(onehot, gy, tn, preferred_element_type=F32)

    if win >= cap:
        scatter([0] * COMBINE_EXPERTS, cap)
    else:
        starts = []
        fits = None
        for k in range(COMBINE_EXPERTS):
            base = (bi * N_EXPERTS + ec * COMBINE_EXPERTS + k) * META_W
            before = meta_ref[base + META_PSTART + ti]
            upto = meta_ref[base + META_PEND + ti]
            start = pl.multiple_of(jnp.minimum(_align_down(before, COMBINE_ALIGN), cap - win), COMBINE_ALIGN)
            starts.append(start)
            ok = upto <= start + win
            fits = ok if fits is None else jnp.logical_and(fits, ok)

        @pl.when(fits)
        def _():
            scatter(starts, win)

        @pl.when(jnp.logical_not(fits))
        def _():
            scatter([0] * COMBINE_EXPERTS, cap)

    @pl.when(ec == pl.num_programs(2) - 1)
    def _():
        xn = x_ref[0] + g2_ref[0] * acc_ref[...]
        if final:
            ms = jnp.mean(xn * xn, axis=-1, keepdims=True)
            xn = (xn * lax.rsqrt(ms + RMS_EPS)) * gf_ref[...]
        o_ref[0] = xn


def _combine(meta, slots4, gy3, row_off, x3, mod3, gfinal, *, cap, ctx_row, final):
    b, seq, _ = x3.shape
    e = slots4.shape[1]
    tt = min(seq, COMBINE_TILE)
    win = COMBINE_WIN if cap > COMBINE_WIN else cap
    assert row_off % cap == 0 and (cap - win) % COMBINE_ALIGN == 0
    blk_off = row_off // cap
    mrow = (lambda bi: bi) if ctx_row is None else (lambda bi: ctx_row)
    return pl.pallas_call(
        functools.partial(_combine_kernel, cap=cap, win=win, final=final),
        grid_spec=pltpu.PrefetchScalarGridSpec(
            num_scalar_prefetch=1,
            grid=(b, seq // tt, e // COMBINE_EXPERTS),
            in_specs=[
                pl.BlockSpec((1, COMBINE_EXPERTS, 1, tt), lambda bi, i, ec, m: (bi, ec, 0, i)),
                pl.BlockSpec((COMBINE_EXPERTS, cap, D_MODEL), lambda bi, i, ec, m: (ec, blk_off + bi, 0)),
                pl.BlockSpec((1, tt, D_MODEL), lambda bi, i, ec, m: (bi, i, 0)),
                pl.BlockSpec((1, 1, D_MODEL), lambda bi, i, ec, m: (mrow(bi), 0, 5)),
                pl.BlockSpec((1, D_MODEL), lambda bi, i, ec, m: (0, 0)),
            ],
            out_specs=pl.BlockSpec((1, tt, D_MODEL), lambda bi, i, ec, m: (bi, i, 0)),
            scratch_shapes=[pltpu.VMEM((tt, D_MODEL), F32)],
        ),
        out_shape=jax.ShapeDtypeStruct((b, seq, D_MODEL), F32),
        compiler_params=_cparams(("parallel", "parallel", "arbitrary"), 48),
        name="combine",
    )(meta, slots4, gy3, x3, mod3, gfinal)


def _moe_route(h2, logits_t):
    b, seq, _ = h2.shape
    cap = EC_CAPACITY_FACTOR * seq // N_EXPERTS
    slots, aff, meta = _select(logits_t, cap)
    meta = meta[:, :, :META_W].reshape(-1)
    slots4 = slots.reshape(b, N_EXPERTS, 1, seq)
    aff4 = aff.reshape(b, N_EXPERTS, 1, seq)
    xg, gate = _gather(meta, h2, slots4, aff4, cap)
    return meta, slots4, xg.reshape(N_EXPERTS, b * cap, D_MODEL), gate.reshape(N_EXPERTS, b * cap, 1), cap


def kernel(x, c, ctx, c_ctx, ada_w, ada_b, norm1_g, norm2_g, w_in, rpb, w_att_o, w_fourier, w_pool,
           pool_scale, w_out, w_router, w_exp_gate, w_exp_up, w_exp_down, final_norm_g):
    b, seq, d = x.shape
    lc = ctx.shape[1]
    assert d == D_MODEL and seq % (GRID_W * ATT_QROWS) == 0 and b + 1 <= MOD_ROWS
    rows = seq // GRID_W
    ctx_row = b

    cond = jnp.concatenate([c, c_ctx[None, :], jnp.zeros((MOD_ROWS - b - 1, d), F32)], axis=0)
    cl, sl = _dft_mats(seq, seq // 2)
    clc, slc = _dft_mats(lc, lc // 2)
    cc, sc = _dft_mats(FOURIER_GROUP_W, FOURIER_GROUP_W)
    gfinal = final_norm_g.reshape(1, d)

    for i in range(DEPTH):
        update_ctx = i < DEPTH - 1
        mod3 = _adaln(cond, ada_w[i], ada_b[i]).reshape(MOD_ROWS, 1, 6 * d)
        g1n = norm1_g[i].reshape(1, d)
        g2n = norm2_g[i].reshape(1, d)
        w_in_b = w_in[i].astype(BF16)
        wao = w_att_o[i].astype(BF16)
        wf = w_fourier[i].astype(BF16)
        wp = w_pool[i].astype(BF16)
        ps = pool_scale[i].reshape(1, d)
        wo = w_out[i].astype(BF16)
        wr_t = jnp.pad(w_router[i], ((0, 0), (0, LANE - N_EXPERTS)))
        bias = _bias_table(rpb[i], rows)

        pb, pp = _modproj(x.reshape(b * seq, d), mod3, g1n, w_in_b, seq=seq, ctx_row=None,
                          n_out=IN_W, with_pool=True)
        n_ctx = IN_W if update_ctx else Q_OFF
        pcb, pcp = _modproj(ctx.reshape(b * lc, d), mod3, g1n, w_in_b[:, :n_ctx], seq=lc, ctx_row=ctx_row,
                            n_out=n_ctx, with_pool=update_ctx)
        pb3 = pb.reshape(b, seq, IN_W)
        pcb3 = pcb.reshape(b, lc, n_ctx)

        att = _nattn(pb3, pcb3, bias)
        four = _fourier(pb3, cc, sc, cl, sl)
        pooled = _pool(pp.reshape(b, seq, POOL_W))
        x, h2, logits_t = _merge(att, four, pooled, pb3, x, mod3, wao, wf, wp, ps, wo, g2n, wr_t, ctx_row=None)
        meta_x, slots_x, xg, gate, cap = _moe_route(h2, logits_t)

        if update_ctx:
            att_c = _cattn(pcb3)
            four_c = _fourier(pcb3, cc, sc, clc, slc)
            pooled_c = _pool(pcp.reshape(b, lc, POOL_W))
            ctx, h2_c, logits_c = _merge(att_c, four_c, pooled_c, pcb3, ctx, mod3, wao, wf, wp, ps, wo, g2n, wr_t,
                                         ctx_row=ctx_row)
            meta_c, slots_c, xg_c, gate_c, cap_c = _moe_route(h2_c, logits_c)
        else:
            xg_c = gate_c = None

        gy, gy_c = _ffn(xg, gate, xg_c, gate_c, w_exp_gate, w_exp_up, w_exp_down, i)
        if update_ctx:
            ctx = _combine(meta_c, slots_c, gy_c, 0, ctx, mod3, gfinal, cap=cap_c, ctx_row=ctx_row, final=False)
        x = _combine(meta_x, slots_x, gy, 0, x, mod3, gfinal, cap=cap, ctx_row=None, final=not update_ctx)
    return x
```

```python
import functools
import math

import jax
import jax.numpy as jnp
from jax import lax
from jax.experimental import pallas as pl
from jax.experimental.pallas import tpu as pltpu

F32 = jnp.float32
BF16 = jnp.bfloat16
I32 = jnp.int32
HIGHEST = lax.Precision.HIGHEST

D_MODEL = 1024
DEPTH = 2
GRID_W = 64
N_HEADS = 16
HEAD_DIM = 64
WIN_R = 8
WIN_C = 16
FOURIER_GROUPS = 4
FOURIER_GROUP_W = 128
FOURIER_W = 512
POOL_WINDOWS = (2, 4, 8, 16)
POOL_GROUP_W = 128
POOL_W = 512
POOL_OUT_GROUP = 256
K_OFF, V_OFF, Q_OFF, F_OFF, P_OFF, G_OFF = 0, 1024, 2048, 3072, 3584, 4096
IN_W = 7168
N_EXPERTS = 16
EC_CAPACITY_FACTOR = 2
EXPERT_FF = 2816
RMS_EPS = 1e-6

LANE = 128
HEAD_PAIR_W = 2 * HEAD_DIM
N_HEAD_PAIRS = N_HEADS // 2
MOD_ROWS = 16
NEG_BIG = -1e30
ATT_QROWS = 2
ATT_KROWS = 10
ATT_TOP = (WIN_R // 2 + ATT_QROWS - 1) // ATT_QROWS
ATT_BOT = (WIN_R // 2 - 1 + ATT_QROWS - 1) // ATT_QROWS
ATT_CLASSES = ATT_TOP + 1 + ATT_BOT
ATT_ALIGN = 64
SELECT_ITERS = 32
MiB = 1024 * 1024


def _cparams(sem, vmem_mib):
    return pltpu.CompilerParams(dimension_semantics=sem, vmem_limit_bytes=vmem_mib * MiB)


def _adaln_kernel(c_ref, w_ref, b_ref, o_ref):
    c = c_ref[...]
    s = c * jax.nn.sigmoid(c)
    o_ref[...] = jnp.dot(s, w_ref[...], precision=HIGHEST, preferred_element_type=F32) + b_ref[...]


def _adaln(cond_rows, ada_w, ada_b):
    n = ada_w.shape[1]
    tn = 1024
    return pl.pallas_call(
        _adaln_kernel,
        grid=(n // tn,),
        in_specs=[
            pl.BlockSpec((MOD_ROWS, D_MODEL), lambda j: (0, 0)),
            pl.BlockSpec((D_MODEL, tn), lambda j: (0, j)),
            pl.BlockSpec((1, tn), lambda j: (0, j)),
        ],
        out_specs=pl.BlockSpec((MOD_ROWS, tn), lambda j: (0, j)),
        out_shape=jax.ShapeDtypeStruct((MOD_ROWS, n), F32),
        compiler_params=_cparams(("arbitrary",), 32),
        name="adaln",
    )(cond_rows, ada_w, ada_b.reshape(1, n))


def _modulate(x, g, shift, scale):
    ms = jnp.mean(x * x, axis=-1, keepdims=True)
    y = x * lax.rsqrt(ms + RMS_EPS)
    return (y * g) * (1.0 + scale) + shift


MODPROJ_TN = 1792


def _modproj_kernel(x_ref, sh_ref, sc_ref, g_ref, w_ref, *rest, pool_tile):
    if pool_tile is None:
        o_ref, h_ref = rest
        pp_ref = None
    else:
        o_ref, pp_ref, h_ref = rest
    j = pl.program_id(1)

    @pl.when(j == 0)
    def _():
        h = _modulate(x_ref[...], g_ref[...], sh_ref[0], sc_ref[0])
        h_ref[...] = h.astype(BF16)

    acc = jnp.dot(h_ref[...], w_ref[...], preferred_element_type=F32)
    o_ref[...] = acc.astype(BF16)
    if pool_tile is not None:
        @pl.when(j == pool_tile)
        def _():
            lo = P_OFF % acc.shape[1]
            pp_ref[...] = acc[:, lo:lo + POOL_W]


def _modproj(x2, mod3, gain, w, *, seq, ctx_row, n_out, with_pool):
    rows = x2.shape[0]
    tn = MODPROJ_TN if n_out % MODPROJ_TN == 0 else 1024
    if ctx_row is None:
        tm = min(seq, 1024)
        tiles_per_seq = seq // tm
        mrow = lambda i: i // tiles_per_seq
    else:
        tm = min(rows, 1024)
        mrow = lambda i: ctx_row
    assert rows % tm == 0 and n_out % tn == 0 and (not with_pool or P_OFF % tn + POOL_W <= tn)
    pool_tile = (P_OFF // tn) if with_pool else None
    out_shape = [jax.ShapeDtypeStruct((rows, n_out), BF16)]
    out_specs = [pl.BlockSpec((tm, tn), lambda i, j: (i, j))]
    if with_pool:
        out_shape.append(jax.ShapeDtypeStruct((rows, POOL_W), F32))
        out_specs.append(pl.BlockSpec((tm, POOL_W), lambda i, j: (i, 0)))
    res = pl.pallas_call(
        functools.partial(_modproj_kernel, pool_tile=pool_tile),
        grid=(rows // tm, n_out // tn),
        in_specs=[
            pl.BlockSpec((tm, D_MODEL), lambda i, j: (i, 0)),
            pl.BlockSpec((1, 1, D_MODEL), lambda i, j: (mrow(i), 0, 0)),
            pl.BlockSpec((1, 1, D_MODEL), lambda i, j: (mrow(i), 0, 1)),
            pl.BlockSpec((1, D_MODEL), lambda i, j: (0, 0)),
            pl.BlockSpec((D_MODEL, tn), lambda i, j: (0, j)),
        ],
        out_specs=out_specs,
        out_shape=out_shape,
        scratch_shapes=[pltpu.VMEM((tm, D_MODEL), BF16)],
        compiler_params=_cparams(("parallel", "arbitrary"), 48),
        name="modproj",
    )(x2, mod3, mod3, gain, w)
    return res if with_pool else (res[0], None)


def _bias_kernel(rpb_ref, o_ref, *, rows):
    h = pl.program_id(0)
    cls = pl.program_id(1)
    i_rep = jnp.where(cls <= ATT_TOP, cls, cls - (ATT_TOP + 1) + (rows // ATT_QROWS - ATT_BOT))
    s = jnp.clip(ATT_QROWS * i_rep - WIN_R // 2, 0, rows - ATT_KROWS)
    qc = lax.broadcasted_iota(I32, (GRID_W, LANE), 0)
    lane = lax.broadcasted_iota(I32, (GRID_W, LANE), 1)
    kc = lane & (GRID_W - 1)
    first_half = lane < GRID_W
    cs = jnp.clip(qc - WIN_C // 2, 0, GRID_W - WIN_C)
    col_valid = (kc >= cs) & (kc < cs + WIN_C)
    dcol = kc - qc + (WIN_C - 1)
    n_coff = 2 * WIN_C - 1
    n_roff = 2 * WIN_R - 1
    for ri in range(ATT_QROWS):
        r = ATT_QROWS * i_rep + ri
        rs = jnp.clip(r - WIN_R // 2, 0, rows - WIN_R)
        for m in range(ATT_KROWS // 2):
            krow_a = s + 2 * m
            krow_b = krow_a + 1
            va = ((krow_a >= rs) & (krow_a < rs + WIN_R)).astype(I32)
            vb = ((krow_b >= rs) & (krow_b < rs + WIN_R)).astype(I32)
            base_a = (h * n_roff + jnp.clip(krow_a - r + WIN_R - 1, 0, n_roff - 1)) * n_coff
            base_b = (h * n_roff + jnp.clip(krow_b - r + WIN_R - 1, 0, n_roff - 1)) * n_coff
            acc = jnp.zeros((GRID_W, LANE), F32)
            for c in range(n_coff):
                val = jnp.where(first_half, rpb_ref[base_a + c], rpb_ref[base_b + c])
                acc = jnp.where(dcol == c, val, acc)
            row_valid = jnp.where(first_half, va, vb) > 0
            tile = jnp.where(col_valid & row_valid, acc, NEG_BIG)
            o_ref[0, 0, ri * GRID_W:(ri + 1) * GRID_W, m * LANE:(m + 1) * LANE] = tile


def _bias_table(rpb, rows):
    nq = ATT_QROWS * GRID_W
    nk = ATT_KROWS * GRID_W
    return pl.pallas_call(
        functools.partial(_bias_kernel, rows=rows),
        grid_spec=pltpu.PrefetchScalarGridSpec(
            num_scalar_prefetch=1,
            grid=(N_HEADS, ATT_CLASSES),
            in_specs=[],
            out_specs=pl.BlockSpec((1, 1, nq, nk), lambda h, c, rpb: (h, c, 0, 0)),
        ),
        out_shape=jax.ShapeDtypeStruct((N_HEADS, ATT_CLASSES, nq, nk), F32),
        compiler_params=_cparams(("arbitrary", "arbitrary"), 32),
        name="bias_table",
    )(rpb.reshape(-1))


_NT = (((1,), (1,)), ((), ()))


def _scores(qm, kw, kc, bias):
    sw = lax.dot_general(qm, kw, _NT, preferred_element_type=F32)
    if bias is not None:
        sw = sw + bias
    sc = lax.dot_general(qm, kc, _NT, preferred_element_type=F32) if kc is not None else None
    return sw, sc


def _probs(sw, sc):
    m = jnp.max(sw, axis=-1, keepdims=True)
    if sc is not None:
        m = jnp.maximum(m, jnp.max(sc, axis=-1, keepdims=True))
    pw = jnp.exp(sw - m)
    l = jnp.sum(pw, axis=-1, keepdims=True)
    pc = None
    if sc is not None:
        pc = jnp.exp(sc - m)
        l = l + jnp.sum(pc, axis=-1, keepdims=True)
        pc = pc.astype(BF16)
    return pw.astype(BF16), pc, l


def _pv(pw, pc, l, vw, vc):
    o = jnp.dot(pw, vw, preferred_element_type=F32)
    if pc is not None:
        o = o + jnp.dot(pc, vc, preferred_element_type=F32)
    return o / l


ATT_UNROLL = 4


ATT_KEY_TILE = 256


def _key_tiles(n):
    return [(off, min(ATT_KEY_TILE, n - off)) for off in range(0, n, ATT_KEY_TILE)]


def _lane_fold(x, op):
    out = x[:, :LANE]
    for c in range(1, x.shape[1] // LANE):
        out = op(out, x[:, c * LANE:(c + 1) * LANE])
    return out


def _nattn_kernel(k_ref, v_ref, q_ref, kc_ref, vc_ref, b_ref, o_ref, s_ref, m_ref, *, rows):
    nq = ATT_QROWS * GRID_W
    nk = ATT_KROWS * GRID_W
    lc = kc_ref.shape[1]
    n_steps = rows // ATT_QROWS
    lane = lax.broadcasted_iota(I32, (nq, HEAD_PAIR_W), 1)
    second = lane >= HEAD_DIM
    scale = HEAD_DIM ** -0.5
    win_tiles = _key_tiles(nk)
    ctx_tiles = _key_tiles(lc)

    ones_cols = jnp.ones((ATT_KEY_TILE, HEAD_PAIR_W), BF16)

    def with_ones(v):
        return jnp.concatenate([v, ones_cols[:v.shape[0]]], axis=1)

    def geometry(ii, u):
        i = ii * ATT_UNROLL + u
        s = jnp.clip(ATT_QROWS * i - WIN_R // 2, 0, rows - ATT_KROWS)
        kstart = pl.multiple_of(s * GRID_W, ATT_ALIGN)
        qstart = pl.multiple_of(i * nq, ATT_ALIGN)
        cls = jnp.where(i < ATT_TOP, i,
                        jnp.where(i < n_steps - ATT_BOT, ATT_TOP, i - (n_steps - ATT_BOT) + ATT_TOP + 1))
        return kstart, qstart, cls

    def pass1(ii, buf, u, hh):
        kstart, qstart, cls = geometry(ii, u)
        c = 2 * u + hh
        q2 = (q_ref[0, pl.ds(qstart, nq), :].astype(F32) * scale).astype(BF16)
        head_lanes = second if hh else jnp.logical_not(second)
        qm = jnp.where(head_lanes, q2, jnp.zeros_like(q2))
        m_run = None
        for off, width in win_tiles:
            kt = k_ref[0, pl.ds(kstart + off, width), :]
            st = lax.dot_general(qm, kt, _NT, preferred_element_type=F32) + b_ref[hh, cls, :, off:off + width]
            s_ref[buf, c, :, off:off + width] = st
            mt = _lane_fold(st, jnp.maximum)
            m_run = mt if m_run is None else jnp.maximum(m_run, mt)
        for off, width in ctx_tiles:
            st = lax.dot_general(qm, kc_ref[0, off:off + width, :], _NT, preferred_element_type=F32)
            s_ref[buf, c, :, nk + off:nk + off + width] = st
            m_run = jnp.maximum(m_run, _lane_fold(st, jnp.maximum))
        m_ref[buf, c] = jnp.max(m_run, axis=-1, keepdims=True)

    def pass2(ii, buf, u, hh):
        kstart, _, _ = geometry(ii, u)
        c = 2 * u + hh
        m = m_ref[buf, c]
        acc = None
        for off, width in win_tiles:
            pt = jnp.exp(s_ref[buf, c, :, off:off + width] - m).astype(BF16)
            pv = jnp.dot(pt, with_ones(v_ref[0, pl.ds(kstart + off, width), :]), preferred_element_type=F32)
            acc = pv if acc is None else acc + pv
        for off, width in ctx_tiles:
            pt = jnp.exp(s_ref[buf, c, :, nk + off:nk + off + width] - m).astype(BF16)
            acc = acc + jnp.dot(pt, with_ones(vc_ref[0, off:off + width, :]), preferred_element_type=F32)
        return acc[:, :HEAD_PAIR_W] / acc[:, HEAD_PAIR_W:]

    def store(ii, u, o0, o1):
        _, qstart, _ = geometry(ii, u)
        o_ref[0, pl.ds(qstart, nq), :] = jnp.where(second, o1, o0).astype(BF16)

    n_iter = n_steps // ATT_UNROLL
    for u in range(ATT_UNROLL):
        for hh in range(2):
            pass1(0, 0, u, hh)

    def overlapped(ii, new):
        for u in range(ATT_UNROLL):
            outs = []
            for hh in range(2):
                pass1(ii, new, u, hh)
                outs.append(pass2(ii - 1, 1 - new, u, hh))
            store(ii - 1, u, *outs)

    def body(jj, carry):
        overlapped(2 * jj + 1, 1)
        overlapped(2 * jj + 2, 0)
        return carry

    assert n_iter % 2 == 0
    lax.fori_loop(0, n_iter // 2 - 1, body, 0)
    last = n_iter - 1
    overlapped(last, 1)
    for u in range(ATT_UNROLL):
        store(last, u, *[pass2(last, 1, u, hh) for hh in range(2)])


def _nattn(pb3, pcb3, bias):
    b, seq, _ = pb3.shape
    lc = pcb3.shape[1]
    rows = seq // GRID_W
    nq = ATT_QROWS * GRID_W
    nk = ATT_KROWS * GRID_W
    kblk, vblk, qblk = K_OFF // LANE, V_OFF // LANE, Q_OFF // LANE
    return pl.pallas_call(
        functools.partial(_nattn_kernel, rows=rows),
        grid=(N_HEAD_PAIRS, b),
        in_specs=[
            pl.BlockSpec((1, seq, HEAD_PAIR_W), lambda hp, bi: (bi, 0, kblk + hp)),
            pl.BlockSpec((1, seq, HEAD_PAIR_W), lambda hp, bi: (bi, 0, vblk + hp)),
            pl.BlockSpec((1, seq, HEAD_PAIR_W), lambda hp, bi: (bi, 0, qblk + hp)),
            pl.BlockSpec((1, lc, HEAD_PAIR_W), lambda hp, bi: (bi, 0, kblk + hp)),
            pl.BlockSpec((1, lc, HEAD_PAIR_W), lambda hp, bi: (bi, 0, vblk + hp)),
            pl.BlockSpec((2, ATT_CLASSES, nq, nk), lambda hp, bi: (hp, 0, 0, 0)),
        ],
        out_specs=pl.BlockSpec((1, seq, HEAD_PAIR_W), lambda hp, bi: (bi, 0, hp)),
        out_shape=jax.ShapeDtypeStruct((b, seq, N_HEADS * HEAD_DIM), BF16),
        scratch_shapes=[pltpu.VMEM((2, 2 * ATT_UNROLL, nq, nk + lc), F32),
                        pltpu.VMEM((2, 2 * ATT_UNROLL, nq, 1), F32)],
        compiler_params=_cparams(("parallel", "parallel"), 40),
        name="nattn",
    )(pb3, pb3, pb3, pcb3, pcb3, bias)


def _cattn_kernel(k_ref, v_ref, q_ref, o_ref):
    lc = q_ref.shape[1]
    lane = lax.broadcasted_iota(I32, (lc, HEAD_PAIR_W), 1)
    second = lane >= HEAD_DIM
    q2 = (q_ref[0].astype(F32) * (HEAD_DIM ** -0.5)).astype(BF16)
    k = k_ref[0]
    v = v_ref[0]
    outs = []
    for hh in range(2):
        head_lanes = second if hh else jnp.logical_not(second)
        qm = jnp.where(head_lanes, q2, jnp.zeros_like(q2))
        pw, _, l = _probs(*_scores(qm, k, None, None))
        outs.append(_pv(pw, None, l, v, None))
    o_ref[0] = jnp.where(second, outs[1], outs[0]).astype(BF16)


def _cattn(pcb3):
    b, lc, _ = pcb3.shape
    kblk, vblk, qblk = K_OFF // LANE, V_OFF // LANE, Q_OFF // LANE
    return pl.pallas_call(
        _cattn_kernel,
        grid=(b, N_HEAD_PAIRS),
        in_specs=[
            pl.BlockSpec((1, lc, HEAD_PAIR_W), lambda bi, hp: (bi, 0, kblk + hp)),
            pl.BlockSpec((1, lc, HEAD_PAIR_W), lambda bi, hp: (bi, 0, vblk + hp)),
            pl.BlockSpec((1, lc, HEAD_PAIR_W), lambda bi, hp: (bi, 0, qblk + hp)),
        ],
        out_specs=pl.BlockSpec((1, lc, HEAD_PAIR_W), lambda bi, hp: (bi, 0, hp)),
        out_shape=jax.ShapeDtypeStruct((b, lc, N_HEADS * HEAD_DIM), BF16),
        compiler_params=_cparams(("parallel", "parallel"), 32),
        name="cattn",
    )(pcb3, pcb3, pcb3)


def _dft_kernel(c_ref, s_ref, *, n):
    tk, ncols = c_ref.shape
    k = pl.program_id(0) * tk + lax.broadcasted_iota(I32, (tk, ncols), 0)
    t = lax.broadcasted_iota(I32, (tk, ncols), 1)
    m = (k * t) & (n - 1)
    ang = m.astype(F32) * (2.0 * math.pi / n)
    c_ref[...] = jnp.cos(ang).astype(BF16)
    s_ref[...] = jnp.sin(ang).astype(BF16)


def _dft_mats(n, ncols):
    tk = min(n, 256)
    return pl.pallas_call(
        functools.partial(_dft_kernel, n=n),
        grid=(n // tk,),
        in_specs=[],
        out_specs=[pl.BlockSpec((tk, ncols), lambda i: (i, 0))] * 2,
        out_shape=[jax.ShapeDtypeStruct((n, ncols), BF16)] * 2,
        compiler_params=_cparams(("parallel",), 48),
        name=f"dft_mats_{n}",
    )()


REV_BLOCK = 128


def _fourier_kernel(u_ref, cc_ref, sc_ref, cl_ref, sl_ref, o_ref, us_ref, ud_ref, a_ref, b_ref, ah_ref, *, seq):
    half = seq // 2
    nblk = seq // REV_BLOCK
    tk = o_ref.shape[1]

    @pl.when(pl.program_id(1) == 0)
    def _():
        d_i = lax.broadcasted_iota(I32, (REV_BLOCK, REV_BLOCK), 0)
        s_i = lax.broadcasted_iota(I32, (REV_BLOCK, REV_BLOCK), 1)
        flip = jnp.where((d_i >= 1) & (s_i == REV_BLOCK - d_i), 1.0, 0.0).astype(BF16)
        row = lax.broadcasted_iota(I32, (REV_BLOCK, FOURIER_W), 0)
        for blk in range(nblk // 2):
            lo = u_ref[0, blk * REV_BLOCK:(blk + 1) * REV_BLOCK, :].astype(F32)
            src = u_ref[0, (nblk - 1 - blk) * REV_BLOCK:(nblk - blk) * REV_BLOCK, :]
            rev = jnp.dot(flip, src, preferred_element_type=F32)
            if blk > 0:
                head = u_ref[0, (nblk - blk) * REV_BLOCK:(nblk - blk) * REV_BLOCK + 16, :].astype(F32)
                rev = jnp.where(row == 0, head[0:1, :], rev)
            rows = slice(blk * REV_BLOCK, (blk + 1) * REV_BLOCK)
            us_ref[rows, :] = (lo + rev).astype(BF16)
            ud_ref[rows, :] = (lo - rev).astype(BF16)
        mid = u_ref[0, half:half + 16, :]
        for g in range(FOURIER_GROUPS):
            sl = slice(g * FOURIER_GROUP_W, (g + 1) * FOURIER_GROUP_W)
            a_ref[:, sl] = jnp.dot(us_ref[:, sl], cc_ref[...], preferred_element_type=F32).astype(BF16)
            b_ref[:, sl] = jnp.dot(ud_ref[:, sl], sc_ref[...], preferred_element_type=F32).astype(BF16)
            ah_ref[:, sl] = jnp.dot(mid[:, sl], cc_ref[...], preferred_element_type=F32)

    k = pl.program_id(1) * tk + lax.broadcasted_iota(I32, (tk, 1), 0)
    sign = (1 - 2 * (k & 1)).astype(F32)
    y = (jnp.dot(cl_ref[...], a_ref[...], preferred_element_type=F32)
         - jnp.dot(sl_ref[...], b_ref[...], preferred_element_type=F32)
         + sign * ah_ref[0:1, :])
    o_ref[0] = (y * (1.0 / math.sqrt(seq * FOURIER_GROUP_W))).astype(BF16)


def _fourier(pb3, cc, sc, cl, sl):
    b, seq, _ = pb3.shape
    tk = min(seq, 512)
    half = seq // 2
    assert seq % (2 * REV_BLOCK) == 0
    return pl.pallas_call(
        functools.partial(_fourier_kernel, seq=seq),
        grid=(b, seq // tk),
        in_specs=[
            pl.BlockSpec((1, seq, FOURIER_W), lambda bi, k: (bi, 0, F_OFF // FOURIER_W)),
            pl.BlockSpec((FOURIER_GROUP_W, FOURIER_GROUP_W), lambda bi, k: (0, 0)),
            pl.BlockSpec((FOURIER_GROUP_W, FOURIER_GROUP_W), lambda bi, k: (0, 0)),
            pl.BlockSpec((tk, half), lambda bi, k: (k, 0)),
            pl.BlockSpec((tk, half), lambda bi, k: (k, 0)),
        ],
        out_specs=pl.BlockSpec((1, tk, FOURIER_W), lambda bi, k: (bi, k, 0)),
        out_shape=jax.ShapeDtypeStruct((b, seq, FOURIER_W), BF16),
        scratch_shapes=[pltpu.VMEM((half, FOURIER_W), BF16)] * 4 + [pltpu.VMEM((16, FOURIER_W), F32)],
        compiler_params=_cparams(("parallel", "arbitrary"), 48),
        name="fourier",
    )(pb3, cc, sc, cl, sl)


POOL_PAD = 8


def _pool_kernel(u_ref, o_ref, pad_ref, *, seq):
    t = lax.broadcasted_iota(I32, (seq, POOL_GROUP_W), 0)
    zeros = jnp.zeros((POOL_PAD, POOL_GROUP_W), F32)
    pad_ref[0:POOL_PAD, :] = zeros
    pad_ref[seq + POOL_PAD:seq + 2 * POOL_PAD, :] = zeros
    pad_ref[POOL_PAD:seq + POOL_PAD, :] = u_ref[0]
    for g, w in enumerate(POOL_WINDOWS):
        @pl.when(pl.program_id(1) == g)
        def _(w=w):
            acc = None
            for d in range(-(w // 2), w - w // 2):
                term = pad_ref[pl.ds(POOL_PAD + d, seq), :]
                acc = term if acc is None else acc + term
            cnt = (jnp.minimum(t + (w - w // 2), seq) - jnp.maximum(t - w // 2, 0)).astype(F32)
            o_ref[0] = (acc / cnt - u_ref[0]).astype(BF16)


def _pool(pp3):
    b, seq, _ = pp3.shape
    spec = pl.BlockSpec((1, seq, POOL_GROUP_W), lambda bi, g: (bi, 0, g))
    return pl.pallas_call(
        functools.partial(_pool_kernel, seq=seq),
        grid=(b, len(POOL_WINDOWS)),
        in_specs=[spec],
        out_specs=spec,
        out_shape=jax.ShapeDtypeStruct((b, seq, POOL_W), BF16),
        scratch_shapes=[pltpu.VMEM((seq + 2 * POOL_PAD, POOL_GROUP_W), F32)],
        compiler_params=_cparams(("parallel", "parallel"), 32),
        name="pool",
    )(pp3)


MERGE_PARTS = 4


def _merge_kernel(att_ref, four_ref, pool_ref, ga_ref, gf_ref, gp_ref, x_ref, g1_ref,
                  wao_ref, wf_ref, wp_ref, ps_ref, wo_ref, sh2_ref, sc2_ref, g2n_ref, wr_ref,
                  o_ref, h_ref, lg_ref):
    tm = x_ref.shape[1]
    n_parts = min(MERGE_PARTS, tm // LANE)
    parts = [slice(p * (tm // n_parts), (p + 1) * (tm // n_parts)) for p in range(n_parts)]
    wr = wr_ref[...]
    wr_hi = wr.astype(BF16)
    wr_lo = (wr - wr_hi.astype(F32)).astype(BF16)
    wr_split = jnp.where(lax.broadcasted_iota(I32, wr.shape, 1) < N_EXPERTS, wr_hi, wr_lo)

    def sigmoid(ref, rows):
        return 0.5 * jnp.tanh(0.5 * ref[0, rows, :].astype(F32)) + 0.5

    branches = []
    for rows in parts:
        y_att = jnp.dot(att_ref[0, rows, :], wao_ref[...], preferred_element_type=F32)
        y_four = jnp.dot(four_ref[0, rows, :], wf_ref[...], preferred_element_type=F32)
        pooled = pool_ref[0, rows, :]
        y_pool = jnp.concatenate(
            [jnp.dot(pooled[:, g * POOL_GROUP_W:(g + 1) * POOL_GROUP_W], wp_ref[g], preferred_element_type=F32)
             for g in range(len(POOL_WINDOWS))], axis=-1) * ps_ref[...]
        branches.append((y_att, y_four, y_pool))
    resid = []
    for rows, (y_att, y_four, y_pool) in zip(parts, branches):
        merged = (sigmoid(ga_ref, rows) * y_att + sigmoid(gf_ref, rows) * y_four + sigmoid(gp_ref, rows) * y_pool)
        y = jnp.dot(merged.astype(BF16), wo_ref[...], preferred_element_type=F32)
        xn = x_ref[0, rows, :] + g1_ref[0] * y
        o_ref[0, rows, :] = xn
        resid.append(xn)
    for rows, xn in zip(parts, resid):
        n = xn.shape[0]
        h = _modulate(xn, g2n_ref[...], sh2_ref[0], sc2_ref[0])
        h_hi = h.astype(BF16)
        h_ref[0, rows, :] = h_hi
        h_lo = (h - h_hi.astype(F32)).astype(BF16)
        r = jnp.dot(jnp.concatenate([h_hi, h_lo], axis=0), wr_split, preferred_element_type=F32)
        hi_t = r[:n].T
        lo_t = r[n:].T
        lg_ref[0, :, rows] = (hi_t[:N_EXPERTS] + hi_t[N_EXPERTS:2 * N_EXPERTS]
                              + lo_t[:N_EXPERTS] + lo_t[N_EXPERTS:2 * N_EXPERTS])


def _merge(att, four, pooled, pb3, x3, mod3, wao, wf, wp, ps, wo, g2n, wr_t, *, ctx_row):
    b, seq, _ = x3.shape
    tm = min(seq, 512)
    gblk = G_OFF // D_MODEL
    mrow = (lambda bi: bi) if ctx_row is None else (lambda bi: ctx_row)
    tok = lambda w: pl.BlockSpec((1, tm, w), lambda bi, i: (bi, i, 0))
    gate = lambda k: pl.BlockSpec((1, tm, D_MODEL), lambda bi, i: (bi, i, gblk + k))
    full = lambda a: pl.BlockSpec(a.shape, lambda bi, i: (0,) * a.ndim)
    modc = lambda k: pl.BlockSpec((1, 1, D_MODEL), lambda bi, i: (mrow(bi), 0, k))
    return pl.pallas_call(
        _merge_kernel,
        grid=(b, seq // tm),
        in_specs=[
            tok(D_MODEL), tok(FOURIER_W), tok(POOL_W), gate(0), gate(1), gate(2), tok(D_MODEL), modc(2),
            full(wao), full(wf), full(wp), full(ps), full(wo), modc(3), modc(4), full(g2n), full(wr_t),
        ],
        out_specs=[tok(D_MODEL), tok(D_MODEL), pl.BlockSpec((1, N_EXPERTS, tm), lambda bi, i: (bi, 0, i))],
        out_shape=[
            jax.ShapeDtypeStruct((b, seq, D_MODEL), F32),
            jax.ShapeDtypeStruct((b, seq, D_MODEL), BF16),
            jax.ShapeDtypeStruct((b, N_EXPERTS, seq), F32),
        ],
        compiler_params=_cparams(("parallel", "parallel"), 48),
        name="merge",
    )(att, four, pooled, pb3, pb3, pb3, x3, mod3, wao, wf, wp, ps, wo, mod3, mod3, g2n, wr_t)


def _exclusive_prefix(mask, tri):
    e, seq = mask.shape
    ones = jnp.where(mask, 1.0, 0.0)
    offs = jnp.zeros((e, 1), F32)
    pieces = []
    for k in range(seq // LANE):
        blk = ones[:, k * LANE:(k + 1) * LANE]
        local = jnp.dot(blk.astype(BF16), tri, preferred_element_type=F32)
        pieces.append(local + offs)
        offs = offs + jnp.sum(blk, axis=1, keepdims=True)
    return jnp.concatenate(pieces, axis=1)


META_TSTART, META_TEND, META_PSTART, META_PEND, META_W = 0, 4, 8, 12, 16
GATHER_SLOT_BLOCK = 128
GATHER_WIN = 1408
GATHER_ALIGN = 128
COMBINE_TILE = 1024
COMBINE_WIN = 192
COMBINE_ALIGN = 16


def _select_kernel(lg_ref, slot_ref, aff_ref, meta_ref, *, cap):
    z = lg_ref[0]
    z = z - jnp.max(z, axis=0, keepdims=True)
    ez = jnp.exp(z)
    a = ez / jnp.sum(ez, axis=0, keepdims=True)
    aff_ref[0] = a
    capf = float(cap)

    def count_ge(th):
        return jnp.sum(jnp.where(a >= th, 1.0, 0.0), axis=1, keepdims=True)

    def bisect(_, lohi):
        lo, hi = lohi
        q2 = (lo + hi) * 0.5
        q1 = (lo + q2) * 0.5
        q3 = (q2 + hi) * 0.5
        g1, g2, g3 = count_ge(q1) >= capf, count_ge(q2) >= capf, count_ge(q3) >= capf
        new_lo = jnp.where(g3, q3, jnp.where(g2, q2, jnp.where(g1, q1, lo)))
        new_hi = jnp.where(g3, hi, jnp.where(g2, q3, jnp.where(g1, q2, q1)))
        return new_lo, new_hi

    e = a.shape[0]
    lo, hi = lax.fori_loop(0, SELECT_ITERS, bisect,
                           (jnp.zeros((e, 1), F32), jnp.full((e, 1), 2.0, F32)))
    r_i = lax.broadcasted_iota(I32, (LANE, LANE), 0)
    c_i = lax.broadcasted_iota(I32, (LANE, LANE), 1)
    tri = jnp.where(r_i < c_i, 1.0, 0.0).astype(BF16)
    above = a >= hi
    n_above = jnp.sum(jnp.where(above, 1.0, 0.0), axis=1, keepdims=True)
    tied = (a >= lo) & jnp.logical_not(above)
    tie_rank = _exclusive_prefix(tied, tri)
    sel = above | (tied & (tie_rank < capf - n_above))
    pos = _exclusive_prefix(sel, tri)
    slot_ref[0] = jnp.where(sel, pos, -1.0).astype(I32)

    seq = a.shape[1]
    t = lax.broadcasted_iota(I32, (e, seq), 1).astype(F32)
    lane = lax.broadcasted_iota(I32, (e, LANE), 1)
    meta = jnp.zeros((e, LANE), F32)
    sb = min(GATHER_SLOT_BLOCK, cap)
    for s in range(cap // sb):
        first = jnp.min(jnp.where(sel & (pos >= float(s * sb)), t, float(seq)), axis=1, keepdims=True)
        last = jnp.max(jnp.where(sel & (pos < float((s + 1) * sb)), t, -1.0), axis=1, keepdims=True)
        meta = jnp.where(lane == META_TSTART + s, first, meta)
        meta = jnp.where(lane == META_TEND + s, last, meta)
    tt = min(COMBINE_TILE, seq)
    for i in range(seq // tt):
        before = jnp.sum(jnp.where(sel & (t < float(i * tt)), 1.0, 0.0), axis=1, keepdims=True)
        upto = jnp.sum(jnp.where(sel & (t < float((i + 1) * tt)), 1.0, 0.0), axis=1, keepdims=True)
        meta = jnp.where(lane == META_PSTART + i, before, meta)
        meta = jnp.where(lane == META_PEND + i, upto, meta)
    meta_ref[0] = meta.astype(I32)


def _select(logits_t, cap):
    b, e, seq = logits_t.shape
    assert cap // min(GATHER_SLOT_BLOCK, cap) <= 4 and seq // min(COMBINE_TILE, seq) <= 4
    spec = pl.BlockSpec((1, e, seq), lambda bi: (bi, 0, 0))
    mspec = pl.BlockSpec((1, e, LANE), lambda bi: (bi, 0, 0))
    return pl.pallas_call(
        functools.partial(_select_kernel, cap=cap),
        grid=(b,),
        in_specs=[spec],
        out_specs=[spec, spec, mspec],
        out_shape=[jax.ShapeDtypeStruct((b, e, seq), I32), jax.ShapeDtypeStruct((b, e, seq), F32),
                   jax.ShapeDtypeStruct((b, e, LANE), I32)],
        compiler_params=_cparams(("parallel",), 32),
        name="select",
    )(logits_t)


def _align_down(v, align):
    shift = align.bit_length() - 1
    return lax.shift_left(lax.shift_right_logical(v, shift), shift)


def _gather_kernel(meta_ref, h_ref, slot_ref, aff_ref, xg_ref, gate_ref, *, cap, sb, win):
    seq = h_ref.shape[1]
    base = (pl.program_id(0) * N_EXPERTS + pl.program_id(1)) * META_W

    def gather_block(s, start, width):
        tok = pl.ds(start, width)
        j = lax.broadcasted_iota(I32, (sb, width), 0) + s * sb
        hit = slot_ref[0, 0, :, tok] == j
        onehot = jnp.where(hit, 1.0, 0.0).astype(BF16)
        rows = slice(s * sb, (s + 1) * sb)
        xg_ref[0, 0, rows, :] = jnp.dot(onehot, h_ref[0, tok, :], preferred_element_type=F32).astype(BF16)
        gate_ref[0, 0, rows, :] = jnp.sum(jnp.where(hit, aff_ref[0, 0, :, tok], 0.0), axis=1, keepdims=True)

    for s in range(cap // sb):
        if win >= seq:
            gather_block(s, 0, seq)
            continue
        first = meta_ref[base + META_TSTART + s]
        last = meta_ref[base + META_TEND + s]
        start = pl.multiple_of(jnp.minimum(_align_down(first, GATHER_ALIGN), seq - win), GATHER_ALIGN)
        fits = last < start + win

        @pl.when(fits)
        def _(s=s, start=start):
            gather_block(s, start, win)

        @pl.when(jnp.logical_not(fits))
        def _(s=s):
            gather_block(s, 0, seq)


def _gather(meta, h2, slots4, aff4, cap):
    b, seq, _ = h2.shape
    e = slots4.shape[1]
    sb = min(GATHER_SLOT_BLOCK, cap)
    win = GATHER_WIN if seq > GATHER_WIN else seq
    assert (seq - win) % GATHER_ALIGN == 0
    row = pl.BlockSpec((1, 1, 1, seq), lambda bi, ei, m: (bi, ei, 0, 0))
    return pl.pallas_call(
        functools.partial(_gather_kernel, cap=cap, sb=sb, win=win),
        grid_spec=pltpu.PrefetchScalarGridSpec(
            num_scalar_prefetch=1,
            grid=(b, e),
            in_specs=[pl.BlockSpec((1, seq, D_MODEL), lambda bi, ei, m: (bi, 0, 0)), row, row],
            out_specs=[
                pl.BlockSpec((1, 1, cap, D_MODEL), lambda bi, ei, m: (ei, bi, 0, 0)),
                pl.BlockSpec((1, 1, cap, 1), lambda bi, ei, m: (ei, bi, 0, 0)),
            ],
        ),
        out_shape=[
            jax.ShapeDtypeStruct((e, b, cap, D_MODEL), BF16),
            jax.ShapeDtypeStruct((e, b, cap, 1), F32),
        ],
        compiler_params=_cparams(("parallel", "arbitrary"), 48),
        name="gather",
    )(meta, h2, slots4, aff4)


FF_CHUNK = 256
FFN_ROW_TILES = 2


def _ffn_kernel(*refs, with_ctx):
    if with_ctx:
        x_ref, gate_ref, xc_ref, gatec_ref, wg_ref, wu_ref, wd_ref, o_ref, oc_ref, acc_ref, accc_ref = refs
    else:
        x_ref, gate_ref, wg_ref, wu_ref, wd_ref, o_ref, acc_ref = refs
    fc = pl.program_id(2)
    last_fc = pl.num_programs(2) - 1

    def swiglu_chunk(x_ref, gate_ref, o_ref, acc_ref):
        @pl.when(fc == 0)
        def _():
            acc_ref[...] = jnp.zeros_like(acc_ref)

        x = x_ref[0]
        a = jnp.dot(x, wg_ref[0, 0].astype(BF16), preferred_element_type=F32)
        u = jnp.dot(x, wu_ref[0, 0].astype(BF16), preferred_element_type=F32)
        hmid = (a * jax.nn.sigmoid(a) * u).astype(BF16)
        acc_ref[...] += jnp.dot(hmid, wd_ref[0, 0].astype(BF16), preferred_element_type=F32)

        @pl.when(fc == last_fc)
        def _():
            o_ref[0] = (acc_ref[...] * gate_ref[0]).astype(BF16)

    swiglu_chunk(x_ref, gate_ref, o_ref, acc_ref)
    if with_ctx:
        @pl.when(pl.program_id(1) == pl.num_programs(1) - 1)
        def _():
            swiglu_chunk(xc_ref, gatec_ref, oc_ref, accc_ref)


def _ffn(xg3, gate3, xc3, gatec3, wg_all, wu_all, wd_all, layer):
    e, m, _ = xg3.shape
    tm = m // FFN_ROW_TILES
    assert m % tm == 0 and tm % 16 == 0
    with_ctx = xc3 is not None
    row = lambda w: pl.BlockSpec((1, tm, w), lambda ei, i, fc: (ei, i, 0))
    in_specs = [row(D_MODEL), row(1)]
    out_specs = [row(D_MODEL)]
    out_shape = [jax.ShapeDtypeStruct((e, m, D_MODEL), BF16)]
    scratch = [pltpu.VMEM((tm, D_MODEL), F32)]
    args = [xg3, gate3]
    if with_ctx:
        mc = xc3.shape[1]
        crow = lambda w: pl.BlockSpec((1, mc, w), lambda ei, i, fc: (ei, 0, 0))
        in_specs += [crow(D_MODEL), crow(1)]
        out_specs.append(crow(D_MODEL))
        out_shape.append(jax.ShapeDtypeStruct((e, mc, D_MODEL), BF16))
        scratch.append(pltpu.VMEM((mc, D_MODEL), F32))
        args += [xc3, gatec3]
    in_specs += [
        pl.BlockSpec((1, 1, D_MODEL, FF_CHUNK), lambda ei, i, fc: (layer, ei, 0, fc)),
        pl.BlockSpec((1, 1, D_MODEL, FF_CHUNK), lambda ei, i, fc: (layer, ei, 0, fc)),
        pl.BlockSpec((1, 1, FF_CHUNK, D_MODEL), lambda ei, i, fc: (layer, ei, fc, 0)),
    ]
    res = pl.pallas_call(
        functools.partial(_ffn_kernel, with_ctx=with_ctx),
        grid=(e, m // tm, EXPERT_FF // FF_CHUNK),
        in_specs=in_specs,
        out_specs=out_specs,
        out_shape=out_shape,
        scratch_shapes=scratch,
        compiler_params=_cparams(("parallel", "arbitrary", "arbitrary"), 56),
        name="ffn",
    )(*args, wg_all, wu_all, wd_all)
    return (res[0], res[1]) if with_ctx else (res[0], None)


COMBINE_EXPERTS = 4


def _combine_kernel(meta_ref, slot_ref, gy_ref, x_ref, g2_ref, gf_ref, o_ref, acc_ref, *, cap, win, final):
    bi = pl.program_id(0)
    ti = pl.program_id(1)
    ec = pl.program_id(2)
    tt = x_ref.shape[1]
    tn = (((0,), (0,)), ((), ()))

    @pl.when(ec == 0)
    def _():
        acc_ref[...] = jnp.zeros_like(acc_ref)

    def scatter(starts, width):
        j = lax.broadcasted_iota(I32, (width, tt), 0)
        onehot = jnp.concatenate(
            [jnp.where(slot_ref[0, k] == j + starts[k], 1.0, 0.0).astype(BF16) for k in range(COMBINE_EXPERTS)],
            axis=0)
        gy = jnp.concatenate([gy_ref[k, pl.ds(starts[k], width), :] for k in range(COMBINE_EXPERTS)], axis=0)
        acc_ref[...] += lax.dot_general(onehot, gy, tn, preferred_element_type=F32)

    if win >= cap:
        scatter([0] * COMBINE_EXPERTS, cap)
    else:
        starts = []
        fits = None
        for k in range(COMBINE_EXPERTS):
            base = (bi * N_EXPERTS + ec * COMBINE_EXPERTS + k) * META_W
            before = meta_ref[base + META_PSTART + ti]
            upto = meta_ref[base + META_PEND + ti]
            start = pl.multiple_of(jnp.minimum(_align_down(before, COMBINE_ALIGN), cap - win), COMBINE_ALIGN)
            starts.append(start)
            ok = upto <= start + win
            fits = ok if fits is None else jnp.logical_and(fits, ok)

        @pl.when(fits)
        def _():
            scatter(starts, win)

        @pl.when(jnp.logical_not(fits))
        def _():
            scatter([0] * COMBINE_EXPERTS, cap)

    @pl.when(ec == pl.num_programs(2) - 1)
    def _():
        xn = x_ref[0] + g2_ref[0] * acc_ref[...]
        if final:
            ms = jnp.mean(xn * xn, axis=-1, keepdims=True)
            xn = (xn * lax.rsqrt(ms + RMS_EPS)) * gf_ref[...]
        o_ref[0] = xn


def _combine(meta, slots4, gy3, row_off, x3, mod3, gfinal, *, cap, ctx_row, final):
    b, seq, _ = x3.shape
    e = slots4.shape[1]
    tt = min(seq, COMBINE_TILE)
    win = COMBINE_WIN if cap > COMBINE_WIN else cap
    assert row_off % cap == 0 and (cap - win) % COMBINE_ALIGN == 0
    blk_off = row_off // cap
    mrow = (lambda bi: bi) if ctx_row is None else (lambda bi: ctx_row)
    return pl.pallas_call(
        functools.partial(_combine_kernel, cap=cap, win=win, final=final),
        grid_spec=pltpu.PrefetchScalarGridSpec(
            num_scalar_prefetch=1,
            grid=(b, seq // tt, e // COMBINE_EXPERTS),
            in_specs=[
                pl.BlockSpec((1, COMBINE_EXPERTS, 1, tt), lambda bi, i, ec, m: (bi, ec, 0, i)),
                pl.BlockSpec((COMBINE_EXPERTS, cap, D_MODEL), lambda bi, i, ec, m: (ec, blk_off + bi, 0)),
                pl.BlockSpec((1, tt, D_MODEL), lambda bi, i, ec, m: (bi, i, 0)),
                pl.BlockSpec((1, 1, D_MODEL), lambda bi, i, ec, m: (mrow(bi), 0, 5)),
                pl.BlockSpec((1, D_MODEL), lambda bi, i, ec, m: (0, 0)),
            ],
            out_specs=pl.BlockSpec((1, tt, D_MODEL), lambda bi, i, ec, m: (bi, i, 0)),
            scratch_shapes=[pltpu.VMEM((tt, D_MODEL), F32)],
        ),
        out_shape=jax.ShapeDtypeStruct((b, seq, D_MODEL), F32),
        compiler_params=_cparams(("parallel", "parallel", "arbitrary"), 48),
        name="combine",
    )(meta, slots4, gy3, x3, mod3, gfinal)


def _moe_route(h2, logits_t):
    b, seq, _ = h2.shape
    cap = EC_CAPACITY_FACTOR * seq // N_EXPERTS
    slots, aff, meta = _select(logits_t, cap)
    meta = meta[:, :, :META_W].reshape(-1)
    slots4 = slots.reshape(b, N_EXPERTS, 1, seq)
    aff4 = aff.reshape(b, N_EXPERTS, 1, seq)
    xg, gate = _gather(meta, h2, slots4, aff4, cap)
    return meta, slots4, xg.reshape(N_EXPERTS, b * cap, D_MODEL), gate.reshape(N_EXPERTS, b * cap, 1), cap


def kernel(x, c, ctx, c_ctx, ada_w, ada_b, norm1_g, norm2_g, w_in, rpb, w_att_o, w_fourier, w_pool,
           pool_scale, w_out, w_router, w_exp_gate, w_exp_up, w_exp_down, final_norm_g):
    b, seq, d = x.shape
    lc = ctx.shape[1]
    assert d == D_MODEL and seq % (GRID_W * ATT_QROWS) == 0 and b + 1 <= MOD_ROWS
    rows = seq // GRID_W
    ctx_row = b

    cond = jnp.concatenate([c, c_ctx[None, :], jnp.zeros((MOD_ROWS - b - 1, d), F32)], axis=0)
    cl, sl = _dft_mats(seq, seq // 2)
    clc, slc = _dft_mats(lc, lc // 2)
    cc, sc = _dft_mats(FOURIER_GROUP_W, FOURIER_GROUP_W)
    gfinal = final_norm_g.reshape(1, d)

    for i in range(DEPTH):
        update_ctx = i < DEPTH - 1
        mod3 = _adaln(cond, ada_w[i], ada_b[i]).reshape(MOD_ROWS, 1, 6 * d)
        g1n = norm1_g[i].reshape(1, d)
        g2n = norm2_g[i].reshape(1, d)
        w_in_b = w_in[i].astype(BF16)
        wao = w_att_o[i].astype(BF16)
        wf = w_fourier[i].astype(BF16)
        wp = w_pool[i].astype(BF16)
        ps = pool_scale[i].reshape(1, d)
        wo = w_out[i].astype(BF16)
        wr_t = jnp.pad(jnp.concatenate([w_router[i], w_router[i]], axis=1), ((0, 0), (0, LANE - 2 * N_EXPERTS)))
        bias = _bias_table(rpb[i], rows)

        pb, pp = _modproj(x.reshape(b * seq, d), mod3, g1n, w_in_b, seq=seq, ctx_row=None,
                          n_out=IN_W, with_pool=True)
        n_ctx = IN_W if update_ctx else Q_OFF
        pcb, pcp = _modproj(ctx.reshape(b * lc, d), mod3, g1n, w_in_b[:, :n_ctx], seq=lc, ctx_row=ctx_row,
                            n_out=n_ctx, with_pool=update_ctx)
        pb3 = pb.reshape(b, seq, IN_W)
        pcb3 = pcb.reshape(b, lc, n_ctx)

        att = _nattn(pb3, pcb3, bias)
        four = _fourier(pb3, cc, sc, cl, sl)
        pooled = _pool(pp.reshape(b, seq, POOL_W))
        x, h2, logits_t = _merge(att, four, pooled, pb3, x, mod3, wao, wf, wp, ps, wo, g2n, wr_t, ctx_row=None)
        meta_x, slots_x, xg, gate, cap = _moe_route(h2, logits_t)

        if update_ctx:
            att_c = _cattn(pcb3)
            four_c = _fourier(pcb3, cc, sc, clc, slc)
            pooled_c = _pool(pcp.reshape(b, lc, POOL_W))
            ctx, h2_c, logits_c = _merge(att_c, four_c, pooled_c, pcb3, ctx, mod3, wao, wf, wp, ps, wo, g2n, wr_t,
                                         ctx_row=ctx_row)
            meta_c, slots_c, xg_c, gate_c, cap_c = _moe_route(h2_c, logits_c)
        else:
            xg_c = gate_c = None

        gy, gy_c = _ffn(xg, gate, xg_c, gate_c, w_exp_gate, w_exp_up, w_exp_down, i)
        if update_ctx:
            ctx = _combine(meta_c, slots_c, gy_c, 0, ctx, mod3, gfinal, cap=cap_c, ctx_row=ctx_row, final=False)
        x = _combine(meta_x, slots_x, gy, 0, x, mod3, gfinal, cap=cap, ctx_row=None, final=not update_ctx)
    return x
```

```python
import functools
import math

import jax
import jax.numpy as jnp
from jax import lax
from jax.experimental import pallas as pl
from jax.experimental.pallas import tpu as pltpu

F32 = jnp.float32
BF16 = jnp.bfloat16
I32 = jnp.int32
HIGHEST = lax.Precision.HIGHEST

D_MODEL = 1024
DEPTH = 2
GRID_W = 64
N_HEADS = 16
HEAD_DIM = 64
WIN_R = 8
WIN_C = 16
FOURIER_GROUPS = 4
FOURIER_GROUP_W = 128
FOURIER_W = 512
POOL_WINDOWS = (2, 4, 8, 16)
POOL_GROUP_W = 128
POOL_W = 512
POOL_OUT_GROUP = 256
K_OFF, V_OFF, Q_OFF, F_OFF, P_OFF, G_OFF = 0, 1024, 2048, 3072, 3584, 4096
IN_W = 7168
N_EXPERTS = 16
EC_CAPACITY_FACTOR = 2
EXPERT_FF = 2816
RMS_EPS = 1e-6

LANE = 128
HEAD_PAIR_W = 2 * HEAD_DIM
N_HEAD_PAIRS = N_HEADS // 2
MOD_ROWS = 16
NEG_BIG = -1e30
ATT_QROWS = 2
ATT_KROWS = 10
ATT_TOP = (WIN_R // 2 + ATT_QROWS - 1) // ATT_QROWS
ATT_BOT = (WIN_R // 2 - 1 + ATT_QROWS - 1) // ATT_QROWS
ATT_CLASSES = ATT_TOP + 1 + ATT_BOT
ATT_ALIGN = 64
SELECT_ITERS = 32
MiB = 1024 * 1024


def _cparams(sem, vmem_mib):
    return pltpu.CompilerParams(dimension_semantics=sem, vmem_limit_bytes=vmem_mib * MiB)


def _adaln_kernel(c_ref, w_ref, b_ref, o_ref):
    c = c_ref[...]
    s = c * jax.nn.sigmoid(c)
    o_ref[...] = jnp.dot(s, w_ref[...], precision=HIGHEST, preferred_element_type=F32) + b_ref[...]


def _adaln(cond_rows, ada_w, ada_b):
    n = ada_w.shape[1]
    tn = 1024
    return pl.pallas_call(
        _adaln_kernel,
        grid=(n // tn,),
        in_specs=[
            pl.BlockSpec((MOD_ROWS, D_MODEL), lambda j: (0, 0)),
            pl.BlockSpec((D_MODEL, tn), lambda j: (0, j)),
            pl.BlockSpec((1, tn), lambda j: (0, j)),
        ],
        out_specs=pl.BlockSpec((MOD_ROWS, tn), lambda j: (0, j)),
        out_shape=jax.ShapeDtypeStruct((MOD_ROWS, n), F32),
        compiler_params=_cparams(("arbitrary",), 32),
        name="adaln",
    )(cond_rows, ada_w, ada_b.reshape(1, n))


def _modulate(x, g, shift, scale):
    ms = jnp.mean(x * x, axis=-1, keepdims=True)
    y = x * lax.rsqrt(ms + RMS_EPS)
    return (y * g) * (1.0 + scale) + shift


MODPROJ_TN = 1792


def _modproj_kernel(x_ref, sh_ref, sc_ref, g_ref, w_ref, *rest, pool_tile):
    if pool_tile is None:
        o_ref, h_ref = rest
        pp_ref = None
    else:
        o_ref, pp_ref, h_ref = rest
    j = pl.program_id(1)

    @pl.when(j == 0)
    def _():
        h = _modulate(x_ref[...], g_ref[...], sh_ref[0], sc_ref[0])
        h_ref[...] = h.astype(BF16)

    acc = jnp.dot(h_ref[...], w_ref[...], preferred_element_type=F32)
    o_ref[...] = acc.astype(BF16)
    if pool_tile is not None:
        @pl.when(j == pool_tile)
        def _():
            lo = P_OFF % acc.shape[1]
            pp_ref[...] = acc[:, lo:lo + POOL_W]


def _modproj(x2, mod3, gain, w, *, seq, ctx_row, n_out, with_pool):
    rows = x2.shape[0]
    tn = MODPROJ_TN if n_out % MODPROJ_TN == 0 else 1024
    if ctx_row is None:
        tm = min(seq, 1024)
        tiles_per_seq = seq // tm
        mrow = lambda i: i // tiles_per_seq
    else:
        tm = min(rows, 1024)
        mrow = lambda i: ctx_row
    assert rows % tm == 0 and n_out % tn == 0 and (not with_pool or P_OFF % tn + POOL_W <= tn)
    pool_tile = (P_OFF // tn) if with_pool else None
    out_shape = [jax.ShapeDtypeStruct((rows, n_out), BF16)]
    out_specs = [pl.BlockSpec((tm, tn), lambda i, j: (i, j))]
    if with_pool:
        out_shape.append(jax.ShapeDtypeStruct((rows, POOL_W), F32))
        out_specs.append(pl.BlockSpec((tm, POOL_W), lambda i, j: (i, 0)))
    res = pl.pallas_call(
        functools.partial(_modproj_kernel, pool_tile=pool_tile),
        grid=(rows // tm, n_out // tn),
        in_specs=[
            pl.BlockSpec((tm, D_MODEL), lambda i, j: (i, 0)),
            pl.BlockSpec((1, 1, D_MODEL), lambda i, j: (mrow(i), 0, 0)),
            pl.BlockSpec((1, 1, D_MODEL), lambda i, j: (mrow(i), 0, 1)),
            pl.BlockSpec((1, D_MODEL), lambda i, j: (0, 0)),
            pl.BlockSpec((D_MODEL, tn), lambda i, j: (0, j)),
        ],
        out_specs=out_specs,
        out_shape=out_shape,
        scratch_shapes=[pltpu.VMEM((tm, D_MODEL), BF16)],
        compiler_params=_cparams(("parallel", "arbitrary"), 48),
        name="modproj",
    )(x2, mod3, mod3, gain, w)
    return res if with_pool else (res[0], None)


def _bias_kernel(rpb_ref, o_ref, *, rows):
    h = pl.program_id(0)
    cls = pl.program_id(1)
    i_rep = jnp.where(cls <= ATT_TOP, cls, cls - (ATT_TOP + 1) + (rows // ATT_QROWS - ATT_BOT))
    s = jnp.clip(ATT_QROWS * i_rep - WIN_R // 2, 0, rows - ATT_KROWS)
    qc = lax.broadcasted_iota(I32, (GRID_W, LANE), 0)
    lane = lax.broadcasted_iota(I32, (GRID_W, LANE), 1)
    kc = lane & (GRID_W - 1)
    first_half = lane < GRID_W
    cs = jnp.clip(qc - WIN_C // 2, 0, GRID_W - WIN_C)
    col_valid = (kc >= cs) & (kc < cs + WIN_C)
    dcol = kc - qc + (WIN_C - 1)
    n_coff = 2 * WIN_C - 1
    n_roff = 2 * WIN_R - 1
    for ri in range(ATT_QROWS):
        r = ATT_QROWS * i_rep + ri
        rs = jnp.clip(r - WIN_R // 2, 0, rows - WIN_R)
        for m in range(ATT_KROWS // 2):
            krow_a = s + 2 * m
            krow_b = krow_a + 1
            va = ((krow_a >= rs) & (krow_a < rs + WIN_R)).astype(I32)
            vb = ((krow_b >= rs) & (krow_b < rs + WIN_R)).astype(I32)
            base_a = (h * n_roff + jnp.clip(krow_a - r + WIN_R - 1, 0, n_roff - 1)) * n_coff
            base_b = (h * n_roff + jnp.clip(krow_b - r + WIN_R - 1, 0, n_roff - 1)) * n_coff
            acc = jnp.zeros((GRID_W, LANE), F32)
            for c in range(n_coff):
                val = jnp.where(first_half, rpb_ref[base_a + c], rpb_ref[base_b + c])
                acc = jnp.where(dcol == c, val, acc)
            row_valid = jnp.where(first_half, va, vb) > 0
            tile = jnp.where(col_valid & row_valid, acc, NEG_BIG)
            o_ref[0, 0, ri * GRID_W:(ri + 1) * GRID_W, m * LANE:(m + 1) * LANE] = tile


def _bias_table(rpb, rows):
    nq = ATT_QROWS * GRID_W
    nk = ATT_KROWS * GRID_W
    return pl.pallas_call(
        functools.partial(_bias_kernel, rows=rows),
        grid_spec=pltpu.PrefetchScalarGridSpec(
            num_scalar_prefetch=1,
            grid=(N_HEADS, ATT_CLASSES),
            in_specs=[],
            out_specs=pl.BlockSpec((1, 1, nq, nk), lambda h, c, rpb: (h, c, 0, 0)),
        ),
        out_shape=jax.ShapeDtypeStruct((N_HEADS, ATT_CLASSES, nq, nk), F32),
        compiler_params=_cparams(("arbitrary", "arbitrary"), 32),
        name="bias_table",
    )(rpb.reshape(-1))


_NT = (((1,), (1,)), ((), ()))


def _scores(qm, kw, kc, bias):
    sw = lax.dot_general(qm, kw, _NT, preferred_element_type=F32)
    if bias is not None:
        sw = sw + bias
    sc = lax.dot_general(qm, kc, _NT, preferred_element_type=F32) if kc is not None else None
    return sw, sc


def _probs(sw, sc):
    m = jnp.max(sw, axis=-1, keepdims=True)
    if sc is not None:
        m = jnp.maximum(m, jnp.max(sc, axis=-1, keepdims=True))
    pw = jnp.exp(sw - m)
    l = jnp.sum(pw, axis=-1, keepdims=True)
    pc = None
    if sc is not None:
        pc = jnp.exp(sc - m)
        l = l + jnp.sum(pc, axis=-1, keepdims=True)
        pc = pc.astype(BF16)
    return pw.astype(BF16), pc, l


def _pv(pw, pc, l, vw, vc):
    o = jnp.dot(pw, vw, preferred_element_type=F32)
    if pc is not None:
        o = o + jnp.dot(pc, vc, preferred_element_type=F32)
    return o / l


ATT_UNROLL = 4


ATT_KEY_TILE = 256


def _key_tiles(n):
    return [(off, min(ATT_KEY_TILE, n - off)) for off in range(0, n, ATT_KEY_TILE)]


def _lane_fold(x, op):
    out = x[:, :LANE]
    for c in range(1, x.shape[1] // LANE):
        out = op(out, x[:, c * LANE:(c + 1) * LANE])
    return out


def _nattn_kernel(k_ref, v_ref, q_ref, kc_ref, vc_ref, b_ref, o_ref, s_ref, m_ref, *, rows):
    nq = ATT_QROWS * GRID_W
    nk = ATT_KROWS * GRID_W
    lc = kc_ref.shape[1]
    n_steps = rows // ATT_QROWS
    lane = lax.broadcasted_iota(I32, (nq, HEAD_PAIR_W), 1)
    second = lane >= HEAD_DIM
    scale = HEAD_DIM ** -0.5
    win_tiles = _key_tiles(nk)
    ctx_tiles = _key_tiles(lc)

    ones_cols = jnp.ones((ATT_KEY_TILE, HEAD_PAIR_W), BF16)

    def with_ones(v):
        return jnp.concatenate([v, ones_cols[:v.shape[0]]], axis=1)

    def geometry(ii, u):
        i = ii * ATT_UNROLL + u
        s = jnp.clip(ATT_QROWS * i - WIN_R // 2, 0, rows - ATT_KROWS)
        kstart = pl.multiple_of(s * GRID_W, ATT_ALIGN)
        qstart = pl.multiple_of(i * nq, ATT_ALIGN)
        cls = jnp.where(i < ATT_TOP, i,
                        jnp.where(i < n_steps - ATT_BOT, ATT_TOP, i - (n_steps - ATT_BOT) + ATT_TOP + 1))
        return kstart, qstart, cls

    def pass1(ii, buf, u, hh):
        kstart, qstart, cls = geometry(ii, u)
        c = 2 * u + hh
        q2 = (q_ref[0, pl.ds(qstart, nq), :].astype(F32) * scale).astype(BF16)
        head_lanes = second if hh else jnp.logical_not(second)
        qm = jnp.where(head_lanes, q2, jnp.zeros_like(q2))
        m_run = None
        for off, width in win_tiles:
            kt = k_ref[0, pl.ds(kstart + off, width), :]
            st = lax.dot_general(qm, kt, _NT, preferred_element_type=F32) + b_ref[hh, cls, :, off:off + width]
            s_ref[buf, c, :, off:off + width] = st
            mt = _lane_fold(st, jnp.maximum)
            m_run = mt if m_run is None else jnp.maximum(m_run, mt)
        for off, width in ctx_tiles:
            st = lax.dot_general(qm, kc_ref[0, off:off + width, :], _NT, preferred_element_type=F32)
            s_ref[buf, c, :, nk + off:nk + off + width] = st
            m_run = jnp.maximum(m_run, _lane_fold(st, jnp.maximum))
        m_ref[buf, c] = jnp.max(m_run, axis=-1, keepdims=True)

    def pass2(ii, buf, u, hh):
        kstart, _, _ = geometry(ii, u)
        c = 2 * u + hh
        m = m_ref[buf, c]
        acc = None
        for off, width in win_tiles:
            pt = jnp.exp(s_ref[buf, c, :, off:off + width] - m).astype(BF16)
            pv = jnp.dot(pt, with_ones(v_ref[0, pl.ds(kstart + off, width), :]), preferred_element_type=F32)
            acc = pv if acc is None else acc + pv
        for off, width in ctx_tiles:
            pt = jnp.exp(s_ref[buf, c, :, nk + off:nk + off + width] - m).astype(BF16)
            acc = acc + jnp.dot(pt, with_ones(vc_ref[0, off:off + width, :]), preferred_element_type=F32)
        return acc[:, :HEAD_PAIR_W] / acc[:, HEAD_PAIR_W:]

    def store(ii, u, o0, o1):
        _, qstart, _ = geometry(ii, u)
        o_ref[0, pl.ds(qstart, nq), :] = jnp.where(second, o1, o0).astype(BF16)

    n_iter = n_steps // ATT_UNROLL
    for u in range(ATT_UNROLL):
        for hh in range(2):
            pass1(0, 0, u, hh)

    def overlapped(ii, new):
        for u in range(ATT_UNROLL):
            outs = []
            for hh in range(2):
                pass1(ii, new, u, hh)
                outs.append(pass2(ii - 1, 1 - new, u, hh))
            store(ii - 1, u, *outs)

    def body(jj, carry):
        overlapped(2 * jj + 1, 1)
        overlapped(2 * jj + 2, 0)
        return carry

    assert n_iter % 2 == 0
    lax.fori_loop(0, n_iter // 2 - 1, body, 0)
    last = n_iter - 1
    overlapped(last, 1)
    for u in range(ATT_UNROLL):
        store(last, u, *[pass2(last, 1, u, hh) for hh in range(2)])


def _nattn(pb3, pcb3, bias):
    b, seq, _ = pb3.shape
    lc = pcb3.shape[1]
    rows = seq // GRID_W
    nq = ATT_QROWS * GRID_W
    nk = ATT_KROWS * GRID_W
    kblk, vblk, qblk = K_OFF // LANE, V_OFF // LANE, Q_OFF // LANE
    return pl.pallas_call(
        functools.partial(_nattn_kernel, rows=rows),
        grid=(N_HEAD_PAIRS, b),
        in_specs=[
            pl.BlockSpec((1, seq, HEAD_PAIR_W), lambda hp, bi: (bi, 0, kblk + hp)),
            pl.BlockSpec((1, seq, HEAD_PAIR_W), lambda hp, bi: (bi, 0, vblk + hp)),
            pl.BlockSpec((1, seq, HEAD_PAIR_W), lambda hp, bi: (bi, 0, qblk + hp)),
            pl.BlockSpec((1, lc, HEAD_PAIR_W), lambda hp, bi: (bi, 0, kblk + hp)),
            pl.BlockSpec((1, lc, HEAD_PAIR_W), lambda hp, bi: (bi, 0, vblk + hp)),
            pl.BlockSpec((2, ATT_CLASSES, nq, nk), lambda hp, bi: (hp, 0, 0, 0)),
        ],
        out_specs=pl.BlockSpec((1, seq, HEAD_PAIR_W), lambda hp, bi: (bi, 0, hp)),
        out_shape=jax.ShapeDtypeStruct((b, seq, N_HEADS * HEAD_DIM), BF16),
        scratch_shapes=[pltpu.VMEM((2, 2 * ATT_UNROLL, nq, nk + lc), F32),
                        pltpu.VMEM((2, 2 * ATT_UNROLL, nq, 1), F32)],
        compiler_params=_cparams(("parallel", "parallel"), 40),
        name="nattn",
    )(pb3, pb3, pb3, pcb3, pcb3, bias)


def _cattn_kernel(k_ref, v_ref, q_ref, o_ref):
    lc = q_ref.shape[1]
    lane = lax.broadcasted_iota(I32, (lc, HEAD_PAIR_W), 1)
    second = lane >= HEAD_DIM
    q2 = (q_ref[0].astype(F32) * (HEAD_DIM ** -0.5)).astype(BF16)
    k = k_ref[0]
    v = v_ref[0]
    outs = []
    for hh in range(2):
        head_lanes = second if hh else jnp.logical_not(second)
        qm = jnp.where(head_lanes, q2, jnp.zeros_like(q2))
        pw, _, l = _probs(*_scores(qm, k, None, None))
        outs.append(_pv(pw, None, l, v, None))
    o_ref[0] = jnp.where(second, outs[1], outs[0]).astype(BF16)


def _cattn(pcb3):
    b, lc, _ = pcb3.shape
    kblk, vblk, qblk = K_OFF // LANE, V_OFF // LANE, Q_OFF // LANE
    return pl.pallas_call(
        _cattn_kernel,
        grid=(b, N_HEAD_PAIRS),
        in_specs=[
            pl.BlockSpec((1, lc, HEAD_PAIR_W), lambda bi, hp: (bi, 0, kblk + hp)),
            pl.BlockSpec((1, lc, HEAD_PAIR_W), lambda bi, hp: (bi, 0, vblk + hp)),
            pl.BlockSpec((1, lc, HEAD_PAIR_W), lambda bi, hp: (bi, 0, qblk + hp)),
        ],
        out_specs=pl.BlockSpec((1, lc, HEAD_PAIR_W), lambda bi, hp: (bi, 0, hp)),
        out_shape=jax.ShapeDtypeStruct((b, lc, N_HEADS * HEAD_DIM), BF16),
        compiler_params=_cparams(("parallel", "parallel"), 32),
        name="cattn",
    )(pcb3, pcb3, pcb3)


def _dft_kernel(c_ref, s_ref, *, n):
    tk, ncols = c_ref.shape
    k = pl.program_id(0) * tk + lax.broadcasted_iota(I32, (tk, LANE), 0)
    lane = lax.broadcasted_iota(I32, (tk, LANE), 1)
    w = 2.0 * math.pi / n
    ang_p = ((k * lane) & (n - 1)).astype(F32) * w
    cp, sp = jnp.cos(ang_p), jnp.sin(ang_p)
    ang_q = ((k * (lane * LANE)) & (n - 1)).astype(F32) * w
    cq, sq = jnp.cos(ang_q), jnp.sin(ang_q)
    for q in range(ncols // LANE):
        cols = slice(q * LANE, (q + 1) * LANE)
        cqq, sqq = cq[:, q:q + 1], sq[:, q:q + 1]
        c_ref[:, cols] = (cqq * cp - sqq * sp).astype(BF16)
        s_ref[:, cols] = (sqq * cp + cqq * sp).astype(BF16)


def _dft_mats(n, ncols):
    tk = min(n, 256)
    return pl.pallas_call(
        functools.partial(_dft_kernel, n=n),
        grid=(n // tk,),
        in_specs=[],
        out_specs=[pl.BlockSpec((tk, ncols), lambda i: (i, 0))] * 2,
        out_shape=[jax.ShapeDtypeStruct((n, ncols), BF16)] * 2,
        compiler_params=_cparams(("parallel",), 48),
        name=f"dft_mats_{n}",
    )()


REV_BLOCK = 128


def _fourier_kernel(u_ref, cc_ref, sc_ref, cl_ref, sl_ref, o_ref, us_ref, ud_ref, a_ref, b_ref, ah_ref, *, seq):
    half = seq // 2
    nblk = seq // REV_BLOCK
    tk = o_ref.shape[1]

    @pl.when(pl.program_id(1) == 0)
    def _():
        d_i = lax.broadcasted_iota(I32, (REV_BLOCK, REV_BLOCK), 0)
        s_i = lax.broadcasted_iota(I32, (REV_BLOCK, REV_BLOCK), 1)
        flip = jnp.where((d_i >= 1) & (s_i == REV_BLOCK - d_i), 1.0, 0.0).astype(BF16)
        row = lax.broadcasted_iota(I32, (REV_BLOCK, FOURIER_W), 0)
        for blk in range(nblk // 2):
            lo = u_ref[0, blk * REV_BLOCK:(blk + 1) * REV_BLOCK, :].astype(F32)
            src = u_ref[0, (nblk - 1 - blk) * REV_BLOCK:(nblk - blk) * REV_BLOCK, :]
            rev = jnp.dot(flip, src, preferred_element_type=F32)
            if blk > 0:
                head = u_ref[0, (nblk - blk) * REV_BLOCK:(nblk - blk) * REV_BLOCK + 16, :].astype(F32)
                rev = jnp.where(row == 0, head[0:1, :], rev)
            rows = slice(blk * REV_BLOCK, (blk + 1) * REV_BLOCK)
            us_ref[rows, :] = (lo + rev).astype(BF16)
            ud_ref[rows, :] = (lo - rev).astype(BF16)
        mid = u_ref[0, half:half + 16, :]
        for g in range(FOURIER_GROUPS):
            sl = slice(g * FOURIER_GROUP_W, (g + 1) * FOURIER_GROUP_W)
            a_ref[:, sl] = jnp.dot(us_ref[:, sl], cc_ref[...], preferred_element_type=F32).astype(BF16)
            b_ref[:, sl] = jnp.dot(ud_ref[:, sl], sc_ref[...], preferred_element_type=F32).astype(BF16)
            ah_ref[:, sl] = jnp.dot(mid[:, sl], cc_ref[...], preferred_element_type=F32)

    k = pl.program_id(1) * tk + lax.broadcasted_iota(I32, (tk, 1), 0)
    sign = (1 - 2 * (k & 1)).astype(F32)
    y = (jnp.dot(cl_ref[...], a_ref[...], preferred_element_type=F32)
         - jnp.dot(sl_ref[...], b_ref[...], preferred_element_type=F32)
         + sign * ah_ref[0:1, :])
    o_ref[0] = (y * (1.0 / math.sqrt(seq * FOURIER_GROUP_W))).astype(BF16)


def _fourier(pb3, cc, sc, cl, sl):
    b, seq, _ = pb3.shape
    tk = min(seq, 512)
    half = seq // 2
    assert seq % (2 * REV_BLOCK) == 0
    return pl.pallas_call(
        functools.partial(_fourier_kernel, seq=seq),
        grid=(b, seq // tk),
        in_specs=[
            pl.BlockSpec((1, seq, FOURIER_W), lambda bi, k: (bi, 0, F_OFF // FOURIER_W)),
            pl.BlockSpec((FOURIER_GROUP_W, FOURIER_GROUP_W), lambda bi, k: (0, 0)),
            pl.BlockSpec((FOURIER_GROUP_W, FOURIER_GROUP_W), lambda bi, k: (0, 0)),
            pl.BlockSpec((tk, half), lambda bi, k: (k, 0)),
            pl.BlockSpec((tk, half), lambda bi, k: (k, 0)),
        ],
        out_specs=pl.BlockSpec((1, tk, FOURIER_W), lambda bi, k: (bi, k, 0)),
        out_shape=jax.ShapeDtypeStruct((b, seq, FOURIER_W), BF16),
        scratch_shapes=[pltpu.VMEM((half, FOURIER_W), BF16)] * 4 + [pltpu.VMEM((16, FOURIER_W), F32)],
        compiler_params=_cparams(("parallel", "arbitrary"), 48),
        name="fourier",
    )(pb3, cc, sc, cl, sl)


POOL_PAD = 8


def _pool_kernel(u_ref, o_ref, pad_ref, *, seq):
    t = lax.broadcasted_iota(I32, (seq, POOL_GROUP_W), 0)
    zeros = jnp.zeros((POOL_PAD, POOL_GROUP_W), F32)
    pad_ref[0:POOL_PAD, :] = zeros
    pad_ref[seq + POOL_PAD:seq + 2 * POOL_PAD, :] = zeros
    pad_ref[POOL_PAD:seq + POOL_PAD, :] = u_ref[0]
    for g, w in enumerate(POOL_WINDOWS):
        @pl.when(pl.program_id(1) == g)
        def _(w=w):
            acc = None
            for d in range(-(w // 2), w - w // 2):
                term = pad_ref[pl.ds(POOL_PAD + d, seq), :]
                acc = term if acc is None else acc + term
            cnt = (jnp.minimum(t + (w - w // 2), seq) - jnp.maximum(t - w // 2, 0)).astype(F32)
            o_ref[0] = (acc / cnt - u_ref[0]).astype(BF16)


def _pool(pp3):
    b, seq, _ = pp3.shape
    spec = pl.BlockSpec((1, seq, POOL_GROUP_W), lambda bi, g: (bi, 0, g))
    return pl.pallas_call(
        functools.partial(_pool_kernel, seq=seq),
        grid=(b, len(POOL_WINDOWS)),
        in_specs=[spec],
        out_specs=spec,
        out_shape=jax.ShapeDtypeStruct((b, seq, POOL_W), BF16),
        scratch_shapes=[pltpu.VMEM((seq + 2 * POOL_PAD, POOL_GROUP_W), F32)],
        compiler_params=_cparams(("parallel", "parallel"), 32),
        name="pool",
    )(pp3)


MERGE_PARTS = 4


def _merge_kernel(att_ref, four_ref, pool_ref, ga_ref, gf_ref, gp_ref, x_ref, g1_ref,
                  wao_ref, wf_ref, wp_ref, ps_ref, wo_ref, sh2_ref, sc2_ref, g2n_ref, wr_ref,
                  o_ref, h_ref, lg_ref):
    tm = x_ref.shape[1]
    n_parts = min(MERGE_PARTS, tm // LANE)
    parts = [slice(p * (tm // n_parts), (p + 1) * (tm // n_parts)) for p in range(n_parts)]
    wr = wr_ref[...]
    wr_hi = wr.astype(BF16)
    wr_lo = (wr - wr_hi.astype(F32)).astype(BF16)
    wr_split = jnp.where(lax.broadcasted_iota(I32, wr.shape, 1) < N_EXPERTS, wr_hi, wr_lo)

    def sigmoid(ref, rows):
        return 0.5 * jnp.tanh(0.5 * ref[0, rows, :].astype(F32)) + 0.5

    branches = []
    for rows in parts:
        y_att = jnp.dot(att_ref[0, rows, :], wao_ref[...], preferred_element_type=F32)
        y_four = jnp.dot(four_ref[0, rows, :], wf_ref[...], preferred_element_type=F32)
        pooled = pool_ref[0, rows, :]
        y_pool = jnp.concatenate(
            [jnp.dot(pooled[:, g * POOL_GROUP_W:(g + 1) * POOL_GROUP_W], wp_ref[g], preferred_element_type=F32)
             for g in range(len(POOL_WINDOWS))], axis=-1) * ps_ref[...]
        branches.append((y_att, y_four, y_pool))
    resid = []
    for rows, (y_att, y_four, y_pool) in zip(parts, branches):
        merged = (sigmoid(ga_ref, rows) * y_att + sigmoid(gf_ref, rows) * y_four + sigmoid(gp_ref, rows) * y_pool)
        y = jnp.dot(merged.astype(BF16), wo_ref[...], preferred_element_type=F32)
        xn = x_ref[0, rows, :] + g1_ref[0] * y
        o_ref[0, rows, :] = xn
        resid.append(xn)
    for rows, xn in zip(parts, resid):
        n = xn.shape[0]
        h = _modulate(xn, g2n_ref[...], sh2_ref[0], sc2_ref[0])
        h_hi = h.astype(BF16)
        h_ref[0, rows, :] = h_hi
        h_lo = (h - h_hi.astype(F32)).astype(BF16)
        r = jnp.dot(jnp.concatenate([h_hi, h_lo], axis=0), wr_split, preferred_element_type=F32)
        hi_t = r[:n].T
        lo_t = r[n:].T
        lg_ref[0, :, rows] = (hi_t[:N_EXPERTS] + hi_t[N_EXPERTS:2 * N_EXPERTS]
                              + lo_t[:N_EXPERTS] + lo_t[N_EXPERTS:2 * N_EXPERTS])


def _merge(att, four, pooled, pb3, x3, mod3, wao, wf, wp, ps, wo, g2n, wr_t, *, ctx_row):
    b, seq, _ = x3.shape
    tm = min(seq, 512)
    gblk = G_OFF // D_MODEL
    mrow = (lambda bi: bi) if ctx_row is None else (lambda bi: ctx_row)
    tok = lambda w: pl.BlockSpec((1, tm, w), lambda bi, i: (bi, i, 0))
    gate = lambda k: pl.BlockSpec((1, tm, D_MODEL), lambda bi, i: (bi, i, gblk + k))
    full = lambda a: pl.BlockSpec(a.shape, lambda bi, i: (0,) * a.ndim)
    modc = lambda k: pl.BlockSpec((1, 1, D_MODEL), lambda bi, i: (mrow(bi), 0, k))
    return pl.pallas_call(
        _merge_kernel,
        grid=(b, seq // tm),
        in_specs=[
            tok(D_MODEL), tok(FOURIER_W), tok(POOL_W), gate(0), gate(1), gate(2), tok(D_MODEL), modc(2),
            full(wao), full(wf), full(wp), full(ps), full(wo), modc(3), modc(4), full(g2n), full(wr_t),
        ],
        out_specs=[tok(D_MODEL), tok(D_MODEL), pl.BlockSpec((1, N_EXPERTS, tm), lambda bi, i: (bi, 0, i))],
        out_shape=[
            jax.ShapeDtypeStruct((b, seq, D_MODEL), F32),
            jax.ShapeDtypeStruct((b, seq, D_MODEL), BF16),
            jax.ShapeDtypeStruct((b, N_EXPERTS, seq), F32),
        ],
        compiler_params=_cparams(("parallel", "parallel"), 48),
        name="merge",
    )(att, four, pooled, pb3, pb3, pb3, x3, mod3, wao, wf, wp, ps, wo, mod3, mod3, g2n, wr_t)


def _exclusive_prefix(mask, tri):
    e, seq = mask.shape
    ones = jnp.where(mask, 1.0, 0.0)
    offs = jnp.zeros((e, 1), F32)
    pieces = []
    for k in range(seq // LANE):
        blk = ones[:, k * LANE:(k + 1) * LANE]
        local = jnp.dot(blk.astype(BF16), tri, preferred_element_type=F32)
        pieces.append(local + offs)
        offs = offs + jnp.sum(blk, axis=1, keepdims=True)
    return jnp.concatenate(pieces, axis=1)


META_TSTART, META_TEND, META_PSTART, META_PEND, META_W = 0, 4, 8, 12, 16
GATHER_SLOT_BLOCK = 128
GATHER_WIN = 1408
GATHER_ALIGN = 128
COMBINE_TILE = 1024
COMBINE_WIN = 192
COMBINE_ALIGN = 16


def _select_kernel(lg_ref, slot_ref, aff_ref, meta_ref, *, cap):
    z = lg_ref[0]
    z = z - jnp.max(z, axis=0, keepdims=True)
    ez = jnp.exp(z)
    a = ez / jnp.sum(ez, axis=0, keepdims=True)
    aff_ref[0] = a
    capf = float(cap)

    def count_ge(th):
        return jnp.sum(jnp.where(a >= th, 1.0, 0.0), axis=1, keepdims=True)

    def bisect(_, lohi):
        lo, hi = lohi
        q2 = (lo + hi) * 0.5
        q1 = (lo + q2) * 0.5
        q3 = (q2 + hi) * 0.5
        g1, g2, g3 = count_ge(q1) >= capf, count_ge(q2) >= capf, count_ge(q3) >= capf
        new_lo = jnp.where(g3, q3, jnp.where(g2, q2, jnp.where(g1, q1, lo)))
        new_hi = jnp.where(g3, hi, jnp.where(g2, q3, jnp.where(g1, q2, q1)))
        return new_lo, new_hi

    e = a.shape[0]
    lo, hi = lax.fori_loop(0, SELECT_ITERS, bisect,
                           (jnp.zeros((e, 1), F32), jnp.full((e, 1), 2.0, F32)))
    r_i = lax.broadcasted_iota(I32, (LANE, LANE), 0)
    c_i = lax.broadcasted_iota(I32, (LANE, LANE), 1)
    tri = jnp.where(r_i < c_i, 1.0, 0.0).astype(BF16)
    above = a >= hi
    n_above = jnp.sum(jnp.where(above, 1.0, 0.0), axis=1, keepdims=True)
    tied = (a >= lo) & jnp.logical_not(above)
    tie_rank = _exclusive_prefix(tied, tri)
    sel = above | (tied & (tie_rank < capf - n_above))
    pos = _exclusive_prefix(sel, tri)
    slot_ref[0] = jnp.where(sel, pos, -1.0).astype(I32)

    seq = a.shape[1]
    t = lax.broadcasted_iota(I32, (e, seq), 1).astype(F32)
    lane = lax.broadcasted_iota(I32, (e, LANE), 1)
    meta = jnp.zeros((e, LANE), F32)
    sb = min(GATHER_SLOT_BLOCK, cap)
    for s in range(cap // sb):
        first = jnp.min(jnp.where(sel & (pos >= float(s * sb)), t, float(seq)), axis=1, keepdims=True)
        last = jnp.max(jnp.where(sel & (pos < float((s + 1) * sb)), t, -1.0), axis=1, keepdims=True)
        meta = jnp.where(lane == META_TSTART + s, first, meta)
        meta = jnp.where(lane == META_TEND + s, last, meta)
    tt = min(COMBINE_TILE, seq)
    for i in range(seq // tt):
        before = jnp.sum(jnp.where(sel & (t < float(i * tt)), 1.0, 0.0), axis=1, keepdims=True)
        upto = jnp.sum(jnp.where(sel & (t < float((i + 1) * tt)), 1.0, 0.0), axis=1, keepdims=True)
        meta = jnp.where(lane == META_PSTART + i, before, meta)
        meta = jnp.where(lane == META_PEND + i, upto, meta)
    meta_ref[0] = meta.astype(I32)


def _select(logits_t, cap):
    b, e, seq = logits_t.shape
    assert cap // min(GATHER_SLOT_BLOCK, cap) <= 4 and seq // min(COMBINE_TILE, seq) <= 4
    spec = pl.BlockSpec((1, e, seq), lambda bi: (bi, 0, 0))
    mspec = pl.BlockSpec((1, e, LANE), lambda bi: (bi, 0, 0))
    return pl.pallas_call(
        functools.partial(_select_kernel, cap=cap),
        grid=(b,),
        in_specs=[spec],
        out_specs=[spec, spec, mspec],
        out_shape=[jax.ShapeDtypeStruct((b, e, seq), I32), jax.ShapeDtypeStruct((b, e, seq), F32),
                   jax.ShapeDtypeStruct((b, e, LANE), I32)],
        compiler_params=_cparams(("parallel",), 32),
        name="select",
    )(logits_t)


def _align_down(v, align):
    shift = align.bit_length() - 1
    return lax.shift_left(lax.shift_right_logical(v, shift), shift)


def _gather_kernel(meta_ref, h_ref, slot_ref, aff_ref, xg_ref, gate_ref, *, cap, sb, win):
    seq = h_ref.shape[1]
    base = (pl.program_id(0) * N_EXPERTS + pl.program_id(1)) * META_W

    n_blocks = cap // sb

    def gather_blocks(starts, width):
        hits = []
        for s in range(n_blocks):
            j = lax.broadcasted_iota(I32, (sb, width), 0) + s * sb
            hits.append(slot_ref[0, 0, :, pl.ds(starts[s], width)] == j)
        for s in range(n_blocks):
            tok = pl.ds(starts[s], width)
            rows = slice(s * sb, (s + 1) * sb)
            onehot = jnp.where(hits[s], 1.0, 0.0).astype(BF16)
            xg_ref[0, 0, rows, :] = jnp.dot(onehot, h_ref[0, tok, :], preferred_element_type=F32).astype(BF16)
            gate_ref[0, 0, rows, :] = jnp.sum(jnp.where(hits[s], aff_ref[0, 0, :, tok], 0.0), axis=1, keepdims=True)

    if win >= seq:
        gather_blocks([0] * n_blocks, seq)
        return
    starts = []
    fits = None
    for s in range(n_blocks):
        first = meta_ref[base + META_TSTART + s]
        last = meta_ref[base + META_TEND + s]
        start = pl.multiple_of(jnp.minimum(_align_down(first, GATHER_ALIGN), seq - win), GATHER_ALIGN)
        starts.append(start)
        ok = last < start + win
        fits = ok if fits is None else jnp.logical_and(fits, ok)

    @pl.when(fits)
    def _():
        gather_blocks(starts, win)

    @pl.when(jnp.logical_not(fits))
    def _():
        gather_blocks([0] * n_blocks, seq)


def _gather(meta, h2, slots4, aff4, cap):
    b, seq, _ = h2.shape
    e = slots4.shape[1]
    sb = min(GATHER_SLOT_BLOCK, cap)
    win = GATHER_WIN if seq > GATHER_WIN else seq
    assert (seq - win) % GATHER_ALIGN == 0
    row = pl.BlockSpec((1, 1, 1, seq), lambda bi, ei, m: (bi, ei, 0, 0))
    return pl.pallas_call(
        functools.partial(_gather_kernel, cap=cap, sb=sb, win=win),
        grid_spec=pltpu.PrefetchScalarGridSpec(
            num_scalar_prefetch=1,
            grid=(b, e),
            in_specs=[pl.BlockSpec((1, seq, D_MODEL), lambda bi, ei, m: (bi, 0, 0)), row, row],
            out_specs=[
                pl.BlockSpec((1, 1, cap, D_MODEL), lambda bi, ei, m: (ei, bi, 0, 0)),
                pl.BlockSpec((1, 1, cap, 1), lambda bi, ei, m: (ei, bi, 0, 0)),
            ],
        ),
        out_shape=[
            jax.ShapeDtypeStruct((e, b, cap, D_MODEL), BF16),
            jax.ShapeDtypeStruct((e, b, cap, 1), F32),
        ],
        compiler_params=_cparams(("parallel", "arbitrary"), 48),
        name="gather",
    )(meta, h2, slots4, aff4)


FF_CHUNK = 256
FFN_ROW_TILES = 2


def _ffn_kernel(*refs, with_ctx):
    if with_ctx:
        x_ref, gate_ref, xc_ref, gatec_ref, wg_ref, wu_ref, wd_ref, o_ref, oc_ref, acc_ref, accc_ref = refs
    else:
        x_ref, gate_ref, wg_ref, wu_ref, wd_ref, o_ref, acc_ref = refs
    fc = pl.program_id(2)
    last_fc = pl.num_programs(2) - 1

    def swiglu_chunk(streams):
        @pl.when(fc == 0)
        def _():
            for _, _, _, acc in streams:
                acc[...] = jnp.zeros_like(acc)

        wg = wg_ref[0, 0].astype(BF16)
        wu = wu_ref[0, 0].astype(BF16)
        wd = wd_ref[0, 0].astype(BF16)
        xs = [x[0] for x, _, _, _ in streams]
        gates_ = [jnp.dot(x, wg, preferred_element_type=F32) for x in xs]
        ups = [jnp.dot(x, wu, preferred_element_type=F32) for x in xs]
        hmids = [(a * jax.nn.sigmoid(a) * u).astype(BF16) for a, u in zip(gates_, ups)]
        for (_, _, _, acc), hmid in zip(streams, hmids):
            acc[...] += jnp.dot(hmid, wd, preferred_element_type=F32)

        @pl.when(fc == last_fc)
        def _():
            for _, gate, o, acc in streams:
                o[0] = (acc[...] * gate[0]).astype(BF16)

    main = (x_ref, gate_ref, o_ref, acc_ref)
    if not with_ctx:
        swiglu_chunk([main])
        return
    on_last_tile = pl.program_id(1) == pl.num_programs(1) - 1

    @pl.when(jnp.logical_not(on_last_tile))
    def _():
        swiglu_chunk([main])

    @pl.when(on_last_tile)
    def _():
        swiglu_chunk([main, (xc_ref, gatec_ref, oc_ref, accc_ref)])


def _ffn(xg3, gate3, xc3, gatec3, wg_all, wu_all, wd_all, layer):
    e, m, _ = xg3.shape
    tm = m // FFN_ROW_TILES
    assert m % tm == 0 and tm % 16 == 0
    with_ctx = xc3 is not None
    row = lambda w: pl.BlockSpec((1, tm, w), lambda ei, i, fc: (ei, i, 0))
    in_specs = [row(D_MODEL), row(1)]
    out_specs = [row(D_MODEL)]
    out_shape = [jax.ShapeDtypeStruct((e, m, D_MODEL), BF16)]
    scratch = [pltpu.VMEM((tm, D_MODEL), F32)]
    args = [xg3, gate3]
    if with_ctx:
        mc = xc3.shape[1]
        crow = lambda w: pl.BlockSpec((1, mc, w), lambda ei, i, fc: (ei, 0, 0))
        in_specs += [crow(D_MODEL), crow(1)]
        out_specs.append(crow(D_MODEL))
        out_shape.append(jax.ShapeDtypeStruct((e, mc, D_MODEL), BF16))
        scratch.append(pltpu.VMEM((mc, D_MODEL), F32))
        args += [xc3, gatec3]
    in_specs += [
        pl.BlockSpec((1, 1, D_MODEL, FF_CHUNK), lambda ei, i, fc: (layer, ei, 0, fc)),
        pl.BlockSpec((1, 1, D_MODEL, FF_CHUNK), lambda ei, i, fc: (layer, ei, 0, fc)),
        pl.BlockSpec((1, 1, FF_CHUNK, D_MODEL), lambda ei, i, fc: (layer, ei, fc, 0)),
    ]
    res = pl.pallas_call(
        functools.partial(_ffn_kernel, with_ctx=with_ctx),
        grid=(e, m // tm, EXPERT_FF // FF_CHUNK),
        in_specs=in_specs,
        out_specs=out_specs,
        out_shape=out_shape,
        scratch_shapes=scratch,
        compiler_params=_cparams(("parallel", "arbitrary", "arbitrary"), 56),
        name="ffn",
    )(*args, wg_all, wu_all, wd_all)
    return (res[0], res[1]) if with_ctx else (res[0], None)


COMBINE_EXPERTS = 4


def _combine_kernel(meta_ref, slot_ref, gy_ref, x_ref, g2_ref, gf_ref, o_ref, acc_ref, *, cap, win, final):
    bi = pl.program_id(0)
    ti = pl.program_id(1)
    ec = pl.program_id(2)
    tt = x_ref.shape[1]
    tn = (((0,), (0,)), ((), ()))

    @pl.when(ec == 0)
    def _():
        acc_ref[...] = jnp.zeros_like(acc_ref)

    def scatter(starts, width):
        j = lax.broadcasted_iota(I32, (width, tt), 0)
        onehot = jnp.concatenate(
            [jnp.where(slot_ref[0, k] == j + starts[k], 1.0, 0.0).astype(BF16) for k in range(COMBINE_EXPERTS)],
            axis=0)
        gy = jnp.concatenate([gy_ref[k, pl.ds(starts[k], width), :] for k in range(COMBINE_EXPERTS)], axis=0)
        acc_ref[...] += lax.dot_general(onehot, gy, tn, preferred_element_type=F32)

    if win >= cap:
        scatter([0] * COMBINE_EXPERTS, cap)
    else:
        starts = []
        fits = None
        for k in range(COMBINE_EXPERTS):
            base = (bi * N_EXPERTS + ec * COMBINE_EXPERTS + k) * META_W
            before = meta_ref[base + META_PSTART + ti]
            upto = meta_ref[base + META_PEND + ti]
            start = pl.multiple_of(jnp.minimum(_align_down(before, COMBINE_ALIGN), cap - win), COMBINE_ALIGN)
            starts.append(start)
            ok = upto <= start + win
            fits = ok if fits is None else jnp.logical_and(fits, ok)

        @pl.when(fits)
        def _():
            scatter(starts, win)

        @pl.when(jnp.logical_not(fits))
        def _():
            scatter([0] * COMBINE_EXPERTS, cap)

    @pl.when(ec == pl.num_programs(2) - 1)
    def _():
        xn = x_ref[0] + g2_ref[0] * acc_ref[...]
        if final:
            ms = jnp.mean(xn * xn, axis=-1, keepdims=True)
            xn = (xn * lax.rsqrt(ms + RMS_EPS)) * gf_ref[...]
        o_ref[0] = xn


def _combine(meta, slots4, gy3, row_off, x3, mod3, gfinal, *, cap, ctx_row, final):
    b, seq, _ = x3.shape
    e = slots4.shape[1]
    tt = min(seq, COMBINE_TILE)
    win = COMBINE_WIN if cap > COMBINE_WIN else cap
    assert row_off % cap == 0 and (cap - win) % COMBINE_ALIGN == 0
    blk_off = row_off // cap
    mrow = (lambda bi: bi) if ctx_row is None else (lambda bi: ctx_row)
    return pl.pallas_call(
        functools.partial(_combine_kernel, cap=cap, win=win, final=final),
        grid_spec=pltpu.PrefetchScalarGridSpec(
            num_scalar_prefetch=1,
            grid=(b, seq // tt, e // COMBINE_EXPERTS),
            in_specs=[
                pl.BlockSpec((1, COMBINE_EXPERTS, 1, tt), lambda bi, i, ec, m: (bi, ec, 0, i)),
                pl.BlockSpec((COMBINE_EXPERTS, cap, D_MODEL), lambda bi, i, ec, m: (ec, blk_off + bi, 0)),
                pl.BlockSpec((1, tt, D_MODEL), lambda bi, i, ec, m: (bi, i, 0)),
                pl.BlockSpec((1, 1, D_MODEL), lambda bi, i, ec, m: (mrow(bi), 0, 5)),
                pl.BlockSpec((1, D_MODEL), lambda bi, i, ec, m: (0, 0)),
            ],
            out_specs=pl.BlockSpec((1, tt, D_MODEL), lambda bi, i, ec, m: (bi, i, 0)),
            scratch_shapes=[pltpu.VMEM((tt, D_MODEL), F32)],
        ),
        out_shape=jax.ShapeDtypeStruct((b, seq, D_MODEL), F32),
        compiler_params=_cparams(("parallel", "parallel", "arbitrary"), 48),
        name="combine",
    )(meta, slots4, gy3, x3, mod3, gfinal)


def _moe_route(h2, logits_t):
    b, seq, _ = h2.shape
    cap = EC_CAPACITY_FACTOR * seq // N_EXPERTS
    slots, aff, meta = _select(logits_t, cap)
    meta = meta[:, :, :META_W].reshape(-1)
    slots4 = slots.reshape(b, N_EXPERTS, 1, seq)
    aff4 = aff.reshape(b, N_EXPERTS, 1, seq)
    xg, gate = _gather(meta, h2, slots4, aff4, cap)
    return meta, slots4, xg.reshape(N_EXPERTS, b * cap, D_MODEL), gate.reshape(N_EXPERTS, b * cap, 1), cap


def kernel(x, c, ctx, c_ctx, ada_w, ada_b, norm1_g, norm2_g, w_in, rpb, w_att_o, w_fourier, w_pool,
           pool_scale, w_out, w_router, w_exp_gate, w_exp_up, w_exp_down, final_norm_g):
    b, seq, d = x.shape
    lc = ctx.shape[1]
    assert d == D_MODEL and seq % (GRID_W * ATT_QROWS) == 0 and b + 1 <= MOD_ROWS
    rows = seq // GRID_W
    ctx_row = b

    cond = jnp.concatenate([c, c_ctx[None, :], jnp.zeros((MOD_ROWS - b - 1, d), F32)], axis=0)
    cl, sl = _dft_mats(seq, seq // 2)
    clc, slc = _dft_mats(lc, lc // 2)
    cc, sc = _dft_mats(FOURIER_GROUP_W, FOURIER_GROUP_W)
    gfinal = final_norm_g.reshape(1, d)

    for i in range(DEPTH):
        update_ctx = i < DEPTH - 1
        mod3 = _adaln(cond, ada_w[i], ada_b[i]).reshape(MOD_ROWS, 1, 6 * d)
        g1n = norm1_g[i].reshape(1, d)
        g2n = norm2_g[i].reshape(1, d)
        w_in_b = w_in[i].astype(BF16)
        wao = w_att_o[i].astype(BF16)
        wf = w_fourier[i].astype(BF16)
        wp = w_pool[i].astype(BF16)
        ps = pool_scale[i].reshape(1, d)
        wo = w_out[i].astype(BF16)
        wr_t = jnp.pad(jnp.concatenate([w_router[i], w_router[i]], axis=1), ((0, 0), (0, LANE - 2 * N_EXPERTS)))
        bias = _bias_table(rpb[i], rows)

        pb, pp = _modproj(x.reshape(b * seq, d), mod3, g1n, w_in_b, seq=seq, ctx_row=None,
                          n_out=IN_W, with_pool=True)
        n_ctx = IN_W if update_ctx else Q_OFF
        pcb, pcp = _modproj(ctx.reshape(b * lc, d), mod3, g1n, w_in_b[:, :n_ctx], seq=lc, ctx_row=ctx_row,
                            n_out=n_ctx, with_pool=update_ctx)
        pb3 = pb.reshape(b, seq, IN_W)
        pcb3 = pcb.reshape(b, lc, n_ctx)

        att = _nattn(pb3, pcb3, bias)
        four = _fourier(pb3, cc, sc, cl, sl)
        pooled = _pool(pp.reshape(b, seq, POOL_W))
        x, h2, logits_t = _merge(att, four, pooled, pb3, x, mod3, wao, wf, wp, ps, wo, g2n, wr_t, ctx_row=None)
        meta_x, slots_x, xg, gate, cap = _moe_route(h2, logits_t)

        if update_ctx:
            att_c = _cattn(pcb3)
            four_c = _fourier(pcb3, cc, sc, clc, slc)
            pooled_c = _pool(pcp.reshape(b, lc, POOL_W))
            ctx, h2_c, logits_c = _merge(att_c, four_c, pooled_c, pcb3, ctx, mod3, wao, wf, wp, ps, wo, g2n, wr_t,
                                         ctx_row=ctx_row)
            meta_c, slots_c, xg_c, gate_c, cap_c = _moe_route(h2_c, logits_c)
        else:
            xg_c = gate_c = None

        gy, gy_c = _ffn(xg, gate, xg_c, gate_c, w_exp_gate, w_exp_up, w_exp_down, i)
        if update_ctx:
            ctx = _combine(meta_c, slots_c, gy_c, 0, ctx, mod3, gfinal, cap=cap_c, ctx_row=ctx_row, final=False)
        x = _combine(meta_x, slots_x, gy, 0, x, mod3, gfinal, cap=cap, ctx_row=None, final=not update_ctx)
    return x
```

```python
import functools
import math

import jax
import jax.numpy as jnp
from jax import lax
from jax.experimental import pallas as pl
from jax.experimental.pallas import tpu as pltpu

F32 = jnp.float32
BF16 = jnp.bfloat16
I32 = jnp.int32
HIGHEST = lax.Precision.HIGHEST

D_MODEL = 1024
DEPTH = 2
GRID_W = 64
N_HEADS = 16
HEAD_DIM = 64
WIN_R = 8
WIN_C = 16
FOURIER_GROUPS = 4
FOURIER_GROUP_W = 128
FOURIER_W = 512
POOL_WINDOWS = (2, 4, 8, 16)
POOL_GROUP_W = 128
POOL_W = 512
POOL_OUT_GROUP = 256
K_OFF, V_OFF, Q_OFF, F_OFF, P_OFF, G_OFF = 0, 1024, 2048, 3072, 3584, 4096
IN_W = 7168
N_EXPERTS = 16
EC_CAPACITY_FACTOR = 2
EXPERT_FF = 2816
RMS_EPS = 1e-6

LANE = 128
HEAD_PAIR_W = 2 * HEAD_DIM
N_HEAD_PAIRS = N_HEADS // 2
MOD_ROWS = 16
NEG_BIG = -1e30
ATT_QROWS = 2
ATT_KROWS = 10
ATT_TOP = (WIN_R // 2 + ATT_QROWS - 1) // ATT_QROWS
ATT_BOT = (WIN_R // 2 - 1 + ATT_QROWS - 1) // ATT_QROWS
ATT_CLASSES = ATT_TOP + 1 + ATT_BOT
ATT_ALIGN = 64
SELECT_ITERS = 32
MiB = 1024 * 1024


def _cparams(sem, vmem_mib):
    return pltpu.CompilerParams(dimension_semantics=sem, vmem_limit_bytes=vmem_mib * MiB)


def _adaln_kernel(c_ref, w_ref, b_ref, o_ref):
    c = c_ref[...]
    s = c * jax.nn.sigmoid(c)
    o_ref[...] = jnp.dot(s, w_ref[...], precision=HIGHEST, preferred_element_type=F32) + b_ref[...]


def _adaln(cond_rows, ada_w, ada_b):
    n = ada_w.shape[1]
    tn = 1024
    return pl.pallas_call(
        _adaln_kernel,
        grid=(n // tn,),
        in_specs=[
            pl.BlockSpec((MOD_ROWS, D_MODEL), lambda j: (0, 0)),
            pl.BlockSpec((D_MODEL, tn), lambda j: (0, j)),
            pl.BlockSpec((1, tn), lambda j: (0, j)),
        ],
        out_specs=pl.BlockSpec((MOD_ROWS, tn), lambda j: (0, j)),
        out_shape=jax.ShapeDtypeStruct((MOD_ROWS, n), F32),
        compiler_params=_cparams(("arbitrary",), 32),
        name="adaln",
    )(cond_rows, ada_w, ada_b.reshape(1, n))


def _modulate(x, g, shift, scale):
    ms = jnp.mean(x * x, axis=-1, keepdims=True)
    y = x * lax.rsqrt(ms + RMS_EPS)
    return (y * g) * (1.0 + scale) + shift


MODPROJ_TN = 1792


def _modproj_kernel(x_ref, sh_ref, sc_ref, g_ref, w_ref, *rest, pool_tile):
    if pool_tile is None:
        o_ref, h_ref = rest
        pp_ref = None
    else:
        o_ref, pp_ref, h_ref = rest
    j = pl.program_id(1)

    @pl.when(j == 0)
    def _():
        h = _modulate(x_ref[...], g_ref[...], sh_ref[0], sc_ref[0])
        h_ref[...] = h.astype(BF16)

    acc = jnp.dot(h_ref[...], w_ref[0].astype(BF16), preferred_element_type=F32)
    o_ref[...] = acc.astype(BF16)
    if pool_tile is not None:
        @pl.when(j == pool_tile)
        def _():
            lo = P_OFF % acc.shape[1]
            pp_ref[...] = acc[:, lo:lo + POOL_W]


def _modproj(x2, mod3, gain, w_all, layer, *, seq, ctx_row, n_out, with_pool):
    rows = x2.shape[0]
    tn = MODPROJ_TN if n_out % MODPROJ_TN == 0 else 1024
    if ctx_row is None:
        tm = min(seq, 1024)
        tiles_per_seq = seq // tm
        mrow = lambda i: i // tiles_per_seq
    else:
        tm = min(rows, 1024)
        mrow = lambda i: ctx_row
    assert rows % tm == 0 and n_out % tn == 0 and (not with_pool or P_OFF % tn + POOL_W <= tn)
    pool_tile = (P_OFF // tn) if with_pool else None
    out_shape = [jax.ShapeDtypeStruct((rows, n_out), BF16)]
    out_specs = [pl.BlockSpec((tm, tn), lambda i, j: (i, j))]
    if with_pool:
        out_shape.append(jax.ShapeDtypeStruct((rows, POOL_W), F32))
        out_specs.append(pl.BlockSpec((tm, POOL_W), lambda i, j: (i, 0)))
    res = pl.pallas_call(
        functools.partial(_modproj_kernel, pool_tile=pool_tile),
        grid=(rows // tm, n_out // tn),
        in_specs=[
            pl.BlockSpec((tm, D_MODEL), lambda i, j: (i, 0)),
            pl.BlockSpec((1, 1, D_MODEL), lambda i, j: (mrow(i), 0, 0)),
            pl.BlockSpec((1, 1, D_MODEL), lambda i, j: (mrow(i), 0, 1)),
            pl.BlockSpec((1, D_MODEL), lambda i, j: (0, 0)),
            pl.BlockSpec((1, D_MODEL, tn), lambda i, j: (layer, 0, j)),
        ],
        out_specs=out_specs,
        out_shape=out_shape,
        scratch_shapes=[pltpu.VMEM((tm, D_MODEL), BF16)],
        compiler_params=_cparams(("parallel", "arbitrary"), 56),
        name="modproj",
    )(x2, mod3, mod3, gain, w_all)
    return res if with_pool else (res[0], None)


def _bias_kernel(rpb_ref, o_ref, *, rows):
    h = pl.program_id(0)
    cls = pl.program_id(1)
    i_rep = jnp.where(cls <= ATT_TOP, cls, cls - (ATT_TOP + 1) + (rows // ATT_QROWS - ATT_BOT))
    s = jnp.clip(ATT_QROWS * i_rep - WIN_R // 2, 0, rows - ATT_KROWS)
    qc = lax.broadcasted_iota(I32, (GRID_W, LANE), 0)
    lane = lax.broadcasted_iota(I32, (GRID_W, LANE), 1)
    kc = lane & (GRID_W - 1)
    first_half = lane < GRID_W
    cs = jnp.clip(qc - WIN_C // 2, 0, GRID_W - WIN_C)
    col_valid = (kc >= cs) & (kc < cs + WIN_C)
    dcol = kc - qc + (WIN_C - 1)
    n_coff = 2 * WIN_C - 1
    n_roff = 2 * WIN_R - 1
    for ri in range(ATT_QROWS):
        r = ATT_QROWS * i_rep + ri
        rs = jnp.clip(r - WIN_R // 2, 0, rows - WIN_R)
        for m in range(ATT_KROWS // 2):
            krow_a = s + 2 * m
            krow_b = krow_a + 1
            va = ((krow_a >= rs) & (krow_a < rs + WIN_R)).astype(I32)
            vb = ((krow_b >= rs) & (krow_b < rs + WIN_R)).astype(I32)
            base_a = (h * n_roff + jnp.clip(krow_a - r + WIN_R - 1, 0, n_roff - 1)) * n_coff
            base_b = (h * n_roff + jnp.clip(krow_b - r + WIN_R - 1, 0, n_roff - 1)) * n_coff
            acc = jnp.zeros((GRID_W, LANE), F32)
            for c in range(n_coff):
                val = jnp.where(first_half, rpb_ref[base_a + c], rpb_ref[base_b + c])
                acc = jnp.where(dcol == c, val, acc)
            row_valid = jnp.where(first_half, va, vb) > 0
            tile = jnp.where(col_valid & row_valid, acc, NEG_BIG)
            o_ref[0, 0, ri * GRID_W:(ri + 1) * GRID_W, m * LANE:(m + 1) * LANE] = tile


def _bias_table(rpb, rows):
    nq = ATT_QROWS * GRID_W
    nk = ATT_KROWS * GRID_W
    return pl.pallas_call(
        functools.partial(_bias_kernel, rows=rows),
        grid_spec=pltpu.PrefetchScalarGridSpec(
            num_scalar_prefetch=1,
            grid=(N_HEADS, ATT_CLASSES),
            in_specs=[],
            out_specs=pl.BlockSpec((1, 1, nq, nk), lambda h, c, rpb: (h, c, 0, 0)),
        ),
        out_shape=jax.ShapeDtypeStruct((N_HEADS, ATT_CLASSES, nq, nk), F32),
        compiler_params=_cparams(("arbitrary", "arbitrary"), 32),
        name="bias_table",
    )(rpb.reshape(-1))


_NT = (((1,), (1,)), ((), ()))


def _scores(qm, kw, kc, bias):
    sw = lax.dot_general(qm, kw, _NT, preferred_element_type=F32)
    if bias is not None:
        sw = sw + bias
    sc = lax.dot_general(qm, kc, _NT, preferred_element_type=F32) if kc is not None else None
    return sw, sc


def _probs(sw, sc):
    m = jnp.max(sw, axis=-1, keepdims=True)
    if sc is not None:
        m = jnp.maximum(m, jnp.max(sc, axis=-1, keepdims=True))
    pw = jnp.exp(sw - m)
    l = jnp.sum(pw, axis=-1, keepdims=True)
    pc = None
    if sc is not None:
        pc = jnp.exp(sc - m)
        l = l + jnp.sum(pc, axis=-1, keepdims=True)
        pc = pc.astype(BF16)
    return pw.astype(BF16), pc, l


def _pv(pw, pc, l, vw, vc):
    o = jnp.dot(pw, vw, preferred_element_type=F32)
    if pc is not None:
        o = o + jnp.dot(pc, vc, preferred_element_type=F32)
    return o / l


ATT_UNROLL = 4


ATT_KEY_TILE = 256
ATT_SAMPLES = 1


def _key_tiles(n):
    return [(off, min(ATT_KEY_TILE, n - off)) for off in range(0, n, ATT_KEY_TILE)]


def _lane_fold(x, op):
    out = x[:, :LANE]
    for c in range(1, x.shape[1] // LANE):
        out = op(out, x[:, c * LANE:(c + 1) * LANE])
    return out


def _nattn_kernel(k_ref, v_ref, q_ref, kc_ref, vc_ref, b_ref, o_ref, s_ref, m_ref, *, rows):
    nq = ATT_QROWS * GRID_W
    nk = ATT_KROWS * GRID_W
    lc = kc_ref.shape[1]
    n_steps = rows // ATT_QROWS
    lane = lax.broadcasted_iota(I32, (nq, HEAD_PAIR_W), 1)
    second = lane >= HEAD_DIM
    scale = HEAD_DIM ** -0.5
    win_tiles = _key_tiles(nk)
    ctx_tiles = _key_tiles(lc)

    ones_cols = jnp.ones((ATT_KEY_TILE, HEAD_PAIR_W), BF16)

    def with_ones(v):
        return jnp.concatenate([v, ones_cols[:v.shape[0]]], axis=1)

    nb = q_ref.shape[0]
    n_iter = n_steps // ATT_UNROLL

    def geometry(g, u):
        if isinstance(g, int):
            bs, ii = divmod(g, n_iter)
        else:
            shift = n_iter.bit_length() - 1
            assert 1 << shift == n_iter
            bs, ii = lax.shift_right_logical(g, shift), g & (n_iter - 1)
        i = ii * ATT_UNROLL + u
        s = jnp.clip(ATT_QROWS * i - WIN_R // 2, 0, rows - ATT_KROWS)
        kstart = pl.multiple_of(s * GRID_W, ATT_ALIGN)
        qstart = pl.multiple_of(i * nq, ATT_ALIGN)
        cls = jnp.where(i < ATT_TOP, i,
                        jnp.where(i < n_steps - ATT_BOT, ATT_TOP, i - (n_steps - ATT_BOT) + ATT_TOP + 1))
        return bs, kstart, qstart, cls

    def pass1(ii, buf, u, hh):
        bs, kstart, qstart, cls = geometry(ii, u)
        c = 2 * u + hh
        q2 = (q_ref[bs, pl.ds(qstart, nq), :].astype(F32) * scale).astype(BF16)
        head_lanes = second if hh else jnp.logical_not(second)
        qm = jnp.where(head_lanes, q2, jnp.zeros_like(q2))
        m_run = None
        for off, width in win_tiles:
            kt = k_ref[bs, pl.ds(kstart + off, width), :]
            st = lax.dot_general(qm, kt, _NT, preferred_element_type=F32) + b_ref[hh, cls, :, off:off + width]
            s_ref[buf, c, :, off:off + width] = st
            mt = _lane_fold(st, jnp.maximum)
            m_run = mt if m_run is None else jnp.maximum(m_run, mt)
        for off, width in ctx_tiles:
            st = lax.dot_general(qm, kc_ref[bs, off:off + width, :], _NT, preferred_element_type=F32)
            s_ref[buf, c, :, nk + off:nk + off + width] = st
            m_run = jnp.maximum(m_run, _lane_fold(st, jnp.maximum))
        m_ref[buf, c] = jnp.max(m_run, axis=-1, keepdims=True)

    def pass2(ii, buf, u, hh):
        bs, kstart, _, _ = geometry(ii, u)
        c = 2 * u + hh
        m = m_ref[buf, c]
        acc = None
        for off, width in win_tiles:
            pt = jnp.exp(s_ref[buf, c, :, off:off + width] - m).astype(BF16)
            pv = jnp.dot(pt, with_ones(v_ref[bs, pl.ds(kstart + off, width), :]), preferred_element_type=F32)
            acc = pv if acc is None else acc + pv
        for off, width in ctx_tiles:
            pt = jnp.exp(s_ref[buf, c, :, nk + off:nk + off + width] - m).astype(BF16)
            acc = acc + jnp.dot(pt, with_ones(vc_ref[bs, off:off + width, :]), preferred_element_type=F32)
        return acc[:, :HEAD_PAIR_W] / acc[:, HEAD_PAIR_W:]

    def store(ii, u, o0, o1):
        bs, _, qstart, _ = geometry(ii, u)
        o_ref[bs, pl.ds(qstart, nq), :] = jnp.where(second, o1, o0).astype(BF16)

    n_groups = nb * n_iter
    for u in range(ATT_UNROLL):
        for hh in range(2):
            pass1(0, 0, u, hh)

    def overlapped(ii, new):
        for u in range(ATT_UNROLL):
            outs = []
            for hh in range(2):
                pass1(ii, new, u, hh)
                outs.append(pass2(ii - 1, 1 - new, u, hh))
            store(ii - 1, u, *outs)

    def body(jj, carry):
        overlapped(2 * jj + 1, 1)
        overlapped(2 * jj + 2, 0)
        return carry

    assert n_groups % 2 == 0
    lax.fori_loop(0, n_groups // 2 - 1, body, 0)
    last = n_groups - 1
    overlapped(last, 1)
    for u in range(ATT_UNROLL):
        store(last, u, *[pass2(last, 1, u, hh) for hh in range(2)])


def _nattn(pb3, pcb3, bias):
    b, seq, _ = pb3.shape
    lc = pcb3.shape[1]
    rows = seq // GRID_W
    nq = ATT_QROWS * GRID_W
    nk = ATT_KROWS * GRID_W
    kblk, vblk, qblk = K_OFF // LANE, V_OFF // LANE, Q_OFF // LANE
    nb = ATT_SAMPLES if b % ATT_SAMPLES == 0 else 1
    return pl.pallas_call(
        functools.partial(_nattn_kernel, rows=rows),
        grid=(N_HEAD_PAIRS, b // nb),
        in_specs=[
            pl.BlockSpec((nb, seq, HEAD_PAIR_W), lambda hp, bi: (bi, 0, kblk + hp)),
            pl.BlockSpec((nb, seq, HEAD_PAIR_W), lambda hp, bi: (bi, 0, vblk + hp)),
            pl.BlockSpec((nb, seq, HEAD_PAIR_W), lambda hp, bi: (bi, 0, qblk + hp)),
            pl.BlockSpec((nb, lc, HEAD_PAIR_W), lambda hp, bi: (bi, 0, kblk + hp)),
            pl.BlockSpec((nb, lc, HEAD_PAIR_W), lambda hp, bi: (bi, 0, vblk + hp)),
            pl.BlockSpec((2, ATT_CLASSES, nq, nk), lambda hp, bi: (hp, 0, 0, 0)),
        ],
        out_specs=pl.BlockSpec((nb, seq, HEAD_PAIR_W), lambda hp, bi: (bi, 0, hp)),
        out_shape=jax.ShapeDtypeStruct((b, seq, N_HEADS * HEAD_DIM), BF16),
        scratch_shapes=[pltpu.VMEM((2, 2 * ATT_UNROLL, nq, nk + lc), F32),
                        pltpu.VMEM((2, 2 * ATT_UNROLL, nq, 1), F32)],
        compiler_params=_cparams(("parallel", "parallel"), 40),
        name="nattn",
    )(pb3, pb3, pb3, pcb3, pcb3, bias)


def _cattn_kernel(k_ref, v_ref, q_ref, o_ref):
    lc = q_ref.shape[1]
    lane = lax.broadcasted_iota(I32, (lc, HEAD_PAIR_W), 1)
    second = lane >= HEAD_DIM
    q2 = (q_ref[0].astype(F32) * (HEAD_DIM ** -0.5)).astype(BF16)
    k = k_ref[0]
    v = v_ref[0]
    outs = []
    for hh in range(2):
        head_lanes = second if hh else jnp.logical_not(second)
        qm = jnp.where(head_lanes, q2, jnp.zeros_like(q2))
        pw, _, l = _probs(*_scores(qm, k, None, None))
        outs.append(_pv(pw, None, l, v, None))
    o_ref[0] = jnp.where(second, outs[1], outs[0]).astype(BF16)


def _cattn(pcb3):
    b, lc, _ = pcb3.shape
    kblk, vblk, qblk = K_OFF // LANE, V_OFF // LANE, Q_OFF // LANE
    return pl.pallas_call(
        _cattn_kernel,
        grid=(b, N_HEAD_PAIRS),
        in_specs=[
            pl.BlockSpec((1, lc, HEAD_PAIR_W), lambda bi, hp: (bi, 0, kblk + hp)),
            pl.BlockSpec((1, lc, HEAD_PAIR_W), lambda bi, hp: (bi, 0, vblk + hp)),
            pl.BlockSpec((1, lc, HEAD_PAIR_W), lambda bi, hp: (bi, 0, qblk + hp)),
        ],
        out_specs=pl.BlockSpec((1, lc, HEAD_PAIR_W), lambda bi, hp: (bi, 0, hp)),
        out_shape=jax.ShapeDtypeStruct((b, lc, N_HEADS * HEAD_DIM), BF16),
        compiler_params=_cparams(("parallel", "parallel"), 32),
        name="cattn",
    )(pcb3, pcb3, pcb3)


def _dft_kernel(c_ref, s_ref, *, n):
    tk, ncols = c_ref.shape
    k = pl.program_id(0) * tk + lax.broadcasted_iota(I32, (tk, LANE), 0)
    lane = lax.broadcasted_iota(I32, (tk, LANE), 1)
    w = 2.0 * math.pi / n
    ang_p = ((k * lane) & (n - 1)).astype(F32) * w
    cp, sp = jnp.cos(ang_p), jnp.sin(ang_p)
    ang_q = ((k * (lane * LANE)) & (n - 1)).astype(F32) * w
    cq, sq = jnp.cos(ang_q), jnp.sin(ang_q)
    for q in range(ncols // LANE):
        cols = slice(q * LANE, (q + 1) * LANE)
        cqq, sqq = cq[:, q:q + 1], sq[:, q:q + 1]
        c_ref[:, cols] = (cqq * cp - sqq * sp).astype(BF16)
        s_ref[:, cols] = (sqq * cp + cqq * sp).astype(BF16)


def _dft_mats(n, ncols):
    tk = min(n, 256)
    return pl.pallas_call(
        functools.partial(_dft_kernel, n=n),
        grid=(n // tk,),
        in_specs=[],
        out_specs=[pl.BlockSpec((tk, ncols), lambda i: (i, 0))] * 2,
        out_shape=[jax.ShapeDtypeStruct((n, ncols), BF16)] * 2,
        compiler_params=_cparams(("parallel",), 48),
        name=f"dft_mats_{n}",
    )()


REV_BLOCK = 128


def _fourier_kernel(u_ref, cc_ref, sc_ref, cl_ref, sl_ref, o_ref, us_ref, ud_ref, a_ref, b_ref, ah_ref, *, seq):
    half = seq // 2
    nblk = seq // REV_BLOCK
    tk = o_ref.shape[1]

    @pl.when(pl.program_id(1) == 0)
    def _():
        d_i = lax.broadcasted_iota(I32, (REV_BLOCK, REV_BLOCK), 0)
        s_i = lax.broadcasted_iota(I32, (REV_BLOCK, REV_BLOCK), 1)
        flip = jnp.where((d_i >= 1) & (s_i == REV_BLOCK - d_i), 1.0, 0.0).astype(BF16)
        row = lax.broadcasted_iota(I32, (REV_BLOCK, FOURIER_W), 0)
        for blk in range(nblk // 2):
            lo = u_ref[0, blk * REV_BLOCK:(blk + 1) * REV_BLOCK, :].astype(F32)
            src = u_ref[0, (nblk - 1 - blk) * REV_BLOCK:(nblk - blk) * REV_BLOCK, :]
            rev = jnp.dot(flip, src, preferred_element_type=F32)
            if blk > 0:
                head = u_ref[0, (nblk - blk) * REV_BLOCK:(nblk - blk) * REV_BLOCK + 16, :].astype(F32)
                rev = jnp.where(row == 0, head[0:1, :], rev)
            rows = slice(blk * REV_BLOCK, (blk + 1) * REV_BLOCK)
            us_ref[rows, :] = (lo + rev).astype(BF16)
            ud_ref[rows, :] = (lo - rev).astype(BF16)
        mid = u_ref[0, half:half + 16, :]
        for g in range(FOURIER_GROUPS):
            sl = slice(g * FOURIER_GROUP_W, (g + 1) * FOURIER_GROUP_W)
            a_ref[:, sl] = jnp.dot(us_ref[:, sl], cc_ref[...], preferred_element_type=F32).astype(BF16)
            b_ref[:, sl] = jnp.dot(ud_ref[:, sl], sc_ref[...], preferred_element_type=F32).astype(BF16)
            ah_ref[:, sl] = jnp.dot(mid[:, sl], cc_ref[...], preferred_element_type=F32)

    k = pl.program_id(1) * tk + lax.broadcasted_iota(I32, (tk, 1), 0)
    sign = (1 - 2 * (k & 1)).astype(F32)
    y = (jnp.dot(cl_ref[...], a_ref[...], preferred_element_type=F32)
         - jnp.dot(sl_ref[...], b_ref[...], preferred_element_type=F32)
         + sign * ah_ref[0:1, :])
    o_ref[0] = (y * (1.0 / math.sqrt(seq * FOURIER_GROUP_W))).astype(BF16)


def _fourier(pb3, cc, sc, cl, sl):
    b, seq, _ = pb3.shape
    tk = min(seq, 512)
    half = seq // 2
    assert seq % (2 * REV_BLOCK) == 0
    return pl.pallas_call(
        functools.partial(_fourier_kernel, seq=seq),
        grid=(b, seq // tk),
        in_specs=[
            pl.BlockSpec((1, seq, FOURIER_W), lambda bi, k: (bi, 0, F_OFF // FOURIER_W)),
            pl.BlockSpec((FOURIER_GROUP_W, FOURIER_GROUP_W), lambda bi, k: (0, 0)),
            pl.BlockSpec((FOURIER_GROUP_W, FOURIER_GROUP_W), lambda bi, k: (0, 0)),
            pl.BlockSpec((tk, half), lambda bi, k: (k, 0)),
            pl.BlockSpec((tk, half), lambda bi, k: (k, 0)),
        ],
        out_specs=pl.BlockSpec((1, tk, FOURIER_W), lambda bi, k: (bi, k, 0)),
        out_shape=jax.ShapeDtypeStruct((b, seq, FOURIER_W), BF16),
        scratch_shapes=[pltpu.VMEM((half, FOURIER_W), BF16)] * 4 + [pltpu.VMEM((16, FOURIER_W), F32)],
        compiler_params=_cparams(("parallel", "arbitrary"), 48),
        name="fourier",
    )(pb3, cc, sc, cl, sl)


POOL_PAD = 8


def _pool_kernel(u_ref, o_ref, pad_ref, *, seq):
    t = lax.broadcasted_iota(I32, (seq, POOL_GROUP_W), 0)
    zeros = jnp.zeros((POOL_PAD, POOL_GROUP_W), F32)
    pad_ref[0:POOL_PAD, :] = zeros
    pad_ref[seq + POOL_PAD:seq + 2 * POOL_PAD, :] = zeros
    pad_ref[POOL_PAD:seq + POOL_PAD, :] = u_ref[0]
    for g, w in enumerate(POOL_WINDOWS):
        @pl.when(pl.program_id(1) == g)
        def _(w=w):
            acc = None
            for d in range(-(w // 2), w - w // 2):
                term = pad_ref[pl.ds(POOL_PAD + d, seq), :]
                acc = term if acc is None else acc + term
            cnt = (jnp.minimum(t + (w - w // 2), seq) - jnp.maximum(t - w // 2, 0)).astype(F32)
            o_ref[0] = (acc / cnt - u_ref[0]).astype(BF16)


def _pool(pp3):
    b, seq, _ = pp3.shape
    spec = pl.BlockSpec((1, seq, POOL_GROUP_W), lambda bi, g: (bi, 0, g))
    return pl.pallas_call(
        functools.partial(_pool_kernel, seq=seq),
        grid=(b, len(POOL_WINDOWS)),
        in_specs=[spec],
        out_specs=spec,
        out_shape=jax.ShapeDtypeStruct((b, seq, POOL_W), BF16),
        scratch_shapes=[pltpu.VMEM((seq + 2 * POOL_PAD, POOL_GROUP_W), F32)],
        compiler_params=_cparams(("parallel", "parallel"), 32),
        name="pool",
    )(pp3)


MERGE_PARTS = 4


def _merge_kernel(att_ref, four_ref, pool_ref, ga_ref, gf_ref, gp_ref, x_ref, g1_ref,
                  wao_ref, wf_ref, wp_ref, ps_ref, wo_ref, sh2_ref, sc2_ref, g2n_ref, wr_ref,
                  o_ref, h_ref, lg_ref):
    tm = x_ref.shape[1]
    n_parts = min(MERGE_PARTS, tm // LANE)
    parts = [slice(p * (tm // n_parts), (p + 1) * (tm // n_parts)) for p in range(n_parts)]
    wr = wr_ref[...]
    wr_hi = wr.astype(BF16)
    wr_lo = (wr - wr_hi.astype(F32)).astype(BF16)
    wr_split = jnp.where(lax.broadcasted_iota(I32, wr.shape, 1) < N_EXPERTS, wr_hi, wr_lo)

    def sigmoid(ref, rows):
        return 0.5 * jnp.tanh(0.5 * ref[0, rows, :].astype(F32)) + 0.5

    branches = []
    for rows in parts:
        y_att = jnp.dot(att_ref[0, rows, :], wao_ref[...], preferred_element_type=F32)
        y_four = jnp.dot(four_ref[0, rows, :], wf_ref[...], preferred_element_type=F32)
        pooled = pool_ref[0, rows, :]
        y_pool = jnp.concatenate(
            [jnp.dot(pooled[:, g * POOL_GROUP_W:(g + 1) * POOL_GROUP_W], wp_ref[g], preferred_element_type=F32)
             for g in range(len(POOL_WINDOWS))], axis=-1) * ps_ref[...]
        branches.append((y_att, y_four, y_pool))
    resid = []
    for rows, (y_att, y_four, y_pool) in zip(parts, branches):
        merged = (sigmoid(ga_ref, rows) * y_att + sigmoid(gf_ref, rows) * y_four + sigmoid(gp_ref, rows) * y_pool)
        y = jnp.dot(merged.astype(BF16), wo_ref[...], preferred_element_type=F32)
        xn = x_ref[0, rows, :] + g1_ref[0] * y
        o_ref[0, rows, :] = xn
        resid.append(xn)
    for rows, xn in zip(parts, resid):
        n = xn.shape[0]
        h = _modulate(xn, g2n_ref[...], sh2_ref[0], sc2_ref[0])
        h_hi = h.astype(BF16)
        h_ref[0, rows, :] = h_hi
        h_lo = (h - h_hi.astype(F32)).astype(BF16)
        r = jnp.dot(jnp.concatenate([h_hi, h_lo], axis=0), wr_split, preferred_element_type=F32)
        hi_t = r[:n].T
        lo_t = r[n:].T
        lg_ref[0, :, rows] = (hi_t[:N_EXPERTS] + hi_t[N_EXPERTS:2 * N_EXPERTS]
                              + lo_t[:N_EXPERTS] + lo_t[N_EXPERTS:2 * N_EXPERTS])


def _merge(att, four, pooled, pb3, x3, mod3, wao, wf, wp, ps, wo, g2n, wr_t, *, ctx_row):
    b, seq, _ = x3.shape
    tm = min(seq, 512)
    gblk = G_OFF // D_MODEL
    mrow = (lambda bi: bi) if ctx_row is None else (lambda bi: ctx_row)
    tok = lambda w: pl.BlockSpec((1, tm, w), lambda bi, i: (bi, i, 0))
    gate = lambda k: pl.BlockSpec((1, tm, D_MODEL), lambda bi, i: (bi, i, gblk + k))
    full = lambda a: pl.BlockSpec(a.shape, lambda bi, i: (0,) * a.ndim)
    modc = lambda k: pl.BlockSpec((1, 1, D_MODEL), lambda bi, i: (mrow(bi), 0, k))
    return pl.pallas_call(
        _merge_kernel,
        grid=(b, seq // tm),
        in_specs=[
            tok(D_MODEL), tok(FOURIER_W), tok(POOL_W), gate(0), gate(1), gate(2), tok(D_MODEL), modc(2),
            full(wao), full(wf), full(wp), full(ps), full(wo), modc(3), modc(4), full(g2n), full(wr_t),
        ],
        out_specs=[tok(D_MODEL), tok(D_MODEL), pl.BlockSpec((1, N_EXPERTS, tm), lambda bi, i: (bi, 0, i))],
        out_shape=[
            jax.ShapeDtypeStruct((b, seq, D_MODEL), F32),
            jax.ShapeDtypeStruct((b, seq, D_MODEL), BF16),
            jax.ShapeDtypeStruct((b, N_EXPERTS, seq), F32),
        ],
        compiler_params=_cparams(("parallel", "parallel"), 48),
        name="merge",
    )(att, four, pooled, pb3, pb3, pb3, x3, mod3, wao, wf, wp, ps, wo, mod3, mod3, g2n, wr_t)


def _exclusive_prefix(mask, tri):
    e, seq = mask.shape
    ones = jnp.where(mask, 1.0, 0.0)
    offs = jnp.zeros((e, 1), F32)
    pieces = []
    for k in range(seq // LANE):
        blk = ones[:, k * LANE:(k + 1) * LANE]
        local = jnp.dot(blk.astype(BF16), tri, preferred_element_type=F32)
        pieces.append(local + offs)
        offs = offs + jnp.sum(blk, axis=1, keepdims=True)
    return jnp.concatenate(pieces, axis=1)


META_TSTART, META_TEND, META_PSTART, META_PEND, META_W = 0, 4, 8, 12, 16
GATHER_SLOT_BLOCK = 128
GATHER_WIN = 1408
GATHER_ALIGN = 128
COMBINE_TILE = 1024
COMBINE_WIN = 192
COMBINE_ALIGN = 16


def _select_kernel(lg_ref, slot_ref, aff_ref, meta_ref, *, cap):
    z = lg_ref[0]
    z = z - jnp.max(z, axis=0, keepdims=True)
    ez = jnp.exp(z)
    a = ez / jnp.sum(ez, axis=0, keepdims=True)
    aff_ref[0] = a
    capf = float(cap)

    def count_ge(th):
        return jnp.sum(jnp.where(a >= th, 1.0, 0.0), axis=1, keepdims=True)

    def bisect(_, lohi):
        lo, hi = lohi
        q2 = (lo + hi) * 0.5
        q1 = (lo + q2) * 0.5
        q3 = (q2 + hi) * 0.5
        g1, g2, g3 = count_ge(q1) >= capf, count_ge(q2) >= capf, count_ge(q3) >= capf
        new_lo = jnp.where(g3, q3, jnp.where(g2, q2, jnp.where(g1, q1, lo)))
        new_hi = jnp.where(g3, hi, jnp.where(g2, q3, jnp.where(g1, q2, q1)))
        return new_lo, new_hi

    e = a.shape[0]
    lo, hi = lax.fori_loop(0, SELECT_ITERS, bisect,
                           (jnp.zeros((e, 1), F32), jnp.full((e, 1), 2.0, F32)))
    r_i = lax.broadcasted_iota(I32, (LANE, LANE), 0)
    c_i = lax.broadcasted_iota(I32, (LANE, LANE), 1)
    tri = jnp.where(r_i < c_i, 1.0, 0.0).astype(BF16)
    above = a >= hi
    n_above = jnp.sum(jnp.where(above, 1.0, 0.0), axis=1, keepdims=True)
    tied = (a >= lo) & jnp.logical_not(above)
    tie_rank = _exclusive_prefix(tied, tri)
    sel = above | (tied & (tie_rank < capf - n_above))
    pos = _exclusive_prefix(sel, tri)
    slot_ref[0] = jnp.where(sel, pos, -1.0).astype(I32)

    seq = a.shape[1]
    t = lax.broadcasted_iota(I32, (e, seq), 1).astype(F32)
    lane = lax.broadcasted_iota(I32, (e, LANE), 1)
    meta = jnp.zeros((e, LANE), F32)
    sb = min(GATHER_SLOT_BLOCK, cap)
    for s in range(cap // sb):
        first = jnp.min(jnp.where(sel & (pos >= float(s * sb)), t, float(seq)), axis=1, keepdims=True)
        last = jnp.max(jnp.where(sel & (pos < float((s + 1) * sb)), t, -1.0), axis=1, keepdims=True)
        meta = jnp.where(lane == META_TSTART + s, first, meta)
        meta = jnp.where(lane == META_TEND + s, last, meta)
    tt = min(COMBINE_TILE, seq)
    for i in range(seq // tt):
        before = jnp.sum(jnp.where(sel & (t < float(i * tt)), 1.0, 0.0), axis=1, keepdims=True)
        upto = jnp.sum(jnp.where(sel & (t < float((i + 1) * tt)), 1.0, 0.0), axis=1, keepdims=True)
        meta = jnp.where(lane == META_PSTART + i, before, meta)
        meta = jnp.where(lane == META_PEND + i, upto, meta)
    meta_ref[0] = meta.astype(I32)


def _select(logits_t, cap):
    b, e, seq = logits_t.shape
    assert cap // min(GATHER_SLOT_BLOCK, cap) <= 4 and seq // min(COMBINE_TILE, seq) <= 4
    spec = pl.BlockSpec((1, e, seq), lambda bi: (bi, 0, 0))
    mspec = pl.BlockSpec((1, e, LANE), lambda bi: (bi, 0, 0))
    return pl.pallas_call(
        functools.partial(_select_kernel, cap=cap),
        grid=(b,),
        in_specs=[spec],
        out_specs=[spec, spec, mspec],
        out_shape=[jax.ShapeDtypeStruct((b, e, seq), I32), jax.ShapeDtypeStruct((b, e, seq), F32),
                   jax.ShapeDtypeStruct((b, e, LANE), I32)],
        compiler_params=_cparams(("parallel",), 32),
        name="select",
    )(logits_t)


def _align_down(v, align):
    shift = align.bit_length() - 1
    return lax.shift_left(lax.shift_right_logical(v, shift), shift)


def _gather_kernel(meta_ref, h_ref, slot_ref, aff_ref, xg_ref, gate_ref, *, cap, sb, win):
    seq = h_ref.shape[1]
    base = (pl.program_id(0) * N_EXPERTS + pl.program_id(1)) * META_W

    n_blocks = cap // sb

    def gather_blocks(starts, width):
        hits = []
        for s in range(n_blocks):
            j = lax.broadcasted_iota(I32, (sb, width), 0) + s * sb
            hits.append(slot_ref[0, 0, :, pl.ds(starts[s], width)] == j)
        for s in range(n_blocks):
            tok = pl.ds(starts[s], width)
            rows = slice(s * sb, (s + 1) * sb)
            onehot = jnp.where(hits[s], 1.0, 0.0).astype(BF16)
            xg_ref[0, 0, rows, :] = jnp.dot(onehot, h_ref[0, tok, :], preferred_element_type=F32).astype(BF16)
            gate_ref[0, 0, rows, :] = jnp.sum(jnp.where(hits[s], aff_ref[0, 0, :, tok], 0.0), axis=1, keepdims=True)

    if win >= seq:
        gather_blocks([0] * n_blocks, seq)
        return
    starts = []
    fits = None
    for s in range(n_blocks):
        first = meta_ref[base + META_TSTART + s]
        last = meta_ref[base + META_TEND + s]
        start = pl.multiple_of(jnp.minimum(_align_down(first, GATHER_ALIGN), seq - win), GATHER_ALIGN)
        starts.append(start)
        ok = last < start + win
        fits = ok if fits is None else jnp.logical_and(fits, ok)

    @pl.when(fits)
    def _():
        gather_blocks(starts, win)

    @pl.when(jnp.logical_not(fits))
    def _():
        gather_blocks([0] * n_blocks, seq)


def _gather(meta, h2, slots4, aff4, cap):
    b, seq, _ = h2.shape
    e = slots4.shape[1]
    sb = min(GATHER_SLOT_BLOCK, cap)
    win = GATHER_WIN if seq > GATHER_WIN else seq
    assert (seq - win) % GATHER_ALIGN == 0
    row = pl.BlockSpec((1, 1, 1, seq), lambda bi, ei, m: (bi, ei, 0, 0))
    return pl.pallas_call(
        functools.partial(_gather_kernel, cap=cap, sb=sb, win=win),
        grid_spec=pltpu.PrefetchScalarGridSpec(
            num_scalar_prefetch=1,
            grid=(b, e),
            in_specs=[pl.BlockSpec((1, seq, D_MODEL), lambda bi, ei, m: (bi, 0, 0)), row, row],
            out_specs=[
                pl.BlockSpec((1, 1, cap, D_MODEL), lambda bi, ei, m: (ei, bi, 0, 0)),
                pl.BlockSpec((1, 1, cap, 1), lambda bi, ei, m: (ei, bi, 0, 0)),
            ],
        ),
        out_shape=[
            jax.ShapeDtypeStruct((e, b, cap, D_MODEL), BF16),
            jax.ShapeDtypeStruct((e, b, cap, 1), F32),
        ],
        compiler_params=_cparams(("parallel", "arbitrary"), 48),
        name="gather",
    )(meta, h2, slots4, aff4)


FF_CHUNK = 256
FFN_ROW_TILES = 2


def _ffn_kernel(*refs, with_ctx):
    if with_ctx:
        x_ref, gate_ref, xc_ref, gatec_ref, wg_ref, wu_ref, wd_ref, o_ref, oc_ref, acc_ref, accc_ref = refs
    else:
        x_ref, gate_ref, wg_ref, wu_ref, wd_ref, o_ref, acc_ref = refs
    fc = pl.program_id(2)
    last_fc = pl.num_programs(2) - 1

    def swiglu_chunk(streams):
        @pl.when(fc == 0)
        def _():
            for _, _, _, acc in streams:
                acc[...] = jnp.zeros_like(acc)

        wg = wg_ref[0, 0].astype(BF16)
        wu = wu_ref[0, 0].astype(BF16)
        wd = wd_ref[0, 0].astype(BF16)
        xs = [x[0] for x, _, _, _ in streams]
        gates_ = [jnp.dot(x, wg, preferred_element_type=F32) for x in xs]
        ups = [jnp.dot(x, wu, preferred_element_type=F32) for x in xs]
        hmids = [(a * jax.nn.sigmoid(a) * u).astype(BF16) for a, u in zip(gates_, ups)]
        for (_, _, _, acc), hmid in zip(streams, hmids):
            acc[...] += jnp.dot(hmid, wd, preferred_element_type=F32)

        @pl.when(fc == last_fc)
        def _():
            for _, gate, o, acc in streams:
                o[0] = (acc[...] * gate[0]).astype(BF16)

    main = (x_ref, gate_ref, o_ref, acc_ref)
    if not with_ctx:
        swiglu_chunk([main])
        return
    on_last_tile = pl.program_id(1) == pl.num_programs(1) - 1

    @pl.when(jnp.logical_not(on_last_tile))
    def _():
        swiglu_chunk([main])

    @pl.when(on_last_tile)
    def _():
        swiglu_chunk([main, (xc_ref, gatec_ref, oc_ref, accc_ref)])


def _ffn(xg3, gate3, xc3, gatec3, wg_all, wu_all, wd_all, layer):
    e, m, _ = xg3.shape
    tm = m // FFN_ROW_TILES
    assert m % tm == 0 and tm % 16 == 0
    with_ctx = xc3 is not None
    row = lambda w: pl.BlockSpec((1, tm, w), lambda ei, i, fc: (ei, i, 0))
    in_specs = [row(D_MODEL), row(1)]
    out_specs = [row(D_MODEL)]
    out_shape = [jax.ShapeDtypeStruct((e, m, D_MODEL), BF16)]
    scratch = [pltpu.VMEM((tm, D_MODEL), F32)]
    args = [xg3, gate3]
    if with_ctx:
        mc = xc3.shape[1]
        crow = lambda w: pl.BlockSpec((1, mc, w), lambda ei, i, fc: (ei, 0, 0))
        in_specs += [crow(D_MODEL), crow(1)]
        out_specs.append(crow(D_MODEL))
        out_shape.append(jax.ShapeDtypeStruct((e, mc, D_MODEL), BF16))
        scratch.append(pltpu.VMEM((mc, D_MODEL), F32))
        args += [xc3, gatec3]
    in_specs += [
        pl.BlockSpec((1, 1, D_MODEL, FF_CHUNK), lambda ei, i, fc: (layer, ei, 0, fc)),
        pl.BlockSpec((1, 1, D_MODEL, FF_CHUNK), lambda ei, i, fc: (layer, ei, 0, fc)),
        pl.BlockSpec((1, 1, FF_CHUNK, D_MODEL), lambda ei, i, fc: (layer, ei, fc, 0)),
    ]
    res = pl.pallas_call(
        functools.partial(_ffn_kernel, with_ctx=with_ctx),
        grid=(e, m // tm, EXPERT_FF // FF_CHUNK),
        in_specs=in_specs,
        out_specs=out_specs,
        out_shape=out_shape,
        scratch_shapes=scratch,
        compiler_params=_cparams(("parallel", "arbitrary", "arbitrary"), 56),
        name="ffn",
    )(*args, wg_all, wu_all, wd_all)
    return (res[0], res[1]) if with_ctx else (res[0], None)


COMBINE_EXPERTS = 4


def _combine_kernel(meta_ref, slot_ref, gy_ref, x_ref, g2_ref, gf_ref, o_ref, acc_ref, *, cap, win, final):
    bi = pl.program_id(0)
    ti = pl.program_id(1)
    ec = pl.program_id(2)
    tt = x_ref.shape[1]
    tn = (((0,), (0,)), ((), ()))

    @pl.when(ec == 0)
    def _():
        acc_ref[...] = jnp.zeros_like(acc_ref)

    def scatter(starts, width):
        j = lax.broadcasted_iota(I32, (width, tt), 0)
        onehot = jnp.concatenate(
            [jnp.where(slot_ref[0, k] == j + starts[k], 1.0, 0.0).astype(BF16) for k in range(COMBINE_EXPERTS)],
            axis=0)
        gy = jnp.concatenate([gy_ref[k, pl.ds(starts[k], width), :] for k in range(COMBINE_EXPERTS)], axis=0)
        acc_ref[...] += lax.dot_general(onehot, gy, tn, preferred_element_type=F32)

    if win >= cap:
        scatter([0] * COMBINE_EXPERTS, cap)
    else:
        starts = []
        fits = None
        for k in range(COMBINE_EXPERTS):
            base = (bi * N_EXPERTS + ec * COMBINE_EXPERTS + k) * META_W
            before = meta_ref[base + META_PSTART + ti]
            upto = meta_ref[base + META_PEND + ti]
            start = pl.multiple_of(jnp.minimum(_align_down(before, COMBINE_ALIGN), cap - win), COMBINE_ALIGN)
            starts.append(start)
            ok = upto <= start + win
            fits = ok if fits is None else jnp.logical_and(fits, ok)

        @pl.when(fits)
        def _():
            scatter(starts, win)

        @pl.when(jnp.logical_not(fits))
        def _():
            scatter([0] * COMBINE_EXPERTS, cap)

    @pl.when(ec == pl.num_programs(2) - 1)
    def _():
        xn = x_ref[0] + g2_ref[0] * acc_ref[...]
        if final:
            ms = jnp.mean(xn * xn, axis=-1, keepdims=True)
            xn = (xn * lax.rsqrt(ms + RMS_EPS)) * gf_ref[...]
        o_ref[0] = xn


def _combine(meta, slots4, gy3, row_off, x3, mod3, gfinal, *, cap, ctx_row, final):
    b, seq, _ = x3.shape
    e = slots4.shape[1]
    tt = min(seq, COMBINE_TILE)
    win = COMBINE_WIN if cap > COMBINE_WIN else cap
    assert row_off % cap == 0 and (cap - win) % COMBINE_ALIGN == 0
    blk_off = row_off // cap
    mrow = (lambda bi: bi) if ctx_row is None else (lambda bi: ctx_row)
    return pl.pallas_call(
        functools.partial(_combine_kernel, cap=cap, win=win, final=final),
        grid_spec=pltpu.PrefetchScalarGridSpec(
            num_scalar_prefetch=1,
            grid=(b, seq // tt, e // COMBINE_EXPERTS),
            in_specs=[
                pl.BlockSpec((1, COMBINE_EXPERTS, 1, tt), lambda bi, i, ec, m: (bi, ec, 0, i)),
                pl.BlockSpec((COMBINE_EXPERTS, cap, D_MODEL), lambda bi, i, ec, m: (ec, blk_off + bi, 0)),
                pl.BlockSpec((1, tt, D_MODEL), lambda bi, i, ec, m: (bi, i, 0)),
                pl.BlockSpec((1, 1, D_MODEL), lambda bi, i, ec, m: (mrow(bi), 0, 5)),
                pl.BlockSpec((1, D_MODEL), lambda bi, i, ec, m: (0, 0)),
            ],
            out_specs=pl.BlockSpec((1, tt, D_MODEL), lambda bi, i, ec, m: (bi, i, 0)),
            scratch_shapes=[pltpu.VMEM((tt, D_MODEL), F32)],
        ),
        out_shape=jax.ShapeDtypeStruct((b, seq, D_MODEL), F32),
        compiler_params=_cparams(("parallel", "parallel", "arbitrary"), 48),
        name="combine",
    )(meta, slots4, gy3, x3, mod3, gfinal)


def _moe_route(h2, logits_t):
    b, seq, _ = h2.shape
    cap = EC_CAPACITY_FACTOR * seq // N_EXPERTS
    slots, aff, meta = _select(logits_t, cap)
    meta = meta[:, :, :META_W].reshape(-1)
    slots4 = slots.reshape(b, N_EXPERTS, 1, seq)
    aff4 = aff.reshape(b, N_EXPERTS, 1, seq)
    xg, gate = _gather(meta, h2, slots4, aff4, cap)
    return meta, slots4, xg.reshape(N_EXPERTS, b * cap, D_MODEL), gate.reshape(N_EXPERTS, b * cap, 1), cap


def kernel(x, c, ctx, c_ctx, ada_w, ada_b, norm1_g, norm2_g, w_in, rpb, w_att_o, w_fourier, w_pool,
           pool_scale, w_out, w_router, w_exp_gate, w_exp_up, w_exp_down, final_norm_g):
    b, seq, d = x.shape
    lc = ctx.shape[1]
    assert d == D_MODEL and seq % (GRID_W * ATT_QROWS) == 0 and b + 1 <= MOD_ROWS
    rows = seq // GRID_W
    ctx_row = b

    cond = jnp.concatenate([c, c_ctx[None, :], jnp.zeros((MOD_ROWS - b - 1, d), F32)], axis=0)
    cl, sl = _dft_mats(seq, seq // 2)
    clc, slc = _dft_mats(lc, lc // 2)
    cc, sc = _dft_mats(FOURIER_GROUP_W, FOURIER_GROUP_W)
    gfinal = final_norm_g.reshape(1, d)

    for i in range(DEPTH):
        update_ctx = i < DEPTH - 1
        mod3 = _adaln(cond, ada_w[i], ada_b[i]).reshape(MOD_ROWS, 1, 6 * d)
        g1n = norm1_g[i].reshape(1, d)
        g2n = norm2_g[i].reshape(1, d)
        wao = w_att_o[i].astype(BF16)
        wf = w_fourier[i].astype(BF16)
        wp = w_pool[i].astype(BF16)
        ps = pool_scale[i].reshape(1, d)
        wo = w_out[i].astype(BF16)
        wr_t = jnp.pad(jnp.concatenate([w_router[i], w_router[i]], axis=1), ((0, 0), (0, LANE - 2 * N_EXPERTS)))
        bias = _bias_table(rpb[i], rows)

        pb, pp = _modproj(x.reshape(b * seq, d), mod3, g1n, w_in, i, seq=seq, ctx_row=None,
                          n_out=IN_W, with_pool=True)
        n_ctx = IN_W if update_ctx else Q_OFF
        pcb, pcp = _modproj(ctx.reshape(b * lc, d), mod3, g1n, w_in, i, seq=lc, ctx_row=ctx_row,
                            n_out=n_ctx, with_pool=update_ctx)
        pb3 = pb.reshape(b, seq, IN_W)
        pcb3 = pcb.reshape(b, lc, n_ctx)

        att = _nattn(pb3, pcb3, bias)
        four = _fourier(pb3, cc, sc, cl, sl)
        pooled = _pool(pp.reshape(b, seq, POOL_W))
        x, h2, logits_t = _merge(att, four, pooled, pb3, x, mod3, wao, wf, wp, ps, wo, g2n, wr_t, ctx_row=None)
        meta_x, slots_x, xg, gate, cap = _moe_route(h2, logits_t)

        if update_ctx:
            att_c = _cattn(pcb3)
            four_c = _fourier(pcb3, cc, sc, clc, slc)
            pooled_c = _pool(pcp.reshape(b, lc, POOL_W))
            ctx, h2_c, logits_c = _merge(att_c, four_c, pooled_c, pcb3, ctx, mod3, wao, wf, wp, ps, wo, g2n, wr_t,
                                         ctx_row=ctx_row)
            meta_c, slots_c, xg_c, gate_c, cap_c = _moe_route(h2_c, logits_c)
        else:
            xg_c = gate_c = None

        gy, gy_c = _ffn(xg, gate, xg_c, gate_c, w_exp_gate, w_exp_up, w_exp_down, i)
        if update_ctx:
            ctx = _combine(meta_c, slots_c, gy_c, 0, ctx, mod3, gfinal, cap=cap_c, ctx_row=ctx_row, final=False)
        x = _combine(meta_x, slots_x, gy, 0, x, mod3, gfinal, cap=cap, ctx_row=None, final=not update_ctx)
    return x
```

```python
import functools
import math

import jax
import jax.numpy as jnp
from jax import lax
from jax.experimental import pallas as pl
from jax.experimental.pallas import tpu as pltpu

F32 = jnp.float32
BF16 = jnp.bfloat16
I32 = jnp.int32
HIGHEST = lax.Precision.HIGHEST

D_MODEL = 1024
DEPTH = 2
GRID_W = 64
N_HEADS = 16
HEAD_DIM = 64
WIN_R = 8
WIN_C = 16
FOURIER_GROUPS = 4
FOURIER_GROUP_W = 128
FOURIER_W = 512
POOL_WINDOWS = (2, 4, 8, 16)
POOL_GROUP_W = 128
POOL_W = 512
POOL_OUT_GROUP = 256
K_OFF, V_OFF, Q_OFF, F_OFF, P_OFF, G_OFF = 0, 1024, 2048, 3072, 3584, 4096
IN_W = 7168
N_EXPERTS = 16
EC_CAPACITY_FACTOR = 2
EXPERT_FF = 2816
RMS_EPS = 1e-6

LANE = 128
HEAD_PAIR_W = 2 * HEAD_DIM
N_HEAD_PAIRS = N_HEADS // 2
MOD_ROWS = 16
NEG_BIG = -1e30
ATT_QROWS = 2
ATT_KROWS = 10
ATT_TOP = (WIN_R // 2 + ATT_QROWS - 1) // ATT_QROWS
ATT_BOT = (WIN_R // 2 - 1 + ATT_QROWS - 1) // ATT_QROWS
ATT_CLASSES = ATT_TOP + 1 + ATT_BOT
ATT_ALIGN = 64
SELECT_ITERS = 32
MiB = 1024 * 1024


def _cparams(sem, vmem_mib):
    return pltpu.CompilerParams(dimension_semantics=sem, vmem_limit_bytes=vmem_mib * MiB)


def _adaln_kernel(c_ref, w_ref, b_ref, o_ref):
    c = c_ref[...]
    s = c * jax.nn.sigmoid(c)
    o_ref[...] = jnp.dot(s, w_ref[...], precision=HIGHEST, preferred_element_type=F32) + b_ref[...]


def _adaln(cond_rows, ada_w, ada_b):
    n = ada_w.shape[1]
    tn = 1024
    return pl.pallas_call(
        _adaln_kernel,
        grid=(n // tn,),
        in_specs=[
            pl.BlockSpec((MOD_ROWS, D_MODEL), lambda j: (0, 0)),
            pl.BlockSpec((D_MODEL, tn), lambda j: (0, j)),
            pl.BlockSpec((1, tn), lambda j: (0, j)),
        ],
        out_specs=pl.BlockSpec((MOD_ROWS, tn), lambda j: (0, j)),
        out_shape=jax.ShapeDtypeStruct((MOD_ROWS, n), F32),
        compiler_params=_cparams(("arbitrary",), 32),
        name="adaln",
    )(cond_rows, ada_w, ada_b.reshape(1, n))


def _modulate(x, g, shift, scale):
    ms = jnp.mean(x * x, axis=-1, keepdims=True)
    y = x * lax.rsqrt(ms + RMS_EPS)
    return (y * g) * (1.0 + scale) + shift


MODPROJ_TN = 1792


def _modproj_kernel(x_ref, sh_ref, sc_ref, g_ref, w_ref, *rest, pool_tile):
    if pool_tile is None:
        o_ref, h_ref = rest
        pp_ref = None
    else:
        o_ref, pp_ref, h_ref = rest
    j = pl.program_id(1)

    @pl.when(j == 0)
    def _():
        h = _modulate(x_ref[...], g_ref[...], sh_ref[0], sc_ref[0])
        h_ref[...] = h.astype(BF16)

    acc = jnp.dot(h_ref[...], w_ref[...], preferred_element_type=F32)
    o_ref[...] = acc.astype(BF16)
    if pool_tile is not None:
        @pl.when(j == pool_tile)
        def _():
            lo = P_OFF % acc.shape[1]
            pp_ref[...] = acc[:, lo:lo + POOL_W]


def _modproj(x2, mod3, gain, w, *, seq, ctx_row, n_out, with_pool):
    rows = x2.shape[0]
    tn = MODPROJ_TN if n_out % MODPROJ_TN == 0 else 1024
    if ctx_row is None:
        tm = min(seq, 1024)
        tiles_per_seq = seq // tm
        mrow = lambda i: i // tiles_per_seq
    else:
        tm = min(rows, 1024)
        mrow = lambda i: ctx_row
    assert rows % tm == 0 and n_out % tn == 0 and (not with_pool or P_OFF % tn + POOL_W <= tn)
    pool_tile = (P_OFF // tn) if with_pool else None
    out_shape = [jax.ShapeDtypeStruct((rows, n_out), BF16)]
    out_specs = [pl.BlockSpec((tm, tn), lambda i, j: (i, j))]
    if with_pool:
        out_shape.append(jax.ShapeDtypeStruct((rows, POOL_W), F32))
        out_specs.append(pl.BlockSpec((tm, POOL_W), lambda i, j: (i, 0)))
    res = pl.pallas_call(
        functools.partial(_modproj_kernel, pool_tile=pool_tile),
        grid=(rows // tm, n_out // tn),
        in_specs=[
            pl.BlockSpec((tm, D_MODEL), lambda i, j: (i, 0)),
            pl.BlockSpec((1, 1, D_MODEL), lambda i, j: (mrow(i), 0, 0)),
            pl.BlockSpec((1, 1, D_MODEL), lambda i, j: (mrow(i), 0, 1)),
            pl.BlockSpec((1, D_MODEL), lambda i, j: (0, 0)),
            pl.BlockSpec((D_MODEL, tn), lambda i, j: (0, j)),
        ],
        out_specs=out_specs,
        out_shape=out_shape,
        scratch_shapes=[pltpu.VMEM((tm, D_MODEL), BF16)],
        compiler_params=_cparams(("parallel", "arbitrary"), 48),
        name="modproj",
    )(x2, mod3, mod3, gain, w)
    return res if with_pool else (res[0], None)


RPB_ROWS = 16


def _bias_kernel(rpb_ref, o_ref, *, rows):
    cls = pl.program_id(1)
    i_rep = jnp.where(cls <= ATT_TOP, cls, cls - (ATT_TOP + 1) + (rows // ATT_QROWS - ATT_BOT))
    s = jnp.clip(ATT_QROWS * i_rep - WIN_R // 2, 0, rows - ATT_KROWS)
    qc = lax.broadcasted_iota(I32, (GRID_W, LANE), 0)
    lane = lax.broadcasted_iota(I32, (GRID_W, LANE), 1)
    kc = lane & (GRID_W - 1)
    first_half = lane < GRID_W
    cs = jnp.clip(qc - WIN_C // 2, 0, GRID_W - WIN_C)
    col_valid = (kc >= cs) & (kc < cs + WIN_C)
    n_roff = 2 * WIN_R - 1
    shift_a = LANE - (WIN_C - 1)
    shift_b = GRID_W - (WIN_C - 1)
    for ri in range(ATT_QROWS):
        r = ATT_QROWS * i_rep + ri
        rs = jnp.clip(r - WIN_R // 2, 0, rows - WIN_R)
        for m in range(ATT_KROWS // 2):
            krow_a = s + 2 * m
            krow_b = krow_a + 1
            va = ((krow_a >= rs) & (krow_a < rs + WIN_R)).astype(I32)
            vb = ((krow_b >= rs) & (krow_b < rs + WIN_R)).astype(I32)
            ro_a = jnp.clip(krow_a - r + WIN_R - 1, 0, n_roff - 1)
            ro_b = jnp.clip(krow_b - r + WIN_R - 1, 0, n_roff - 1)
            row_a = jnp.broadcast_to(rpb_ref[0, pl.ds(ro_a, 1), :], (GRID_W, LANE))
            row_b = jnp.broadcast_to(rpb_ref[0, pl.ds(ro_b, 1), :], (GRID_W, LANE))
            acc = jnp.where(first_half,
                            pltpu.roll(row_a, shift_a, 1, stride=1, stride_axis=0),
                            pltpu.roll(row_b, shift_b, 1, stride=1, stride_axis=0))
            row_valid = jnp.where(first_half, va, vb) > 0
            tile = jnp.where(col_valid & row_valid, acc, NEG_BIG)
            o_ref[0, 0, ri * GRID_W:(ri + 1) * GRID_W, m * LANE:(m + 1) * LANE] = tile


def _bias_table(rpb, rows):
    nq = ATT_QROWS * GRID_W
    nk = ATT_KROWS * GRID_W
    n_roff, n_coff = rpb.shape[1], rpb.shape[2]
    assert n_roff <= RPB_ROWS and n_coff <= LANE
    rpb_p = jnp.pad(rpb, ((0, 0), (0, RPB_ROWS - n_roff), (0, LANE - n_coff)))
    return pl.pallas_call(
        functools.partial(_bias_kernel, rows=rows),
        grid=(N_HEADS, ATT_CLASSES),
        in_specs=[pl.BlockSpec((1, RPB_ROWS, LANE), lambda h, c: (h, 0, 0))],
        out_specs=pl.BlockSpec((1, 1, nq, nk), lambda h, c: (h, c, 0, 0)),
        out_shape=jax.ShapeDtypeStruct((N_HEADS, ATT_CLASSES, nq, nk), F32),
        compiler_params=_cparams(("arbitrary", "arbitrary"), 32),
        name="bias_table",
    )(rpb_p)


_NT = (((1,), (1,)), ((), ()))


def _scores(qm, kw, kc, bias):
    sw = lax.dot_general(qm, kw, _NT, preferred_element_type=F32)
    if bias is not None:
        sw = sw + bias
    sc = lax.dot_general(qm, kc, _NT, preferred_element_type=F32) if kc is not None else None
    return sw, sc


def _probs(sw, sc):
    m = jnp.max(sw, axis=-1, keepdims=True)
    if sc is not None:
        m = jnp.maximum(m, jnp.max(sc, axis=-1, keepdims=True))
    pw = jnp.exp(sw - m)
    l = jnp.sum(pw, axis=-1, keepdims=True)
    pc = None
    if sc is not None:
        pc = jnp.exp(sc - m)
        l = l + jnp.sum(pc, axis=-1, keepdims=True)
        pc = pc.astype(BF16)
    return pw.astype(BF16), pc, l


def _pv(pw, pc, l, vw, vc):
    o = jnp.dot(pw, vw, preferred_element_type=F32)
    if pc is not None:
        o = o + jnp.dot(pc, vc, preferred_element_type=F32)
    return o / l


ATT_UNROLL = 4


ATT_KEY_TILE = 256


def _key_tiles(n):
    return [(off, min(ATT_KEY_TILE, n - off)) for off in range(0, n, ATT_KEY_TILE)]


def _lane_fold(x, op):
    out = x[:, :LANE]
    for c in range(1, x.shape[1] // LANE):
        out = op(out, x[:, c * LANE:(c + 1) * LANE])
    return out


def _nattn_kernel(k_ref, v_ref, q_ref, kc_ref, vc_ref, b_ref, o_ref, s_ref, m_ref, *, rows):
    nq = ATT_QROWS * GRID_W
    nk = ATT_KROWS * GRID_W
    lc = kc_ref.shape[1]
    n_steps = rows // ATT_QROWS
    lane = lax.broadcasted_iota(I32, (nq, HEAD_PAIR_W), 1)
    second = lane >= HEAD_DIM
    scale = HEAD_DIM ** -0.5
    win_tiles = _key_tiles(nk)
    ctx_tiles = _key_tiles(lc)

    ones_cols = jnp.ones((ATT_KEY_TILE, HEAD_PAIR_W), BF16)

    def with_ones(v):
        return jnp.concatenate([v, ones_cols[:v.shape[0]]], axis=1)

    def geometry(ii, u):
        i = ii * ATT_UNROLL + u
        s = jnp.clip(ATT_QROWS * i - WIN_R // 2, 0, rows - ATT_KROWS)
        kstart = pl.multiple_of(s * GRID_W, ATT_ALIGN)
        qstart = pl.multiple_of(i * nq, ATT_ALIGN)
        cls = jnp.where(i < ATT_TOP, i,
                        jnp.where(i < n_steps - ATT_BOT, ATT_TOP, i - (n_steps - ATT_BOT) + ATT_TOP + 1))
        return kstart, qstart, cls

    def pass1(ii, buf, u, hh):
        kstart, qstart, cls = geometry(ii, u)
        c = 2 * u + hh
        q2 = (q_ref[0, pl.ds(qstart, nq), :].astype(F32) * scale).astype(BF16)
        head_lanes = second if hh else jnp.logical_not(second)
        qm = jnp.where(head_lanes, q2, jnp.zeros_like(q2))
        m_run = None
        for off, width in win_tiles:
            kt = k_ref[0, pl.ds(kstart + off, width), :]
            st = lax.dot_general(qm, kt, _NT, preferred_element_type=F32) + b_ref[hh, cls, :, off:off + width]
            s_ref[buf, c, :, off:off + width] = st
            mt = _lane_fold(st, jnp.maximum)
            m_run = mt if m_run is None else jnp.maximum(m_run, mt)
        for off, width in ctx_tiles:
            st = lax.dot_general(qm, kc_ref[0, off:off + width, :], _NT, preferred_element_type=F32)
            s_ref[buf, c, :, nk + off:nk + off + width] = st
            m_run = jnp.maximum(m_run, _lane_fold(st, jnp.maximum))
        m_ref[buf, c] = jnp.max(m_run, axis=-1, keepdims=True)

    def pass2(ii, buf, u, hh):
        kstart, _, _ = geometry(ii, u)
        c = 2 * u + hh
        m = m_ref[buf, c]
        acc = None
        for off, width in win_tiles:
            pt = jnp.exp(s_ref[buf, c, :, off:off + width] - m).astype(BF16)
            pv = jnp.dot(pt, with_ones(v_ref[0, pl.ds(kstart + off, width), :]), preferred_element_type=F32)
            acc = pv if acc is None else acc + pv
        for off, width in ctx_tiles:
            pt = jnp.exp(s_ref[buf, c, :, nk + off:nk + off + width] - m).astype(BF16)
            acc = acc + jnp.dot(pt, with_ones(vc_ref[0, off:off + width, :]), preferred_element_type=F32)
        return acc[:, :HEAD_PAIR_W] / acc[:, HEAD_PAIR_W:]

    def store(ii, u, o0, o1):
        _, qstart, _ = geometry(ii, u)
        o_ref[0, pl.ds(qstart, nq), :] = jnp.where(second, o1, o0).astype(BF16)

    n_iter = n_steps // ATT_UNROLL
    for u in range(ATT_UNROLL):
        for hh in range(2):
            pass1(0, 0, u, hh)

    def overlapped(ii, new):
        for u in range(ATT_UNROLL):
            outs = []
            for hh in range(2):
                pass1(ii, new, u, hh)
                outs.append(pass2(ii - 1, 1 - new, u, hh))
            store(ii - 1, u, *outs)

    def body(jj, carry):
        overlapped(2 * jj + 1, 1)
        overlapped(2 * jj + 2, 0)
        return carry

    assert n_iter % 2 == 0
    lax.fori_loop(0, n_iter // 2 - 1, body, 0)
    last = n_iter - 1
    overlapped(last, 1)
    for u in range(ATT_UNROLL):
        store(last, u, *[pass2(last, 1, u, hh) for hh in range(2)])


def _nattn(pb3, pcb3, bias):
    b, seq, _ = pb3.shape
    lc = pcb3.shape[1]
    rows = seq // GRID_W
    nq = ATT_QROWS * GRID_W
    nk = ATT_KROWS * GRID_W
    kblk, vblk, qblk = K_OFF // LANE, V_OFF // LANE, Q_OFF // LANE
    return pl.pallas_call(
        functools.partial(_nattn_kernel, rows=rows),
        grid=(N_HEAD_PAIRS, b),
        in_specs=[
            pl.BlockSpec((1, seq, HEAD_PAIR_W), lambda hp, bi: (bi, 0, kblk + hp)),
            pl.BlockSpec((1, seq, HEAD_PAIR_W), lambda hp, bi: (bi, 0, vblk + hp)),
            pl.BlockSpec((1, seq, HEAD_PAIR_W), lambda hp, bi: (bi, 0, qblk + hp)),
            pl.BlockSpec((1, lc, HEAD_PAIR_W), lambda hp, bi: (bi, 0, kblk + hp)),
            pl.BlockSpec((1, lc, HEAD_PAIR_W), lambda hp, bi: (bi, 0, vblk + hp)),
            pl.BlockSpec((2, ATT_CLASSES, nq, nk), lambda hp, bi: (hp, 0, 0, 0)),
        ],
        out_specs=pl.BlockSpec((1, seq, HEAD_PAIR_W), lambda hp, bi: (bi, 0, hp)),
        out_shape=jax.ShapeDtypeStruct((b, seq, N_HEADS * HEAD_DIM), BF16),
        scratch_shapes=[pltpu.VMEM((2, 2 * ATT_UNROLL, nq, nk + lc), F32),
                        pltpu.VMEM((2, 2 * ATT_UNROLL, nq, 1), F32)],
        compiler_params=_cparams(("parallel", "parallel"), 40),
        name="nattn",
    )(pb3, pb3, pb3, pcb3, pcb3, bias)


def _cattn_kernel(k_ref, v_ref, q_ref, o_ref):
    lc = q_ref.shape[1]
    lane = lax.broadcasted_iota(I32, (lc, HEAD_PAIR_W), 1)
    second = lane >= HEAD_DIM
    q2 = (q_ref[0].astype(F32) * (HEAD_DIM ** -0.5)).astype(BF16)
    k = k_ref[0]
    v = v_ref[0]
    outs = []
    for hh in range(2):
        head_lanes = second if hh else jnp.logical_not(second)
        qm = jnp.where(head_lanes, q2, jnp.zeros_like(q2))
        pw, _, l = _probs(*_scores(qm, k, None, None))
        outs.append(_pv(pw, None, l, v, None))
    o_ref[0] = jnp.where(second, outs[1], outs[0]).astype(BF16)


def _cattn(pcb3):
    b, lc, _ = pcb3.shape
    kblk, vblk, qblk = K_OFF // LANE, V_OFF // LANE, Q_OFF // LANE
    return pl.pallas_call(
        _cattn_kernel,
        grid=(b, N_HEAD_PAIRS),
        in_specs=[
            pl.BlockSpec((1, lc, HEAD_PAIR_W), lambda bi, hp: (bi, 0, kblk + hp)),
            pl.BlockSpec((1, lc, HEAD_PAIR_W), lambda bi, hp: (bi, 0, vblk + hp)),
            pl.BlockSpec((1, lc, HEAD_PAIR_W), lambda bi, hp: (bi, 0, qblk + hp)),
        ],
        out_specs=pl.BlockSpec((1, lc, HEAD_PAIR_W), lambda bi, hp: (bi, 0, hp)),
        out_shape=jax.ShapeDtypeStruct((b, lc, N_HEADS * HEAD_DIM), BF16),
        compiler_params=_cparams(("parallel", "parallel"), 32),
        name="cattn",
    )(pcb3, pcb3, pcb3)


def _dft_kernel(c_ref, s_ref, *, n):
    tk, ncols = c_ref.shape
    k = pl.program_id(0) * tk + lax.broadcasted_iota(I32, (tk, LANE), 0)
    lane = lax.broadcasted_iota(I32, (tk, LANE), 1)
    w = 2.0 * math.pi / n
    ang_p = ((k * lane) & (n - 1)).astype(F32) * w
    cp, sp = jnp.cos(ang_p), jnp.sin(ang_p)
    ang_q = ((k * (lane * LANE)) & (n - 1)).astype(F32) * w
    cq, sq = jnp.cos(ang_q), jnp.sin(ang_q)
    for q in range(ncols // LANE):
        cols = slice(q * LANE, (q + 1) * LANE)
        cqq, sqq = cq[:, q:q + 1], sq[:, q:q + 1]
        c_ref[:, cols] = (cqq * cp - sqq * sp).astype(BF16)
        s_ref[:, cols] = (sqq * cp + cqq * sp).astype(BF16)


def _dft_mats(n, ncols):
    tk = min(n, 256)
    return pl.pallas_call(
        functools.partial(_dft_kernel, n=n),
        grid=(n // tk,),
        in_specs=[],
        out_specs=[pl.BlockSpec((tk, ncols), lambda i: (i, 0))] * 2,
        out_shape=[jax.ShapeDtypeStruct((n, ncols), BF16)] * 2,
        compiler_params=_cparams(("parallel",), 48),
        name=f"dft_mats_{n}",
    )()


REV_BLOCK = 128


def _fourier_kernel(u_ref, cc_ref, sc_ref, cl_ref, sl_ref, o_ref, us_ref, ud_ref, a_ref, b_ref, ah_ref, *, seq):
    half = seq // 2
    nblk = seq // REV_BLOCK
    tk = o_ref.shape[1]

    @pl.when(pl.program_id(1) == 0)
    def _():
        d_i = lax.broadcasted_iota(I32, (REV_BLOCK, REV_BLOCK), 0)
        s_i = lax.broadcasted_iota(I32, (REV_BLOCK, REV_BLOCK), 1)
        flip = jnp.where((d_i >= 1) & (s_i == REV_BLOCK - d_i), 1.0, 0.0).astype(BF16)
        row = lax.broadcasted_iota(I32, (REV_BLOCK, FOURIER_W), 0)
        for blk in range(nblk // 2):
            lo = u_ref[0, blk * REV_BLOCK:(blk + 1) * REV_BLOCK, :].astype(F32)
            src = u_ref[0, (nblk - 1 - blk) * REV_BLOCK:(nblk - blk) * REV_BLOCK, :]
            rev = jnp.dot(flip, src, preferred_element_type=F32)
            if blk > 0:
                head = u_ref[0, (nblk - blk) * REV_BLOCK:(nblk - blk) * REV_BLOCK + 16, :].astype(F32)
                rev = jnp.where(row == 0, head[0:1, :], rev)
            rows = slice(blk * REV_BLOCK, (blk + 1) * REV_BLOCK)
            us_ref[rows, :] = (lo + rev).astype(BF16)
            ud_ref[rows, :] = (lo - rev).astype(BF16)
        mid = u_ref[0, half:half + 16, :]
        for g in range(FOURIER_GROUPS):
            sl = slice(g * FOURIER_GROUP_W, (g + 1) * FOURIER_GROUP_W)
            a_ref[:, sl] = jnp.dot(us_ref[:, sl], cc_ref[...], preferred_element_type=F32).astype(BF16)
            b_ref[:, sl] = jnp.dot(ud_ref[:, sl], sc_ref[...], preferred_element_type=F32).astype(BF16)
            ah_ref[:, sl] = jnp.dot(mid[:, sl], cc_ref[...], preferred_element_type=F32)

    k = pl.program_id(1) * tk + lax.broadcasted_iota(I32, (tk, 1), 0)
    sign = (1 - 2 * (k & 1)).astype(F32)
    y = (jnp.dot(cl_ref[...], a_ref[...], preferred_element_type=F32)
         - jnp.dot(sl_ref[...], b_ref[...], preferred_element_type=F32)
         + sign * ah_ref[0:1, :])
    o_ref[0] = (y * (1.0 / math.sqrt(seq * FOURIER_GROUP_W))).astype(BF16)


def _fourier(pb3, cc, sc, cl, sl):
    b, seq, _ = pb3.shape
    tk = min(seq, 512)
    half = seq // 2
    assert seq % (2 * REV_BLOCK) == 0
    return pl.pallas_call(
        functools.partial(_fourier_kernel, seq=seq),
        grid=(b, seq // tk),
        in_specs=[
            pl.BlockSpec((1, seq, FOURIER_W), lambda bi, k: (bi, 0, F_OFF // FOURIER_W)),
            pl.BlockSpec((FOURIER_GROUP_W, FOURIER_GROUP_W), lambda bi, k: (0, 0)),
            pl.BlockSpec((FOURIER_GROUP_W, FOURIER_GROUP_W), lambda bi, k: (0, 0)),
            pl.BlockSpec((tk, half), lambda bi, k: (k, 0)),
            pl.BlockSpec((tk, half), lambda bi, k: (k, 0)),
        ],
        out_specs=pl.BlockSpec((1, tk, FOURIER_W), lambda bi, k: (bi, k, 0)),
        out_shape=jax.ShapeDtypeStruct((b, seq, FOURIER_W), BF16),
        scratch_shapes=[pltpu.VMEM((half, FOURIER_W), BF16)] * 4 + [pltpu.VMEM((16, FOURIER_W), F32)],
        compiler_params=_cparams(("parallel", "arbitrary"), 48),
        name="fourier",
    )(pb3, cc, sc, cl, sl)


POOL_PAD = 8


def _pool_kernel(u_ref, o_ref, pad_ref, *, seq):
    t = lax.broadcasted_iota(I32, (seq, POOL_GROUP_W), 0)
    zeros = jnp.zeros((POOL_PAD, POOL_GROUP_W), F32)
    pad_ref[0:POOL_PAD, :] = zeros
    pad_ref[seq + POOL_PAD:seq + 2 * POOL_PAD, :] = zeros
    pad_ref[POOL_PAD:seq + POOL_PAD, :] = u_ref[0]
    for g, w in enumerate(POOL_WINDOWS):
        @pl.when(pl.program_id(1) == g)
        def _(w=w):
            acc = None
            for d in range(-(w // 2), w - w // 2):
                term = pad_ref[pl.ds(POOL_PAD + d, seq), :]
                acc = term if acc is None else acc + term
            cnt = (jnp.minimum(t + (w - w // 2), seq) - jnp.maximum(t - w // 2, 0)).astype(F32)
            o_ref[0] = (acc / cnt - u_ref[0]).astype(BF16)


def _pool(pp3):
    b, seq, _ = pp3.shape
    spec = pl.BlockSpec((1, seq, POOL_GROUP_W), lambda bi, g: (bi, 0, g))
    return pl.pallas_call(
        functools.partial(_pool_kernel, seq=seq),
        grid=(b, len(POOL_WINDOWS)),
        in_specs=[spec],
        out_specs=spec,
        out_shape=jax.ShapeDtypeStruct((b, seq, POOL_W), BF16),
        scratch_shapes=[pltpu.VMEM((seq + 2 * POOL_PAD, POOL_GROUP_W), F32)],
        compiler_params=_cparams(("parallel", "parallel"), 32),
        name="pool",
    )(pp3)


MERGE_PARTS = 4


def _merge_kernel(att_ref, four_ref, pool_ref, ga_ref, gf_ref, gp_ref, x_ref, g1_ref,
                  wao_ref, wf_ref, wp_ref, ps_ref, wo_ref, sh2_ref, sc2_ref, g2n_ref, wr_ref,
                  o_ref, h_ref, lg_ref):
    tm = x_ref.shape[1]
    n_parts = min(MERGE_PARTS, tm // LANE)
    parts = [slice(p * (tm // n_parts), (p + 1) * (tm // n_parts)) for p in range(n_parts)]
    wr = wr_ref[...]
    wr_hi = wr.astype(BF16)
    wr_lo = (wr - wr_hi.astype(F32)).astype(BF16)
    wr_split = jnp.where(lax.broadcasted_iota(I32, wr.shape, 1) < N_EXPERTS, wr_hi, wr_lo)

    def sigmoid(ref, rows):
        return 0.5 * jnp.tanh(0.5 * ref[0, rows, :].astype(F32)) + 0.5

    branches = []
    for rows in parts:
        y_att = jnp.dot(att_ref[0, rows, :], wao_ref[...], preferred_element_type=F32)
        y_four = jnp.dot(four_ref[0, rows, :], wf_ref[...], preferred_element_type=F32)
        pooled = pool_ref[0, rows, :]
        y_pool = jnp.concatenate(
            [jnp.dot(pooled[:, g * POOL_GROUP_W:(g + 1) * POOL_GROUP_W], wp_ref[g], preferred_element_type=F32)
             for g in range(len(POOL_WINDOWS))], axis=-1) * ps_ref[...]
        branches.append((y_att, y_four, y_pool))
    resid = []
    for rows, (y_att, y_four, y_pool) in zip(parts, branches):
        merged = (sigmoid(ga_ref, rows) * y_att + sigmoid(gf_ref, rows) * y_four + sigmoid(gp_ref, rows) * y_pool)
        y = jnp.dot(merged.astype(BF16), wo_ref[...], preferred_element_type=F32)
        xn = x_ref[0, rows, :] + g1_ref[0] * y
        o_ref[0, rows, :] = xn
        resid.append(xn)
    for rows, xn in zip(parts, resid):
        n = xn.shape[0]
        h = _modulate(xn, g2n_ref[...], sh2_ref[0], sc2_ref[0])
        h_hi = h.astype(BF16)
        h_ref[0, rows, :] = h_hi
        h_lo = (h - h_hi.astype(F32)).astype(BF16)
        r = jnp.dot(jnp.concatenate([h_hi, h_lo], axis=0), wr_split, preferred_element_type=F32)
        hi_t = r[:n].T
        lo_t = r[n:].T
        lg_ref[0, :, rows] = (hi_t[:N_EXPERTS] + hi_t[N_EXPERTS:2 * N_EXPERTS]
                              + lo_t[:N_EXPERTS] + lo_t[N_EXPERTS:2 * N_EXPERTS])


def _merge(att, four, pooled, pb3, x3, mod3, wao, wf, wp, ps, wo, g2n, wr_t, *, ctx_row):
    b, seq, _ = x3.shape
    tm = min(seq, 512)
    gblk = G_OFF // D_MODEL
    mrow = (lambda bi: bi) if ctx_row is None else (lambda bi: ctx_row)
    tok = lambda w: pl.BlockSpec((1, tm, w), lambda bi, i: (bi, i, 0))
    gate = lambda k: pl.BlockSpec((1, tm, D_MODEL), lambda bi, i: (bi, i, gblk + k))
    full = lambda a: pl.BlockSpec(a.shape, lambda bi, i: (0,) * a.ndim)
    modc = lambda k: pl.BlockSpec((1, 1, D_MODEL), lambda bi, i: (mrow(bi), 0, k))
    return pl.pallas_call(
        _merge_kernel,
        grid=(b, seq // tm),
        in_specs=[
            tok(D_MODEL), tok(FOURIER_W), tok(POOL_W), gate(0), gate(1), gate(2), tok(D_MODEL), modc(2),
            full(wao), full(wf), full(wp), full(ps), full(wo), modc(3), modc(4), full(g2n), full(wr_t),
        ],
        out_specs=[tok(D_MODEL), tok(D_MODEL), pl.BlockSpec((1, N_EXPERTS, tm), lambda bi, i: (bi, 0, i))],
        out_shape=[
            jax.ShapeDtypeStruct((b, seq, D_MODEL), F32),
            jax.ShapeDtypeStruct((b, seq, D_MODEL), BF16),
            jax.ShapeDtypeStruct((b, N_EXPERTS, seq), F32),
        ],
        compiler_params=_cparams(("parallel", "parallel"), 48),
        name="merge",
    )(att, four, pooled, pb3, pb3, pb3, x3, mod3, wao, wf, wp, ps, wo, mod3, mod3, g2n, wr_t)


def _exclusive_prefix(mask, tri):
    e, seq = mask.shape
    ones = jnp.where(mask, 1.0, 0.0)
    offs = jnp.zeros((e, 1), F32)
    pieces = []
    for k in range(seq // LANE):
        blk = ones[:, k * LANE:(k + 1) * LANE]
        local = jnp.dot(blk.astype(BF16), tri, preferred_element_type=F32)
        pieces.append(local + offs)
        offs = offs + jnp.sum(blk, axis=1, keepdims=True)
    return jnp.concatenate(pieces, axis=1)


META_TSTART, META_TEND, META_PSTART, META_PEND, META_W = 0, 4, 8, 12, 16
GATHER_SLOT_BLOCK = 128
GATHER_WIN = 1408
GATHER_ALIGN = 128
COMBINE_TILE = 1024
COMBINE_WIN = 192
COMBINE_ALIGN = 16


def _select_kernel(lg_ref, slot_ref, aff_ref, meta_ref, *, cap):
    z = lg_ref[0]
    z = z - jnp.max(z, axis=0, keepdims=True)
    ez = jnp.exp(z)
    a = ez / jnp.sum(ez, axis=0, keepdims=True)
    aff_ref[0] = a
    capf = float(cap)

    def count_ge(th):
        return jnp.sum(jnp.where(a >= th, 1.0, 0.0), axis=1, keepdims=True)

    def bisect(_, lohi):
        lo, hi = lohi
        q2 = (lo + hi) * 0.5
        q1 = (lo + q2) * 0.5
        q3 = (q2 + hi) * 0.5
        g1, g2, g3 = count_ge(q1) >= capf, count_ge(q2) >= capf, count_ge(q3) >= capf
        new_lo = jnp.where(g3, q3, jnp.where(g2, q2, jnp.where(g1, q1, lo)))
        new_hi = jnp.where(g3, hi, jnp.where(g2, q3, jnp.where(g1, q2, q1)))
        return new_lo, new_hi

    e = a.shape[0]
    lo, hi = lax.fori_loop(0, SELECT_ITERS, bisect,
                           (jnp.zeros((e, 1), F32), jnp.full((e, 1), 2.0, F32)))
    r_i = lax.broadcasted_iota(I32, (LANE, LANE), 0)
    c_i = lax.broadcasted_iota(I32, (LANE, LANE), 1)
    tri = jnp.where(r_i < c_i, 1.0, 0.0).astype(BF16)
    above = a >= hi
    n_above = jnp.sum(jnp.where(above, 1.0, 0.0), axis=1, keepdims=True)
    tied = (a >= lo) & jnp.logical_not(above)
    tie_rank = _exclusive_prefix(tied, tri)
    sel = above | (tied & (tie_rank < capf - n_above))
    pos = _exclusive_prefix(sel, tri)
    slot_ref[0] = jnp.where(sel, pos, -1.0).astype(I32)

    seq = a.shape[1]
    t = lax.broadcasted_iota(I32, (e, seq), 1).astype(F32)
    lane = lax.broadcasted_iota(I32, (e, LANE), 1)
    meta = jnp.zeros((e, LANE), F32)
    sb = min(GATHER_SLOT_BLOCK, cap)
    for s in range(cap // sb):
        first = jnp.min(jnp.where(sel & (pos >= float(s * sb)), t, float(seq)), axis=1, keepdims=True)
        last = jnp.max(jnp.where(sel & (pos < float((s + 1) * sb)), t, -1.0), axis=1, keepdims=True)
        meta = jnp.where(lane == META_TSTART + s, first, meta)
        meta = jnp.where(lane == META_TEND + s, last, meta)
    tt = min(COMBINE_TILE, seq)
    for i in range(seq // tt):
        before = jnp.sum(jnp.where(sel & (t < float(i * tt)), 1.0, 0.0), axis=1, keepdims=True)
        upto = jnp.sum(jnp.where(sel & (t < float((i + 1) * tt)), 1.0, 0.0), axis=1, keepdims=True)
        meta = jnp.where(lane == META_PSTART + i, before, meta)
        meta = jnp.where(lane == META_PEND + i, upto, meta)
    meta_ref[0] = meta.astype(I32)


def _select(logits_t, cap):
    b, e, seq = logits_t.shape
    assert cap // min(GATHER_SLOT_BLOCK, cap) <= 4 and seq // min(COMBINE_TILE, seq) <= 4
    spec = pl.BlockSpec((1, e, seq), lambda bi: (bi, 0, 0))
    mspec = pl.BlockSpec((1, e, LANE), lambda bi: (bi, 0, 0))
    return pl.pallas_call(
        functools.partial(_select_kernel, cap=cap),
        grid=(b,),
        in_specs=[spec],
        out_specs=[spec, spec, mspec],
        out_shape=[jax.ShapeDtypeStruct((b, e, seq), I32), jax.ShapeDtypeStruct((b, e, seq), F32),
                   jax.ShapeDtypeStruct((b, e, LANE), I32)],
        compiler_params=_cparams(("parallel",), 32),
        name="select",
    )(logits_t)


def _align_down(v, align):
    shift = align.bit_length() - 1
    return lax.shift_left(lax.shift_right_logical(v, shift), shift)


def _gather_kernel(meta_ref, h_ref, slot_ref, aff_ref, xg_ref, gate_ref, *, cap, sb, win):
    seq = h_ref.shape[1]
    base = (pl.program_id(0) * N_EXPERTS + pl.program_id(1)) * META_W

    n_blocks = cap // sb

    def gather_blocks(starts, width):
        hits = []
        for s in range(n_blocks):
            j = lax.broadcasted_iota(I32, (sb, width), 0) + s * sb
            hits.append(slot_ref[0, 0, :, pl.ds(starts[s], width)] == j)
        for s in range(n_blocks):
            tok = pl.ds(starts[s], width)
            rows = slice(s * sb, (s + 1) * sb)
            onehot = jnp.where(hits[s], 1.0, 0.0).astype(BF16)
            xg_ref[0, 0, rows, :] = jnp.dot(onehot, h_ref[0, tok, :], preferred_element_type=F32).astype(BF16)
            gate_ref[0, 0, rows, :] = jnp.sum(jnp.where(hits[s], aff_ref[0, 0, :, tok], 0.0), axis=1, keepdims=True)

    if win >= seq:
        gather_blocks([0] * n_blocks, seq)
        return
    starts = []
    fits = None
    for s in range(n_blocks):
        first = meta_ref[base + META_TSTART + s]
        last = meta_ref[base + META_TEND + s]
        start = pl.multiple_of(jnp.minimum(_align_down(first, GATHER_ALIGN), seq - win), GATHER_ALIGN)
        starts.append(start)
        ok = last < start + win
        fits = ok if fits is None else jnp.logical_and(fits, ok)

    @pl.when(fits)
    def _():
        gather_blocks(starts, win)

    @pl.when(jnp.logical_not(fits))
    def _():
        gather_blocks([0] * n_blocks, seq)


def _gather(meta, h2, slots4, aff4, cap):
    b, seq, _ = h2.shape
    e = slots4.shape[1]
    sb = min(GATHER_SLOT_BLOCK, cap)
    win = GATHER_WIN if seq > GATHER_WIN else seq
    assert (seq - win) % GATHER_ALIGN == 0
    row = pl.BlockSpec((1, 1, 1, seq), lambda bi, ei, m: (bi, ei, 0, 0))
    return pl.pallas_call(
        functools.partial(_gather_kernel, cap=cap, sb=sb, win=win),
        grid_spec=pltpu.PrefetchScalarGridSpec(
            num_scalar_prefetch=1,
            grid=(b, e),
            in_specs=[pl.BlockSpec((1, seq, D_MODEL), lambda bi, ei, m: (bi, 0, 0)), row, row],
            out_specs=[
                pl.BlockSpec((1, 1, cap, D_MODEL), lambda bi, ei, m: (ei, bi, 0, 0)),
                pl.BlockSpec((1, 1, cap, 1), lambda bi, ei, m: (ei, bi, 0, 0)),
            ],
        ),
        out_shape=[
            jax.ShapeDtypeStruct((e, b, cap, D_MODEL), BF16),
            jax.ShapeDtypeStruct((e, b, cap, 1), F32),
        ],
        compiler_params=_cparams(("parallel", "arbitrary"), 48),
        name="gather",
    )(meta, h2, slots4, aff4)


FF_CHUNK = 256
FFN_ROW_TILES = 2


def _ffn_kernel(*refs, with_ctx):
    if with_ctx:
        x_ref, gate_ref, xc_ref, gatec_ref, wg_ref, wu_ref, wd_ref, o_ref, oc_ref, acc_ref, accc_ref = refs
    else:
        x_ref, gate_ref, wg_ref, wu_ref, wd_ref, o_ref, acc_ref = refs
    fc = pl.program_id(2)
    last_fc = pl.num_programs(2) - 1

    def swiglu_chunk(streams):
        @pl.when(fc == 0)
        def _():
            for _, _, _, acc in streams:
                acc[...] = jnp.zeros_like(acc)

        wg = wg_ref[0, 0].astype(BF16)
        wu = wu_ref[0, 0].astype(BF16)
        wd = wd_ref[0, 0].astype(BF16)
        xs = [x[0] for x, _, _, _ in streams]
        gates_ = [jnp.dot(x, wg, preferred_element_type=F32) for x in xs]
        ups = [jnp.dot(x, wu, preferred_element_type=F32) for x in xs]
        hmids = [(a * jax.nn.sigmoid(a) * u).astype(BF16) for a, u in zip(gates_, ups)]
        for (_, _, _, acc), hmid in zip(streams, hmids):
            acc[...] += jnp.dot(hmid, wd, preferred_element_type=F32)

        @pl.when(fc == last_fc)
        def _():
            for _, gate, o, acc in streams:
                o[0] = (acc[...] * gate[0]).astype(BF16)

    main = (x_ref, gate_ref, o_ref, acc_ref)
    if not with_ctx:
        swiglu_chunk([main])
        return
    on_last_tile = pl.program_id(1) == pl.num_programs(1) - 1

    @pl.when(jnp.logical_not(on_last_tile))
    def _():
        swiglu_chunk([main])

    @pl.when(on_last_tile)
    def _():
        swiglu_chunk([main, (xc_ref, gatec_ref, oc_ref, accc_ref)])


def _ffn(xg3, gate3, xc3, gatec3, wg_all, wu_all, wd_all, layer):
    e, m, _ = xg3.shape
    tm = m // FFN_ROW_TILES
    assert m % tm == 0 and tm % 16 == 0
    with_ctx = xc3 is not None
    row = lambda w: pl.BlockSpec((1, tm, w), lambda ei, i, fc: (ei, i, 0))
    in_specs = [row(D_MODEL), row(1)]
    out_specs = [row(D_MODEL)]
    out_shape = [jax.ShapeDtypeStruct((e, m, D_MODEL), BF16)]
    scratch = [pltpu.VMEM((tm, D_MODEL), F32)]
    args = [xg3, gate3]
    if with_ctx:
        mc = xc3.shape[1]
        crow = lambda w: pl.BlockSpec((1, mc, w), lambda ei, i, fc: (ei, 0, 0))
        in_specs += [crow(D_MODEL), crow(1)]
        out_specs.append(crow(D_MODEL))
        out_shape.append(jax.ShapeDtypeStruct((e, mc, D_MODEL), BF16))
        scratch.append(pltpu.VMEM((mc, D_MODEL), F32))
        args += [xc3, gatec3]
    in_specs += [
        pl.BlockSpec((1, 1, D_MODEL, FF_CHUNK), lambda ei, i, fc: (layer, ei, 0, fc)),
        pl.BlockSpec((1, 1, D_MODEL, FF_CHUNK), lambda ei, i, fc: (layer, ei, 0, fc)),
        pl.BlockSpec((1, 1, FF_CHUNK, D_MODEL), lambda ei, i, fc: (layer, ei, fc, 0)),
    ]
    res = pl.pallas_call(
        functools.partial(_ffn_kernel, with_ctx=with_ctx),
        grid=(e, m // tm, EXPERT_FF // FF_CHUNK),
        in_specs=in_specs,
        out_specs=out_specs,
        out_shape=out_shape,
        scratch_shapes=scratch,
        compiler_params=_cparams(("parallel", "arbitrary", "arbitrary"), 56),
        name="ffn",
    )(*args, wg_all, wu_all, wd_all)
    return (res[0], res[1]) if with_ctx else (res[0], None)


COMBINE_EXPERTS = 4


def _combine_kernel(meta_ref, slot_ref, gy_ref, x_ref, g2_ref, gf_ref, o_ref, acc_ref, *, cap, win, final):
    bi = pl.program_id(0)
    ti = pl.program_id(1)
    ec = pl.program_id(2)
    tt = x_ref.shape[1]
    tn = (((0,), (0,)), ((), ()))

    @pl.when(ec == 0)
    def _():
        acc_ref[...] = jnp.zeros_like(acc_ref)

    def scatter(starts, width):
        j = lax.broadcasted_iota(I32, (width, tt), 0)
        onehot = jnp.concatenate(
            [jnp.where(slot_ref[0, k] == j + starts[k], 1.0, 0.0).astype(BF16) for k in range(COMBINE_EXPERTS)],
            axis=0)
        gy = jnp.concatenate([gy_ref[k, pl.ds(starts[k], width), :] for k in range(COMBINE_EXPERTS)], axis=0)
        acc_ref[...] += lax.dot_general(onehot, gy, tn, preferred_element_type=F32)

    if win >= cap:
        scatter([0] * COMBINE_EXPERTS, cap)
    else:
        starts = []
        fits = None
        for k in range(COMBINE_EXPERTS):
            base = (bi * N_EXPERTS + ec * COMBINE_EXPERTS + k) * META_W
            before = meta_ref[base + META_PSTART + ti]
            upto = meta_ref[base + META_PEND + ti]
            start = pl.multiple_of(jnp.minimum(_align_down(before, COMBINE_ALIGN), cap - win), COMBINE_ALIGN)
            starts.append(start)
            ok = upto <= start + win
            fits = ok if fits is None else jnp.logical_and(fits, ok)

        @pl.when(fits)
        def _():
            scatter(starts, win)

        @pl.when(jnp.logical_not(fits))
        def _():
            scatter([0] * COMBINE_EXPERTS, cap)

    @pl.when(ec == pl.num_programs(2) - 1)
    def _():
        xn = x_ref[0] + g2_ref[0] * acc_ref[...]
        if final:
            ms = jnp.mean(xn * xn, axis=-1, keepdims=True)
            xn = (xn * lax.rsqrt(ms + RMS_EPS)) * gf_ref[...]
        o_ref[0] = xn


def _combine(meta, slots4, gy3, row_off, x3, mod3, gfinal, *, cap, ctx_row, final):
    b, seq, _ = x3.shape
    e = slots4.shape[1]
    tt = min(seq, COMBINE_TILE)
    win = COMBINE_WIN if cap > COMBINE_WIN else cap
    assert row_off % cap == 0 and (cap - win) % COMBINE_ALIGN == 0
    blk_off = row_off // cap
    mrow = (lambda bi: bi) if ctx_row is None else (lambda bi: ctx_row)
    return pl.pallas_call(
        functools.partial(_combine_kernel, cap=cap, win=win, final=final),
        grid_spec=pltpu.PrefetchScalarGridSpec(
            num_scalar_prefetch=1,
            grid=(b, seq // tt, e // COMBINE_EXPERTS),
            in_specs=[
                pl.BlockSpec((1, COMBINE_EXPERTS, 1, tt), lambda bi, i, ec, m: (bi, ec, 0, i)),
                pl.BlockSpec((COMBINE_EXPERTS, cap, D_MODEL), lambda bi, i, ec, m: (ec, blk_off + bi, 0)),
                pl.BlockSpec((1, tt, D_MODEL), lambda bi, i, ec, m: (bi, i, 0)),
                pl.BlockSpec((1, 1, D_MODEL), lambda bi, i, ec, m: (mrow(bi), 0, 5)),
                pl.BlockSpec((1, D_MODEL), lambda bi, i, ec, m: (0, 0)),
            ],
            out_specs=pl.BlockSpec((1, tt, D_MODEL), lambda bi, i, ec, m: (bi, i, 0)),
            scratch_shapes=[pltpu.VMEM((tt, D_MODEL), F32)],
        ),
        out_shape=jax.ShapeDtypeStruct((b, seq, D_MODEL), F32),
        compiler_params=_cparams(("parallel", "parallel", "arbitrary"), 48),
        name="combine",
    )(meta, slots4, gy3, x3, mod3, gfinal)


def _moe_route(h2, logits_t):
    b, seq, _ = h2.shape
    cap = EC_CAPACITY_FACTOR * seq // N_EXPERTS
    slots, aff, meta = _select(logits_t, cap)
    meta = meta[:, :, :META_W].reshape(-1)
    slots4 = slots.reshape(b, N_EXPERTS, 1, seq)
    aff4 = aff.reshape(b, N_EXPERTS, 1, seq)
    xg, gate = _gather(meta, h2, slots4, aff4, cap)
    return meta, slots4, xg.reshape(N_EXPERTS, b * cap, D_MODEL), gate.reshape(N_EXPERTS, b * cap, 1), cap


def kernel(x, c, ctx, c_ctx, ada_w, ada_b, norm1_g, norm2_g, w_in, rpb, w_att_o, w_fourier, w_pool,
           pool_scale, w_out, w_router, w_exp_gate, w_exp_up, w_exp_down, final_norm_g):
    b, seq, d = x.shape
    lc = ctx.shape[1]
    assert d == D_MODEL and seq % (GRID_W * ATT_QROWS) == 0 and b + 1 <= MOD_ROWS
    rows = seq // GRID_W
    ctx_row = b

    cond = jnp.concatenate([c, c_ctx[None, :], jnp.zeros((MOD_ROWS - b - 1, d), F32)], axis=0)
    cl, sl = _dft_mats(seq, seq // 2)
    clc, slc = _dft_mats(lc, lc // 2)
    cc, sc = _dft_mats(FOURIER_GROUP_W, FOURIER_GROUP_W)
    gfinal = final_norm_g.reshape(1, d)

    for i in range(DEPTH):
        update_ctx = i < DEPTH - 1
        mod3 = _adaln(cond, ada_w[i], ada_b[i]).reshape(MOD_ROWS, 1, 6 * d)
        g1n = norm1_g[i].reshape(1, d)
        g2n = norm2_g[i].reshape(1, d)
        w_in_b = w_in[i].astype(BF16)
        wao = w_att_o[i].astype(BF16)
        wf = w_fourier[i].astype(BF16)
        wp = w_pool[i].astype(BF16)
        ps = pool_scale[i].reshape(1, d)
        wo = w_out[i].astype(BF16)
        wr_t = jnp.pad(jnp.concatenate([w_router[i], w_router[i]], axis=1), ((0, 0), (0, LANE - 2 * N_EXPERTS)))
        bias = _bias_table(rpb[i], rows)

        pb, pp = _modproj(x.reshape(b * seq, d), mod3, g1n, w_in_b, seq=seq, ctx_row=None,
                          n_out=IN_W, with_pool=True)
        n_ctx = IN_W if update_ctx else Q_OFF
        pcb, pcp = _modproj(ctx.reshape(b * lc, d), mod3, g1n, w_in_b[:, :n_ctx], seq=lc, ctx_row=ctx_row,
                            n_out=n_ctx, with_pool=update_ctx)
        pb3 = pb.reshape(b, seq, IN_W)
        pcb3 = pcb.reshape(b, lc, n_ctx)

        att = _nattn(pb3, pcb3, bias)
        four = _fourier(pb3, cc, sc, cl, sl)
        pooled = _pool(pp.reshape(b, seq, POOL_W))
        x, h2, logits_t = _merge(att, four, pooled, pb3, x, mod3, wao, wf, wp, ps, wo, g2n, wr_t, ctx_row=None)
        meta_x, slots_x, xg, gate, cap = _moe_route(h2, logits_t)

        if update_ctx:
            att_c = _cattn(pcb3)
            four_c = _fourier(pcb3, cc, sc, clc, slc)
            pooled_c = _pool(pcp.reshape(b, lc, POOL_W))
            ctx, h2_c, logits_c = _merge(att_c, four_c, pooled_c, pcb3, ctx, mod3, wao, wf, wp, ps, wo, g2n, wr_t,
                                         ctx_row=ctx_row)
            meta_c, slots_c, xg_c, gate_c, cap_c = _moe_route(h2_c, logits_c)
        else:
            xg_c = gate_c = None

        gy, gy_c = _ffn(xg, gate, xg_c, gate_c, w_exp_gate, w_exp_up, w_exp_down, i)
        if update_ctx:
            ctx = _combine(meta_c, slots_c, gy_c, 0, ctx, mod3, gfinal, cap=cap_c, ctx_row=ctx_row, final=False)
        x = _combine(meta_x, slots_x, gy, 0, x, mod3, gfinal, cap=cap, ctx_row=None, final=not update_ctx)
    return x
```

```python
import functools
import math

import jax
import jax.numpy as jnp
from jax import lax
from jax.experimental import pallas as pl
from jax.experimental.pallas import tpu as pltpu

F32 = jnp.float32
BF16 = jnp.bfloat16
I32 = jnp.int32
HIGHEST = lax.Precision.HIGHEST

D_MODEL = 1024
DEPTH = 2
GRID_W = 64
N_HEADS = 16
HEAD_DIM = 64
WIN_R = 8
WIN_C = 16
FOURIER_GROUPS = 4
FOURIER_GROUP_W = 128
FOURIER_W = 512
POOL_WINDOWS = (2, 4, 8, 16)
POOL_GROUP_W = 128
POOL_W = 512
POOL_OUT_GROUP = 256
K_OFF, V_OFF, Q_OFF, F_OFF, P_OFF, G_OFF = 0, 1024, 2048, 3072, 3584, 4096
IN_W = 7168
N_EXPERTS = 16
EC_CAPACITY_FACTOR = 2
EXPERT_FF = 2816
RMS_EPS = 1e-6

LANE = 128
HEAD_PAIR_W = 2 * HEAD_DIM
N_HEAD_PAIRS = N_HEADS // 2
MOD_ROWS = 16
NEG_BIG = -1e30
ATT_QROWS = 2
ATT_KROWS = 10
ATT_TOP = (WIN_R // 2 + ATT_QROWS - 1) // ATT_QROWS
ATT_BOT = (WIN_R // 2 - 1 + ATT_QROWS - 1) // ATT_QROWS
ATT_CLASSES = ATT_TOP + 1 + ATT_BOT
ATT_ALIGN = 64
SELECT_ITERS = 32
MiB = 1024 * 1024


def _cparams(sem, vmem_mib):
    return pltpu.CompilerParams(dimension_semantics=sem, vmem_limit_bytes=vmem_mib * MiB)


def _adaln_kernel(c_ref, w_ref, b_ref, o_ref):
    c = c_ref[...]
    s = c * jax.nn.sigmoid(c)
    o_ref[...] = jnp.dot(s, w_ref[...], precision=HIGHEST, preferred_element_type=F32) + b_ref[...]


def _adaln(cond_rows, ada_w, ada_b):
    n = ada_w.shape[1]
    tn = 1024
    return pl.pallas_call(
        _adaln_kernel,
        grid=(n // tn,),
        in_specs=[
            pl.BlockSpec((MOD_ROWS, D_MODEL), lambda j: (0, 0)),
            pl.BlockSpec((D_MODEL, tn), lambda j: (0, j)),
            pl.BlockSpec((1, tn), lambda j: (0, j)),
        ],
        out_specs=pl.BlockSpec((MOD_ROWS, tn), lambda j: (0, j)),
        out_shape=jax.ShapeDtypeStruct((MOD_ROWS, n), F32),
        compiler_params=_cparams(("arbitrary",), 32),
        name="adaln",
    )(cond_rows, ada_w, ada_b.reshape(1, n))


def _modulate(x, g, shift, scale):
    ms = jnp.mean(x * x, axis=-1, keepdims=True)
    y = x * lax.rsqrt(ms + RMS_EPS)
    return (y * g) * (1.0 + scale) + shift


MODPROJ_TN = 1792


def _modproj_kernel(x_ref, sh_ref, sc_ref, g_ref, w_ref, *rest, pool_tile):
    if pool_tile is None:
        o_ref, h_ref = rest
        pp_ref = None
    else:
        o_ref, pp_ref, h_ref = rest
    j = pl.program_id(1)

    @pl.when(j == 0)
    def _():
        h = _modulate(x_ref[...], g_ref[...], sh_ref[0], sc_ref[0])
        h_ref[...] = h.astype(BF16)

    acc = jnp.dot(h_ref[...], w_ref[...], preferred_element_type=F32)
    o_ref[...] = acc.astype(BF16)
    if pool_tile is not None:
        @pl.when(j == pool_tile)
        def _():
            lo = P_OFF % acc.shape[1]
            pp_ref[...] = acc[:, lo:lo + POOL_W]


def _modproj(x2, mod3, gain, w, *, seq, ctx_row, n_out, with_pool):
    rows = x2.shape[0]
    tn = MODPROJ_TN if n_out % MODPROJ_TN == 0 else 1024
    if ctx_row is None:
        tm = min(seq, 1024)
        tiles_per_seq = seq // tm
        mrow = lambda i: i // tiles_per_seq
    else:
        tm = min(rows, 1024)
        mrow = lambda i: ctx_row
    assert rows % tm == 0 and n_out % tn == 0 and (not with_pool or P_OFF % tn + POOL_W <= tn)
    pool_tile = (P_OFF // tn) if with_pool else None
    out_shape = [jax.ShapeDtypeStruct((rows, n_out), BF16)]
    out_specs = [pl.BlockSpec((tm, tn), lambda i, j: (i, j))]
    if with_pool:
        out_shape.append(jax.ShapeDtypeStruct((rows, POOL_W), F32))
        out_specs.append(pl.BlockSpec((tm, POOL_W), lambda i, j: (i, 0)))
    res = pl.pallas_call(
        functools.partial(_modproj_kernel, pool_tile=pool_tile),
        grid=(rows // tm, n_out // tn),
        in_specs=[
            pl.BlockSpec((tm, D_MODEL), lambda i, j: (i, 0)),
            pl.BlockSpec((1, 1, D_MODEL), lambda i, j: (mrow(i), 0, 0)),
            pl.BlockSpec((1, 1, D_MODEL), lambda i, j: (mrow(i), 0, 1)),
            pl.BlockSpec((1, D_MODEL), lambda i, j: (0, 0)),
            pl.BlockSpec((D_MODEL, tn), lambda i, j: (0, j)),
        ],
        out_specs=out_specs,
        out_shape=out_shape,
        scratch_shapes=[pltpu.VMEM((tm, D_MODEL), BF16)],
        compiler_params=_cparams(("parallel", "arbitrary"), 48),
        name="modproj",
    )(x2, mod3, mod3, gain, w)
    return res if with_pool else (res[0], None)


RPB_ROWS = 16


def _clip(v, lo, hi):
    return max(lo, min(v, hi))


def _bias_kernel(rpb_ref, o_ref, *, rows):
    qc = lax.broadcasted_iota(I32, (GRID_W, LANE), 0)
    lane = lax.broadcasted_iota(I32, (GRID_W, LANE), 1)
    kc = lane & (GRID_W - 1)
    first_half = lane < GRID_W
    cs = jnp.clip(qc - WIN_C // 2, 0, GRID_W - WIN_C)
    col_valid = (kc >= cs) & (kc < cs + WIN_C)
    n_roff = 2 * WIN_R - 1
    shift_a = LANE - (WIN_C - 1)
    shift_b = GRID_W - (WIN_C - 1)
    n_steps = rows // ATT_QROWS
    for cls in range(ATT_CLASSES):
        i_rep = cls if cls <= ATT_TOP else cls - (ATT_TOP + 1) + (n_steps - ATT_BOT)
        s = _clip(ATT_QROWS * i_rep - WIN_R // 2, 0, rows - ATT_KROWS)
        for ri in range(ATT_QROWS):
            r = ATT_QROWS * i_rep + ri
            rs = _clip(r - WIN_R // 2, 0, rows - WIN_R)
            for m in range(ATT_KROWS // 2):
                krow_a = s + 2 * m
                krow_b = krow_a + 1
                va = rs <= krow_a < rs + WIN_R
                vb = rs <= krow_b < rs + WIN_R
                ro_a = _clip(krow_a - r + WIN_R - 1, 0, n_roff - 1)
                ro_b = _clip(krow_b - r + WIN_R - 1, 0, n_roff - 1)
                row_a = jnp.broadcast_to(rpb_ref[0, ro_a:ro_a + 1, :], (GRID_W, LANE))
                row_b = jnp.broadcast_to(rpb_ref[0, ro_b:ro_b + 1, :], (GRID_W, LANE))
                acc = jnp.where(first_half,
                                pltpu.roll(row_a, shift_a, 1, stride=1, stride_axis=0),
                                pltpu.roll(row_b, shift_b, 1, stride=1, stride_axis=0))
                valid = col_valid
                if not va:
                    valid = valid & jnp.logical_not(first_half)
                if not vb:
                    valid = valid & first_half
                tile = jnp.where(valid, acc, NEG_BIG)
                o_ref[0, cls, ri * GRID_W:(ri + 1) * GRID_W, m * LANE:(m + 1) * LANE] = tile


def _bias_table(rpb, rows):
    nq = ATT_QROWS * GRID_W
    nk = ATT_KROWS * GRID_W
    n_roff, n_coff = rpb.shape[1], rpb.shape[2]
    assert n_roff <= RPB_ROWS and n_coff <= LANE
    rpb_p = jnp.pad(rpb, ((0, 0), (0, RPB_ROWS - n_roff), (0, LANE - n_coff)))
    return pl.pallas_call(
        functools.partial(_bias_kernel, rows=rows),
        grid=(N_HEADS,),
        in_specs=[pl.BlockSpec((1, RPB_ROWS, LANE), lambda h: (h, 0, 0))],
        out_specs=pl.BlockSpec((1, ATT_CLASSES, nq, nk), lambda h: (h, 0, 0, 0)),
        out_shape=jax.ShapeDtypeStruct((N_HEADS, ATT_CLASSES, nq, nk), F32),
        compiler_params=_cparams(("parallel",), 32),
        name="bias_table",
    )(rpb_p)


_NT = (((1,), (1,)), ((), ()))


def _scores(qm, kw, kc, bias):
    sw = lax.dot_general(qm, kw, _NT, preferred_element_type=F32)
    if bias is not None:
        sw = sw + bias
    sc = lax.dot_general(qm, kc, _NT, preferred_element_type=F32) if kc is not None else None
    return sw, sc


def _probs(sw, sc):
    m = jnp.max(sw, axis=-1, keepdims=True)
    if sc is not None:
        m = jnp.maximum(m, jnp.max(sc, axis=-1, keepdims=True))
    pw = jnp.exp(sw - m)
    l = jnp.sum(pw, axis=-1, keepdims=True)
    pc = None
    if sc is not None:
        pc = jnp.exp(sc - m)
        l = l + jnp.sum(pc, axis=-1, keepdims=True)
        pc = pc.astype(BF16)
    return pw.astype(BF16), pc, l


def _pv(pw, pc, l, vw, vc):
    o = jnp.dot(pw, vw, preferred_element_type=F32)
    if pc is not None:
        o = o + jnp.dot(pc, vc, preferred_element_type=F32)
    return o / l


ATT_UNROLL = 4


ATT_KEY_TILE = 256


def _key_tiles(n):
    return [(off, min(ATT_KEY_TILE, n - off)) for off in range(0, n, ATT_KEY_TILE)]


def _lane_fold(x, op):
    out = x[:, :LANE]
    for c in range(1, x.shape[1] // LANE):
        out = op(out, x[:, c * LANE:(c + 1) * LANE])
    return out


def _nattn_kernel(k_ref, v_ref, q_ref, kc_ref, vc_ref, b_ref, o_ref, s_ref, m_ref, *, rows):
    nq = ATT_QROWS * GRID_W
    nk = ATT_KROWS * GRID_W
    lc = kc_ref.shape[1]
    n_steps = rows // ATT_QROWS
    lane = lax.broadcasted_iota(I32, (nq, HEAD_PAIR_W), 1)
    second = lane >= HEAD_DIM
    scale = HEAD_DIM ** -0.5
    win_tiles = _key_tiles(nk)
    ctx_tiles = _key_tiles(lc)

    ones_cols = jnp.ones((ATT_KEY_TILE, HEAD_PAIR_W), BF16)

    def with_ones(v):
        return jnp.concatenate([v, ones_cols[:v.shape[0]]], axis=1)

    def geometry(ii, u):
        i = ii * ATT_UNROLL + u
        s = jnp.clip(ATT_QROWS * i - WIN_R // 2, 0, rows - ATT_KROWS)
        kstart = pl.multiple_of(s * GRID_W, ATT_ALIGN)
        qstart = pl.multiple_of(i * nq, ATT_ALIGN)
        cls = jnp.where(i < ATT_TOP, i,
                        jnp.where(i < n_steps - ATT_BOT, ATT_TOP, i - (n_steps - ATT_BOT) + ATT_TOP + 1))
        return kstart, qstart, cls

    def pass1(ii, buf, u, hh):
        kstart, qstart, cls = geometry(ii, u)
        c = 2 * u + hh
        q2 = (q_ref[0, pl.ds(qstart, nq), :].astype(F32) * scale).astype(BF16)
        head_lanes = second if hh else jnp.logical_not(second)
        qm = jnp.where(head_lanes, q2, jnp.zeros_like(q2))
        m_run = None
        for off, width in win_tiles:
            kt = k_ref[0, pl.ds(kstart + off, width), :]
            st = lax.dot_general(qm, kt, _NT, preferred_element_type=F32) + b_ref[hh, cls, :, off:off + width]
            s_ref[buf, c, :, off:off + width] = st
            mt = _lane_fold(st, jnp.maximum)
            m_run = mt if m_run is None else jnp.maximum(m_run, mt)
        for off, width in ctx_tiles:
            st = lax.dot_general(qm, kc_ref[0, off:off + width, :], _NT, preferred_element_type=F32)
            s_ref[buf, c, :, nk + off:nk + off + width] = st
            m_run = jnp.maximum(m_run, _lane_fold(st, jnp.maximum))
        m_ref[buf, c] = jnp.max(m_run, axis=-1, keepdims=True)

    def pass2(ii, buf, u, hh):
        kstart, _, _ = geometry(ii, u)
        c = 2 * u + hh
        m = m_ref[buf, c]
        acc = None
        for off, width in win_tiles:
            pt = jnp.exp(s_ref[buf, c, :, off:off + width] - m).astype(BF16)
            pv = jnp.dot(pt, with_ones(v_ref[0, pl.ds(kstart + off, width), :]), preferred_element_type=F32)
            acc = pv if acc is None else acc + pv
        for off, width in ctx_tiles:
            pt = jnp.exp(s_ref[buf, c, :, nk + off:nk + off + width] - m).astype(BF16)
            acc = acc + jnp.dot(pt, with_ones(vc_ref[0, off:off + width, :]), preferred_element_type=F32)
        return acc[:, :HEAD_PAIR_W] / acc[:, HEAD_PAIR_W:]

    def store(ii, u, o0, o1):
        _, qstart, _ = geometry(ii, u)
        o_ref[0, pl.ds(qstart, nq), :] = jnp.where(second, o1, o0).astype(BF16)

    n_iter = n_steps // ATT_UNROLL
    for u in range(ATT_UNROLL):
        for hh in range(2):
            pass1(0, 0, u, hh)

    def overlapped(ii, new):
        for u in range(ATT_UNROLL):
            outs = []
            for hh in range(2):
                pass1(ii, new, u, hh)
                outs.append(pass2(ii - 1, 1 - new, u, hh))
            store(ii - 1, u, *outs)

    def body(jj, carry):
        overlapped(2 * jj + 1, 1)
        overlapped(2 * jj + 2, 0)
        return carry

    assert n_iter % 2 == 0
    lax.fori_loop(0, n_iter // 2 - 1, body, 0)
    last = n_iter - 1
    overlapped(last, 1)
    for u in range(ATT_UNROLL):
        store(last, u, *[pass2(last, 1, u, hh) for hh in range(2)])


def _nattn(pb3, pcb3, bias):
    b, seq, _ = pb3.shape
    lc = pcb3.shape[1]
    rows = seq // GRID_W
    nq = ATT_QROWS * GRID_W
    nk = ATT_KROWS * GRID_W
    kblk, vblk, qblk = K_OFF // LANE, V_OFF // LANE, Q_OFF // LANE
    return pl.pallas_call(
        functools.partial(_nattn_kernel, rows=rows),
        grid=(N_HEAD_PAIRS, b),
        in_specs=[
            pl.BlockSpec((1, seq, HEAD_PAIR_W), lambda hp, bi: (bi, 0, kblk + hp)),
            pl.BlockSpec((1, seq, HEAD_PAIR_W), lambda hp, bi: (bi, 0, vblk + hp)),
            pl.BlockSpec((1, seq, HEAD_PAIR_W), lambda hp, bi: (bi, 0, qblk + hp)),
            pl.BlockSpec((1, lc, HEAD_PAIR_W), lambda hp, bi: (bi, 0, kblk + hp)),
            pl.BlockSpec((1, lc, HEAD_PAIR_W), lambda hp, bi: (bi, 0, vblk + hp)),
            pl.BlockSpec((2, ATT_CLASSES, nq, nk), lambda hp, bi: (hp, 0, 0, 0)),
        ],
        out_specs=pl.BlockSpec((1, seq, HEAD_PAIR_W), lambda hp, bi: (bi, 0, hp)),
        out_shape=jax.ShapeDtypeStruct((b, seq, N_HEADS * HEAD_DIM), BF16),
        scratch_shapes=[pltpu.VMEM((2, 2 * ATT_UNROLL, nq, nk + lc), F32),
                        pltpu.VMEM((2, 2 * ATT_UNROLL, nq, 1), F32)],
        compiler_params=_cparams(("parallel", "parallel"), 40),
        name="nattn",
    )(pb3, pb3, pb3, pcb3, pcb3, bias)


def _cattn_kernel(k_ref, v_ref, q_ref, o_ref):
    lc = q_ref.shape[1]
    lane = lax.broadcasted_iota(I32, (lc, HEAD_PAIR_W), 1)
    second = lane >= HEAD_DIM
    chains = []
    for hp in range(N_HEAD_PAIRS):
        cols = slice(hp * HEAD_PAIR_W, (hp + 1) * HEAD_PAIR_W)
        q2 = (q_ref[0, :, cols].astype(F32) * (HEAD_DIM ** -0.5)).astype(BF16)
        for hh in range(2):
            head_lanes = second if hh else jnp.logical_not(second)
            qm = jnp.where(head_lanes, q2, jnp.zeros_like(q2))
            chains.append(_scores(qm, k_ref[0, :, cols], None, None))
    probs = [_probs(sw, sc) for sw, sc in chains]
    for hp in range(N_HEAD_PAIRS):
        cols = slice(hp * HEAD_PAIR_W, (hp + 1) * HEAD_PAIR_W)
        outs = [_pv(probs[2 * hp + hh][0], None, probs[2 * hp + hh][2], v_ref[0, :, cols], None) for hh in range(2)]
        o_ref[0, :, cols] = jnp.where(second, outs[1], outs[0]).astype(BF16)


def _cattn(pcb3):
    b, lc, _ = pcb3.shape
    width = N_HEADS * HEAD_DIM
    spec = lambda off: pl.BlockSpec((1, lc, width), lambda bi: (bi, 0, off // width))
    return pl.pallas_call(
        _cattn_kernel,
        grid=(b,),
        in_specs=[spec(K_OFF), spec(V_OFF), spec(Q_OFF)],
        out_specs=pl.BlockSpec((1, lc, width), lambda bi: (bi, 0, 0)),
        out_shape=jax.ShapeDtypeStruct((b, lc, width), BF16),
        compiler_params=_cparams(("parallel",), 32),
        name="cattn",
    )(pcb3, pcb3, pcb3)


def _dft_kernel(c_ref, s_ref, *, n):
    tk, ncols = c_ref.shape
    k = pl.program_id(0) * tk + lax.broadcasted_iota(I32, (tk, LANE), 0)
    lane = lax.broadcasted_iota(I32, (tk, LANE), 1)
    w = 2.0 * math.pi / n
    ang_p = ((k * lane) & (n - 1)).astype(F32) * w
    cp, sp = jnp.cos(ang_p), jnp.sin(ang_p)
    ang_q = ((k * (lane * LANE)) & (n - 1)).astype(F32) * w
    cq, sq = jnp.cos(ang_q), jnp.sin(ang_q)
    for q in range(ncols // LANE):
        cols = slice(q * LANE, (q + 1) * LANE)
        cqq, sqq = cq[:, q:q + 1], sq[:, q:q + 1]
        c_ref[:, cols] = (cqq * cp - sqq * sp).astype(BF16)
        s_ref[:, cols] = (sqq * cp + cqq * sp).astype(BF16)


def _dft_mats(n, ncols):
    tk = min(n, 256)
    return pl.pallas_call(
        functools.partial(_dft_kernel, n=n),
        grid=(n // tk,),
        in_specs=[],
        out_specs=[pl.BlockSpec((tk, ncols), lambda i: (i, 0))] * 2,
        out_shape=[jax.ShapeDtypeStruct((n, ncols), BF16)] * 2,
        compiler_params=_cparams(("parallel",), 48),
        name=f"dft_mats_{n}",
    )()


REV_BLOCK = 128


def _fourier_kernel(u_ref, cc_ref, sc_ref, cl_ref, sl_ref, o_ref, us_ref, ud_ref, a_ref, b_ref, ah_ref, *, seq):
    half = seq // 2
    nblk = seq // REV_BLOCK
    tk = o_ref.shape[1]

    @pl.when(pl.program_id(1) == 0)
    def _():
        d_i = lax.broadcasted_iota(I32, (REV_BLOCK, REV_BLOCK), 0)
        s_i = lax.broadcasted_iota(I32, (REV_BLOCK, REV_BLOCK), 1)
        flip = jnp.where((d_i >= 1) & (s_i == REV_BLOCK - d_i), 1.0, 0.0).astype(BF16)
        row = lax.broadcasted_iota(I32, (REV_BLOCK, FOURIER_W), 0)
        for blk in range(nblk // 2):
            lo = u_ref[0, blk * REV_BLOCK:(blk + 1) * REV_BLOCK, :].astype(F32)
            src = u_ref[0, (nblk - 1 - blk) * REV_BLOCK:(nblk - blk) * REV_BLOCK, :]
            rev = jnp.dot(flip, src, preferred_element_type=F32)
            if blk > 0:
                head = u_ref[0, (nblk - blk) * REV_BLOCK:(nblk - blk) * REV_BLOCK + 16, :].astype(F32)
                rev = jnp.where(row == 0, head[0:1, :], rev)
            rows = slice(blk * REV_BLOCK, (blk + 1) * REV_BLOCK)
            us_ref[rows, :] = (lo + rev).astype(BF16)
            ud_ref[rows, :] = (lo - rev).astype(BF16)
        mid = u_ref[0, half:half + 16, :]
        for g in range(FOURIER_GROUPS):
            sl = slice(g * FOURIER_GROUP_W, (g + 1) * FOURIER_GROUP_W)
            a_ref[:, sl] = jnp.dot(us_ref[:, sl], cc_ref[...], preferred_element_type=F32).astype(BF16)
            b_ref[:, sl] = jnp.dot(ud_ref[:, sl], sc_ref[...], preferred_element_type=F32).astype(BF16)
            ah_ref[:, sl] = jnp.dot(mid[:, sl], cc_ref[...], preferred_element_type=F32)

    k = pl.program_id(1) * tk + lax.broadcasted_iota(I32, (tk, 1), 0)
    sign = (1 - 2 * (k & 1)).astype(F32)
    y = (jnp.dot(cl_ref[...], a_ref[...], preferred_element_type=F32)
         - jnp.dot(sl_ref[...], b_ref[...], preferred_element_type=F32)
         + sign * ah_ref[0:1, :])
    o_ref[0] = (y * (1.0 / math.sqrt(seq * FOURIER_GROUP_W))).astype(BF16)


def _fourier(pb3, cc, sc, cl, sl):
    b, seq, _ = pb3.shape
    tk = min(seq, 512)
    half = seq // 2
    assert seq % (2 * REV_BLOCK) == 0
    return pl.pallas_call(
        functools.partial(_fourier_kernel, seq=seq),
        grid=(b, seq // tk),
        in_specs=[
            pl.BlockSpec((1, seq, FOURIER_W), lambda bi, k: (bi, 0, F_OFF // FOURIER_W)),
            pl.BlockSpec((FOURIER_GROUP_W, FOURIER_GROUP_W), lambda bi, k: (0, 0)),
            pl.BlockSpec((FOURIER_GROUP_W, FOURIER_GROUP_W), lambda bi, k: (0, 0)),
            pl.BlockSpec((tk, half), lambda bi, k: (k, 0)),
            pl.BlockSpec((tk, half), lambda bi, k: (k, 0)),
        ],
        out_specs=pl.BlockSpec((1, tk, FOURIER_W), lambda bi, k: (bi, k, 0)),
        out_shape=jax.ShapeDtypeStruct((b, seq, FOURIER_W), BF16),
        scratch_shapes=[pltpu.VMEM((half, FOURIER_W), BF16)] * 4 + [pltpu.VMEM((16, FOURIER_W), F32)],
        compiler_params=_cparams(("parallel", "arbitrary"), 48),
        name="fourier",
    )(pb3, cc, sc, cl, sl)


POOL_PAD = 8


def _pool_kernel(u_ref, o_ref, pad_ref, *, seq):
    t = lax.broadcasted_iota(I32, (seq, POOL_GROUP_W), 0)
    zeros = jnp.zeros((POOL_PAD, POOL_GROUP_W), F32)
    pad_ref[0:POOL_PAD, :] = zeros
    pad_ref[seq + POOL_PAD:seq + 2 * POOL_PAD, :] = zeros
    pad_ref[POOL_PAD:seq + POOL_PAD, :] = u_ref[0]
    for g, w in enumerate(POOL_WINDOWS):
        @pl.when(pl.program_id(1) == g)
        def _(w=w):
            acc = None
            for d in range(-(w // 2), w - w // 2):
                term = pad_ref[pl.ds(POOL_PAD + d, seq), :]
                acc = term if acc is None else acc + term
            cnt = (jnp.minimum(t + (w - w // 2), seq) - jnp.maximum(t - w // 2, 0)).astype(F32)
            o_ref[0] = (acc / cnt - u_ref[0]).astype(BF16)


def _pool(pp3):
    b, seq, _ = pp3.shape
    spec = pl.BlockSpec((1, seq, POOL_GROUP_W), lambda bi, g: (bi, 0, g))
    return pl.pallas_call(
        functools.partial(_pool_kernel, seq=seq),
        grid=(b, len(POOL_WINDOWS)),
        in_specs=[spec],
        out_specs=spec,
        out_shape=jax.ShapeDtypeStruct((b, seq, POOL_W), BF16),
        scratch_shapes=[pltpu.VMEM((seq + 2 * POOL_PAD, POOL_GROUP_W), F32)],
        compiler_params=_cparams(("parallel", "parallel"), 32),
        name="pool",
    )(pp3)


MERGE_PARTS = 4


def _merge_kernel(att_ref, four_ref, pool_ref, ga_ref, gf_ref, gp_ref, x_ref, g1_ref,
                  wao_ref, wf_ref, wp_ref, ps_ref, wo_ref, sh2_ref, sc2_ref, g2n_ref, wr_ref,
                  o_ref, h_ref, lg_ref):
    tm = x_ref.shape[1]
    n_parts = min(MERGE_PARTS, tm // LANE)
    parts = [slice(p * (tm // n_parts), (p + 1) * (tm // n_parts)) for p in range(n_parts)]
    wr = wr_ref[...]
    wr_hi = wr.astype(BF16)
    wr_lo = (wr - wr_hi.astype(F32)).astype(BF16)
    wr_split = jnp.where(lax.broadcasted_iota(I32, wr.shape, 1) < N_EXPERTS, wr_hi, wr_lo)

    def sigmoid(ref, rows):
        return 0.5 * jnp.tanh(0.5 * ref[0, rows, :].astype(F32)) + 0.5

    branches = []
    for rows in parts:
        y_att = jnp.dot(att_ref[0, rows, :], wao_ref[...], preferred_element_type=F32)
        y_four = jnp.dot(four_ref[0, rows, :], wf_ref[...], preferred_element_type=F32)
        pooled = pool_ref[0, rows, :]
        y_pool = jnp.concatenate(
            [jnp.dot(pooled[:, g * POOL_GROUP_W:(g + 1) * POOL_GROUP_W], wp_ref[g], preferred_element_type=F32)
             for g in range(len(POOL_WINDOWS))], axis=-1) * ps_ref[...]
        branches.append((y_att, y_four, y_pool))
    resid = []
    for rows, (y_att, y_four, y_pool) in zip(parts, branches):
        merged = (sigmoid(ga_ref, rows) * y_att + sigmoid(gf_ref, rows) * y_four + sigmoid(gp_ref, rows) * y_pool)
        y = jnp.dot(merged.astype(BF16), wo_ref[...], preferred_element_type=F32)
        xn = x_ref[0, rows, :] + g1_ref[0] * y
        o_ref[0, rows, :] = xn
        resid.append(xn)
    for rows, xn in zip(parts, resid):
        n = xn.shape[0]
        h = _modulate(xn, g2n_ref[...], sh2_ref[0], sc2_ref[0])
        h_hi = h.astype(BF16)
        h_ref[0, rows, :] = h_hi
        h_lo = (h - h_hi.astype(F32)).astype(BF16)
        r = jnp.dot(jnp.concatenate([h_hi, h_lo], axis=0), wr_split, preferred_element_type=F32)
        hi_t = r[:n].T
        lo_t = r[n:].T
        lg_ref[0, :, rows] = (hi_t[:N_EXPERTS] + hi_t[N_EXPERTS:2 * N_EXPERTS]
                              + lo_t[:N_EXPERTS] + lo_t[N_EXPERTS:2 * N_EXPERTS])


def _merge(att, four, pooled, pb3, x3, mod3, wao, wf, wp, ps, wo, g2n, wr_t, *, ctx_row):
    b, seq, _ = x3.shape
    tm = min(seq, 512)
    gblk = G_OFF // D_MODEL
    mrow = (lambda bi: bi) if ctx_row is None else (lambda bi: ctx_row)
    tok = lambda w: pl.BlockSpec((1, tm, w), lambda bi, i: (bi, i, 0))
    gate = lambda k: pl.BlockSpec((1, tm, D_MODEL), lambda bi, i: (bi, i, gblk + k))
    full = lambda a: pl.BlockSpec(a.shape, lambda bi, i: (0,) * a.ndim)
    modc = lambda k: pl.BlockSpec((1, 1, D_MODEL), lambda bi, i: (mrow(bi), 0, k))
    return pl.pallas_call(
        _merge_kernel,
        grid=(b, seq // tm),
        in_specs=[
            tok(D_MODEL), tok(FOURIER_W), tok(POOL_W), gate(0), gate(1), gate(2), tok(D_MODEL), modc(2),
            full(wao), full(wf), full(wp), full(ps), full(wo), modc(3), modc(4), full(g2n), full(wr_t),
        ],
        out_specs=[tok(D_MODEL), tok(D_MODEL), pl.BlockSpec((1, N_EXPERTS, tm), lambda bi, i: (bi, 0, i))],
        out_shape=[
            jax.ShapeDtypeStruct((b, seq, D_MODEL), F32),
            jax.ShapeDtypeStruct((b, seq, D_MODEL), BF16),
            jax.ShapeDtypeStruct((b, N_EXPERTS, seq), F32),
        ],
        compiler_params=_cparams(("parallel", "parallel"), 48),
        name="merge",
    )(att, four, pooled, pb3, pb3, pb3, x3, mod3, wao, wf, wp, ps, wo, mod3, mod3, g2n, wr_t)


def _exclusive_prefix(mask, tri):
    e, seq = mask.shape
    ones = jnp.where(mask, 1.0, 0.0)
    offs = jnp.zeros((e, 1), F32)
    pieces = []
    for k in range(seq // LANE):
        blk = ones[:, k * LANE:(k + 1) * LANE]
        local = jnp.dot(blk.astype(BF16), tri, preferred_element_type=F32)
        pieces.append(local + offs)
        offs = offs + jnp.sum(blk, axis=1, keepdims=True)
    return jnp.concatenate(pieces, axis=1)


META_TSTART, META_TEND, META_PSTART, META_PEND, META_W = 0, 4, 8, 12, 16
GATHER_SLOT_BLOCK = 128
GATHER_WIN = 1408
GATHER_ALIGN = 128
COMBINE_TILE = 1024
COMBINE_WIN = 192
COMBINE_ALIGN = 16


def _select_kernel(lg_ref, slot_ref, aff_ref, meta_ref, *, cap):
    z = lg_ref[0]
    z = z - jnp.max(z, axis=0, keepdims=True)
    ez = jnp.exp(z)
    a = ez / jnp.sum(ez, axis=0, keepdims=True)
    aff_ref[0] = a
    capf = float(cap)

    def count_ge(th):
        return jnp.sum(jnp.where(a >= th, 1.0, 0.0), axis=1, keepdims=True)

    def bisect(_, lohi):
        lo, hi = lohi
        q2 = (lo + hi) * 0.5
        q1 = (lo + q2) * 0.5
        q3 = (q2 + hi) * 0.5
        g1, g2, g3 = count_ge(q1) >= capf, count_ge(q2) >= capf, count_ge(q3) >= capf
        new_lo = jnp.where(g3, q3, jnp.where(g2, q2, jnp.where(g1, q1, lo)))
        new_hi = jnp.where(g3, hi, jnp.where(g2, q3, jnp.where(g1, q2, q1)))
        return new_lo, new_hi

    e = a.shape[0]
    lo, hi = lax.fori_loop(0, SELECT_ITERS, bisect,
                           (jnp.zeros((e, 1), F32), jnp.full((e, 1), 2.0, F32)))
    r_i = lax.broadcasted_iota(I32, (LANE, LANE), 0)
    c_i = lax.broadcasted_iota(I32, (LANE, LANE), 1)
    tri = jnp.where(r_i < c_i, 1.0, 0.0).astype(BF16)
    above = a >= hi
    n_above = jnp.sum(jnp.where(above, 1.0, 0.0), axis=1, keepdims=True)
    tied = (a >= lo) & jnp.logical_not(above)
    tie_rank = _exclusive_prefix(tied, tri)
    sel = above | (tied & (tie_rank < capf - n_above))
    pos = _exclusive_prefix(sel, tri)
    slot_ref[0] = jnp.where(sel, pos, -1.0).astype(I32)

    seq = a.shape[1]
    t = lax.broadcasted_iota(I32, (e, seq), 1).astype(F32)
    lane = lax.broadcasted_iota(I32, (e, LANE), 1)
    meta = jnp.zeros((e, LANE), F32)
    sb = min(GATHER_SLOT_BLOCK, cap)
    for s in range(cap // sb):
        first = jnp.min(jnp.where(sel & (pos >= float(s * sb)), t, float(seq)), axis=1, keepdims=True)
        last = jnp.max(jnp.where(sel & (pos < float((s + 1) * sb)), t, -1.0), axis=1, keepdims=True)
        meta = jnp.where(lane == META_TSTART + s, first, meta)
        meta = jnp.where(lane == META_TEND + s, last, meta)
    tt = min(COMBINE_TILE, seq)
    for i in range(seq // tt):
        before = jnp.sum(jnp.where(sel & (t < float(i * tt)), 1.0, 0.0), axis=1, keepdims=True)
        upto = jnp.sum(jnp.where(sel & (t < float((i + 1) * tt)), 1.0, 0.0), axis=1, keepdims=True)
        meta = jnp.where(lane == META_PSTART + i, before, meta)
        meta = jnp.where(lane == META_PEND + i, upto, meta)
    meta_ref[0] = meta.astype(I32)


def _select(logits_t, cap):
    b, e, seq = logits_t.shape
    assert cap // min(GATHER_SLOT_BLOCK, cap) <= 4 and seq // min(COMBINE_TILE, seq) <= 4
    spec = pl.BlockSpec((1, e, seq), lambda bi: (bi, 0, 0))
    mspec = pl.BlockSpec((1, e, LANE), lambda bi: (bi, 0, 0))
    return pl.pallas_call(
        functools.partial(_select_kernel, cap=cap),
        grid=(b,),
        in_specs=[spec],
        out_specs=[spec, spec, mspec],
        out_shape=[jax.ShapeDtypeStruct((b, e, seq), I32), jax.ShapeDtypeStruct((b, e, seq), F32),
                   jax.ShapeDtypeStruct((b, e, LANE), I32)],
        compiler_params=_cparams(("parallel",), 32),
        name="select",
    )(logits_t)


def _align_down(v, align):
    shift = align.bit_length() - 1
    return lax.shift_left(lax.shift_right_logical(v, shift), shift)


def _gather_kernel(meta_ref, h_ref, slot_ref, aff_ref, xg_ref, gate_ref, *, cap, sb, win):
    seq = h_ref.shape[1]
    base = (pl.program_id(0) * N_EXPERTS + pl.program_id(1)) * META_W

    n_blocks = cap // sb

    def gather_blocks(starts, width):
        hits = []
        for s in range(n_blocks):
            j = lax.broadcasted_iota(I32, (sb, width), 0) + s * sb
            hits.append(slot_ref[0, 0, :, pl.ds(starts[s], width)] == j)
        for s in range(n_blocks):
            tok = pl.ds(starts[s], width)
            rows = slice(s * sb, (s + 1) * sb)
            onehot = jnp.where(hits[s], 1.0, 0.0).astype(BF16)
            xg_ref[0, 0, rows, :] = jnp.dot(onehot, h_ref[0, tok, :], preferred_element_type=F32).astype(BF16)
            gate_ref[0, 0, rows, :] = jnp.sum(jnp.where(hits[s], aff_ref[0, 0, :, tok], 0.0), axis=1, keepdims=True)

    if win >= seq:
        gather_blocks([0] * n_blocks, seq)
        return
    starts = []
    fits = None
    for s in range(n_blocks):
        first = meta_ref[base + META_TSTART + s]
        last = meta_ref[base + META_TEND + s]
        start = pl.multiple_of(jnp.minimum(_align_down(first, GATHER_ALIGN), seq - win), GATHER_ALIGN)
        starts.append(start)
        ok = last < start + win
        fits = ok if fits is None else jnp.logical_and(fits, ok)

    @pl.when(fits)
    def _():
        gather_blocks(starts, win)

    @pl.when(jnp.logical_not(fits))
    def _():
        gather_blocks([0] * n_blocks, seq)


def _gather(meta, h2, slots4, aff4, cap):
    b, seq, _ = h2.shape
    e = slots4.shape[1]
    sb = min(GATHER_SLOT_BLOCK, cap)
    win = GATHER_WIN if seq > GATHER_WIN else seq
    assert (seq - win) % GATHER_ALIGN == 0
    row = pl.BlockSpec((1, 1, 1, seq), lambda bi, ei, m: (bi, ei, 0, 0))
    return pl.pallas_call(
        functools.partial(_gather_kernel, cap=cap, sb=sb, win=win),
        grid_spec=pltpu.PrefetchScalarGridSpec(
            num_scalar_prefetch=1,
            grid=(b, e),
            in_specs=[pl.BlockSpec((1, seq, D_MODEL), lambda bi, ei, m: (bi, 0, 0)), row, row],
            out_specs=[
                pl.BlockSpec((1, 1, cap, D_MODEL), lambda bi, ei, m: (ei, bi, 0, 0)),
                pl.BlockSpec((1, 1, cap, 1), lambda bi, ei, m: (ei, bi, 0, 0)),
            ],
        ),
        out_shape=[
            jax.ShapeDtypeStruct((e, b, cap, D_MODEL), BF16),
            jax.ShapeDtypeStruct((e, b, cap, 1), F32),
        ],
        compiler_params=_cparams(("parallel", "arbitrary"), 48),
        name="gather",
    )(meta, h2, slots4, aff4)


FF_CHUNK = 256
FFN_ROW_TILES = 2


def _ffn_kernel(*refs, with_ctx):
    if with_ctx:
        x_ref, gate_ref, xc_ref, gatec_ref, wg_ref, wu_ref, wd_ref, o_ref, oc_ref, acc_ref, accc_ref = refs
    else:
        x_ref, gate_ref, wg_ref, wu_ref, wd_ref, o_ref, acc_ref = refs
    fc = pl.program_id(2)
    last_fc = pl.num_programs(2) - 1

    def swiglu_chunk(streams):
        @pl.when(fc == 0)
        def _():
            for _, _, _, acc in streams:
                acc[...] = jnp.zeros_like(acc)

        wg = wg_ref[0, 0].astype(BF16)
        wu = wu_ref[0, 0].astype(BF16)
        wd = wd_ref[0, 0].astype(BF16)
        xs = [x[0] for x, _, _, _ in streams]
        gates_ = [jnp.dot(x, wg, preferred_element_type=F32) for x in xs]
        ups = [jnp.dot(x, wu, preferred_element_type=F32) for x in xs]
        hmids = [(a * jax.nn.sigmoid(a) * u).astype(BF16) for a, u in zip(gates_, ups)]
        for (_, _, _, acc), hmid in zip(streams, hmids):
            acc[...] += jnp.dot(hmid, wd, preferred_element_type=F32)

        @pl.when(fc == last_fc)
        def _():
            for _, gate, o, acc in streams:
                o[0] = (acc[...] * gate[0]).astype(BF16)

    main = (x_ref, gate_ref, o_ref, acc_ref)
    if not with_ctx:
        swiglu_chunk([main])
        return
    on_last_tile = pl.program_id(1) == pl.num_programs(1) - 1

    @pl.when(jnp.logical_not(on_last_tile))
    def _():
        swiglu_chunk([main])

    @pl.when(on_last_tile)
    def _():
        swiglu_chunk([main, (xc_ref, gatec_ref, oc_ref, accc_ref)])


def _ffn(xg3, gate3, xc3, gatec3, wg_all, wu_all, wd_all, layer):
    e, m, _ = xg3.shape
    tm = m // FFN_ROW_TILES
    assert m % tm == 0 and tm % 16 == 0
    with_ctx = xc3 is not None
    row = lambda w: pl.BlockSpec((1, tm, w), lambda ei, i, fc: (ei, i, 0))
    in_specs = [row(D_MODEL), row(1)]
    out_specs = [row(D_MODEL)]
    out_shape = [jax.ShapeDtypeStruct((e, m, D_MODEL), BF16)]
    scratch = [pltpu.VMEM((tm, D_MODEL), F32)]
    args = [xg3, gate3]
    if with_ctx:
        mc = xc3.shape[1]
        crow = lambda w: pl.BlockSpec((1, mc, w), lambda ei, i, fc: (ei, 0, 0))
        in_specs += [crow(D_MODEL), crow(1)]
        out_specs.append(crow(D_MODEL))
        out_shape.append(jax.ShapeDtypeStruct((e, mc, D_MODEL), BF16))
        scratch.append(pltpu.VMEM((mc, D_MODEL), F32))
        args += [xc3, gatec3]
    in_specs += [
        pl.BlockSpec((1, 1, D_MODEL, FF_CHUNK), lambda ei, i, fc: (layer, ei, 0, fc)),
        pl.BlockSpec((1, 1, D_MODEL, FF_CHUNK), lambda ei, i, fc: (layer, ei, 0, fc)),
        pl.BlockSpec((1, 1, FF_CHUNK, D_MODEL), lambda ei, i, fc: (layer, ei, fc, 0)),
    ]
    res = pl.pallas_call(
        functools.partial(_ffn_kernel, with_ctx=with_ctx),
        grid=(e, m // tm, EXPERT_FF // FF_CHUNK),
        in_specs=in_specs,
        out_specs=out_specs,
        out_shape=out_shape,
        scratch_shapes=scratch,
        compiler_params=_cparams(("parallel", "arbitrary", "arbitrary"), 56),
        name="ffn",
    )(*args, wg_all, wu_all, wd_all)
    return (res[0], res[1]) if with_ctx else (res[0], None)


COMBINE_EXPERTS = 4


def _combine_kernel(meta_ref, slot_ref, gy_ref, x_ref, g2_ref, gf_ref, o_ref, acc_ref, *, cap, win, final):
    bi = pl.program_id(0)
    ti = pl.program_id(1)
    ec = pl.program_id(2)
    tt = x_ref.shape[1]
    tn = (((0,), (0,)), ((), ()))

    @pl.when(ec == 0)
    def _():
        acc_ref[...] = jnp.zeros_like(acc_ref)

    def scatter(starts, width):
        j = lax.broadcasted_iota(I32, (width, tt), 0)
        onehot = jnp.concatenate(
            [jnp.where(slot_ref[0, k] == j + starts[k], 1.0, 0.0).astype(BF16) for k in range(COMBINE_EXPERTS)],
            axis=0)
        gy = jnp.concatenate([gy_ref[k, pl.ds(starts[k], width), :] for k in range(COMBINE_EXPERTS)], axis=0)
        acc_ref[...] += lax.dot_general(onehot, gy, tn, preferred_element_type=F32)

    if win >= cap:
        scatter([0] * COMBINE_EXPERTS, cap)
    else:
        starts = []
        fits = None
        for k in range(COMBINE_EXPERTS):
            base = (bi * N_EXPERTS + ec * COMBINE_EXPERTS + k) * META_W
            before = meta_ref[base + META_PSTART + ti]
            upto = meta_ref[base + META_PEND + ti]
            start = pl.multiple_of(jnp.minimum(_align_down(before, COMBINE_ALIGN), cap - win), COMBINE_ALIGN)
            starts.append(start)
            ok = upto <= start + win
            fits = ok if fits is None else jnp.logical_and(fits, ok)

        @pl.when(fits)
        def _():
            scatter(starts, win)

        @pl.when(jnp.logical_not(fits))
        def _():
            scatter([0] * COMBINE_EXPERTS, cap)

    @pl.when(ec == pl.num_programs(2) - 1)
    def _():
        xn = x_ref[0] + g2_ref[0] * acc_ref[...]
        if final:
            ms = jnp.mean(xn * xn, axis=-1, keepdims=True)
            xn = (xn * lax.rsqrt(ms + RMS_EPS)) * gf_ref[...]
        o_ref[0] = xn


def _combine(meta, slots4, gy3, row_off, x3, mod3, gfinal, *, cap, ctx_row, final):
    b, seq, _ = x3.shape
    e = slots4.shape[1]
    tt = min(seq, COMBINE_TILE)
    win = COMBINE_WIN if cap > COMBINE_WIN else cap
    assert row_off % cap == 0 and (cap - win) % COMBINE_ALIGN == 0
    blk_off = row_off // cap
    mrow = (lambda bi: bi) if ctx_row is None else (lambda bi: ctx_row)
    return pl.pallas_call(
        functools.partial(_combine_kernel, cap=cap, win=win, final=final),
        grid_spec=pltpu.PrefetchScalarGridSpec(
            num_scalar_prefetch=1,
            grid=(b, seq // tt, e // COMBINE_EXPERTS),
            in_specs=[
                pl.BlockSpec((1, COMBINE_EXPERTS, 1, tt), lambda bi, i, ec, m: (bi, ec, 0, i)),
                pl.BlockSpec((COMBINE_EXPERTS, cap, D_MODEL), lambda bi, i, ec, m: (ec, blk_off + bi, 0)),
                pl.BlockSpec((1, tt, D_MODEL), lambda bi, i, ec, m: (bi, i, 0)),
                pl.BlockSpec((1, 1, D_MODEL), lambda bi, i, ec, m: (mrow(bi), 0, 5)),
                pl.BlockSpec((1, D_MODEL), lambda bi, i, ec, m: (0, 0)),
            ],
            out_specs=pl.BlockSpec((1, tt, D_MODEL), lambda bi, i, ec, m: (bi, i, 0)),
            scratch_shapes=[pltpu.VMEM((tt, D_MODEL), F32)],
        ),
        out_shape=jax.ShapeDtypeStruct((b, seq, D_MODEL), F32),
        compiler_params=_cparams(("parallel", "parallel", "arbitrary"), 48),
        name="combine",
    )(meta, slots4, gy3, x3, mod3, gfinal)


def _moe_route(h2, logits_t):
    b, seq, _ = h2.shape
    cap = EC_CAPACITY_FACTOR * seq // N_EXPERTS
    slots, aff, meta = _select(logits_t, cap)
    meta = meta[:, :, :META_W].reshape(-1)
    slots4 = slots.reshape(b, N_EXPERTS, 1, seq)
    aff4 = aff.reshape(b, N_EXPERTS, 1, seq)
    xg, gate = _gather(meta, h2, slots4, aff4, cap)
    return meta, slots4, xg.reshape(N_EXPERTS, b * cap, D_MODEL), gate.reshape(N_EXPERTS, b * cap, 1), cap


def kernel(x, c, ctx, c_ctx, ada_w, ada_b, norm1_g, norm2_g, w_in, rpb, w_att_o, w_fourier, w_pool,
           pool_scale, w_out, w_router, w_exp_gate, w_exp_up, w_exp_down, final_norm_g):
    b, seq, d = x.shape
    lc = ctx.shape[1]
    assert d == D_MODEL and seq % (GRID_W * ATT_QROWS) == 0 and b + 1 <= MOD_ROWS
    rows = seq // GRID_W
    ctx_row = b

    cond = jnp.concatenate([c, c_ctx[None, :], jnp.zeros((MOD_ROWS - b - 1, d), F32)], axis=0)
    cl, sl = _dft_mats(seq, seq // 2)
    clc, slc = _dft_mats(lc, lc // 2)
    cc, sc = _dft_mats(FOURIER_GROUP_W, FOURIER_GROUP_W)
    gfinal = final_norm_g.reshape(1, d)

    for i in range(DEPTH):
        update_ctx = i < DEPTH - 1
        mod3 = _adaln(cond, ada_w[i], ada_b[i]).reshape(MOD_ROWS, 1, 6 * d)
        g1n = norm1_g[i].reshape(1, d)
        g2n = norm2_g[i].reshape(1, d)
        w_in_b = w_in[i].astype(BF16)
        wao = w_att_o[i].astype(BF16)
        wf = w_fourier[i].astype(BF16)
        wp = w_pool[i].astype(BF16)
        ps = pool_scale[i].reshape(1, d)
        wo = w_out[i].astype(BF16)
        wr_t = jnp.pad(jnp.concatenate([w_router[i], w_router[i]], axis=1), ((0, 0), (0, LANE - 2 * N_EXPERTS)))
        bias = _bias_table(rpb[i], rows)

        pb, pp = _modproj(x.reshape(b * seq, d), mod3, g1n, w_in_b, seq=seq, ctx_row=None,
                          n_out=IN_W, with_pool=True)
        n_ctx = IN_W if update_ctx else Q_OFF
        pcb, pcp = _modproj(ctx.reshape(b * lc, d), mod3, g1n, w_in_b[:, :n_ctx], seq=lc, ctx_row=ctx_row,
                            n_out=n_ctx, with_pool=update_ctx)
        pb3 = pb.reshape(b, seq, IN_W)
        pcb3 = pcb.reshape(b, lc, n_ctx)

        att = _nattn(pb3, pcb3, bias)
        four = _fourier(pb3, cc, sc, cl, sl)
        pooled = _pool(pp.reshape(b, seq, POOL_W))
        x, h2, logits_t = _merge(att, four, pooled, pb3, x, mod3, wao, wf, wp, ps, wo, g2n, wr_t, ctx_row=None)
        meta_x, slots_x, xg, gate, cap = _moe_route(h2, logits_t)

        if update_ctx:
            att_c = _cattn(pcb3)
            four_c = _fourier(pcb3, cc, sc, clc, slc)
            pooled_c = _pool(pcp.reshape(b, lc, POOL_W))
            ctx, h2_c, logits_c = _merge(att_c, four_c, pooled_c, pcb3, ctx, mod3, wao, wf, wp, ps, wo, g2n, wr_t,
                                         ctx_row=ctx_row)
            meta_c, slots_c, xg_c, gate_c, cap_c = _moe_route(h2_c, logits_c)
        else:
            xg_c = gate_c = None

        gy, gy_c = _ffn(xg, gate, xg_c, gate_c, w_exp_gate, w_exp_up, w_exp_down, i)
        if update_ctx:
            ctx = _combine(meta_c, slots_c, gy_c, 0, ctx, mod3, gfinal, cap=cap_c, ctx_row=ctx_row, final=False)
        x = _combine(meta_x, slots_x, gy, 0, x, mod3, gfinal, cap=cap, ctx_row=None, final=not update_ctx)
    return x
```

```python
import functools
import math

import jax
import jax.numpy as jnp
from jax import lax
from jax.experimental import pallas as pl
from jax.experimental.pallas import tpu as pltpu

F32 = jnp.float32
BF16 = jnp.bfloat16
I32 = jnp.int32
HIGHEST = lax.Precision.HIGHEST

D_MODEL = 1024
DEPTH = 2
GRID_W = 64
N_HEADS = 16
HEAD_DIM = 64
WIN_R = 8
WIN_C = 16
FOURIER_GROUPS = 4
FOURIER_GROUP_W = 128
FOURIER_W = 512
POOL_WINDOWS = (2, 4, 8, 16)
POOL_GROUP_W = 128
POOL_W = 512
POOL_OUT_GROUP = 256
K_OFF, V_OFF, Q_OFF, F_OFF, P_OFF, G_OFF = 0, 1024, 2048, 3072, 3584, 4096
IN_W = 7168
N_EXPERTS = 16
EC_CAPACITY_FACTOR = 2
EXPERT_FF = 2816
RMS_EPS = 1e-6

LANE = 128
HEAD_PAIR_W = 2 * HEAD_DIM
N_HEAD_PAIRS = N_HEADS // 2
MOD_ROWS = 16
NEG_BIG = -1e30
ATT_QROWS = 2
ATT_KROWS = 10
ATT_TOP = (WIN_R // 2 + ATT_QROWS - 1) // ATT_QROWS
ATT_BOT = (WIN_R // 2 - 1 + ATT_QROWS - 1) // ATT_QROWS
ATT_CLASSES = ATT_TOP + 1 + ATT_BOT
ATT_ALIGN = 64
SELECT_ITERS = 32
MiB = 1024 * 1024


def _cparams(sem, vmem_mib):
    return pltpu.CompilerParams(dimension_semantics=sem, vmem_limit_bytes=vmem_mib * MiB)


def _adaln_kernel(c_ref, w_ref, b_ref, o_ref):
    c = c_ref[...]
    s = c * jax.nn.sigmoid(c)
    o_ref[...] = jnp.dot(s, w_ref[...], precision=HIGHEST, preferred_element_type=F32) + b_ref[...]


def _adaln(cond_rows, ada_w, ada_b):
    n = ada_w.shape[1]
    tn = 1024
    return pl.pallas_call(
        _adaln_kernel,
        grid=(n // tn,),
        in_specs=[
            pl.BlockSpec((MOD_ROWS, D_MODEL), lambda j: (0, 0)),
            pl.BlockSpec((D_MODEL, tn), lambda j: (0, j)),
            pl.BlockSpec((1, tn), lambda j: (0, j)),
        ],
        out_specs=pl.BlockSpec((MOD_ROWS, tn), lambda j: (0, j)),
        out_shape=jax.ShapeDtypeStruct((MOD_ROWS, n), F32),
        compiler_params=_cparams(("arbitrary",), 32),
        name="adaln",
    )(cond_rows, ada_w, ada_b.reshape(1, n))


def _modulate(x, g, shift, scale):
    ms = jnp.mean(x * x, axis=-1, keepdims=True)
    y = x * lax.rsqrt(ms + RMS_EPS)
    return (y * g) * (1.0 + scale) + shift


MODPROJ_TN = 1792


def _modproj_kernel(x_ref, sh_ref, sc_ref, g_ref, w_ref, *rest, pool_tile):
    if pool_tile is None:
        o_ref, h_ref = rest
        pp_ref = None
    else:
        o_ref, pp_ref, h_ref = rest
    j = pl.program_id(1)

    @pl.when(j == 0)
    def _():
        h = _modulate(x_ref[...], g_ref[...], sh_ref[0], sc_ref[0])
        h_ref[...] = h.astype(BF16)

    acc = jnp.dot(h_ref[...], w_ref[...], preferred_element_type=F32)
    o_ref[...] = acc.astype(BF16)
    if pool_tile is not None:
        @pl.when(j == pool_tile)
        def _():
            lo = P_OFF % acc.shape[1]
            pp_ref[...] = acc[:, lo:lo + POOL_W]


def _modproj(x2, mod3, gain, w, *, seq, ctx_row, n_out, with_pool):
    rows = x2.shape[0]
    tn = MODPROJ_TN if n_out % MODPROJ_TN == 0 else 1024
    if ctx_row is None:
        tm = min(seq, 1024)
        tiles_per_seq = seq // tm
        mrow = lambda i: i // tiles_per_seq
    else:
        tm = min(rows, 1024)
        mrow = lambda i: ctx_row
    assert rows % tm == 0 and n_out % tn == 0 and (not with_pool or P_OFF % tn + POOL_W <= tn)
    pool_tile = (P_OFF // tn) if with_pool else None
    out_shape = [jax.ShapeDtypeStruct((rows, n_out), BF16)]
    out_specs = [pl.BlockSpec((tm, tn), lambda i, j: (i, j))]
    if with_pool:
        out_shape.append(jax.ShapeDtypeStruct((rows, POOL_W), F32))
        out_specs.append(pl.BlockSpec((tm, POOL_W), lambda i, j: (i, 0)))
    res = pl.pallas_call(
        functools.partial(_modproj_kernel, pool_tile=pool_tile),
        grid=(rows // tm, n_out // tn),
        in_specs=[
            pl.BlockSpec((tm, D_MODEL), lambda i, j: (i, 0)),
            pl.BlockSpec((1, 1, D_MODEL), lambda i, j: (mrow(i), 0, 0)),
            pl.BlockSpec((1, 1, D_MODEL), lambda i, j: (mrow(i), 0, 1)),
            pl.BlockSpec((1, D_MODEL), lambda i, j: (0, 0)),
            pl.BlockSpec((D_MODEL, tn), lambda i, j: (0, j)),
        ],
        out_specs=out_specs,
        out_shape=out_shape,
        scratch_shapes=[pltpu.VMEM((tm, D_MODEL), BF16)],
        compiler_params=_cparams(("parallel", "arbitrary"), 48),
        name="modproj",
    )(x2, mod3, mod3, gain, w)
    return res if with_pool else (res[0], None)


RPB_ROWS = 16


def _clip(v, lo, hi):
    return max(lo, min(v, hi))


def _bias_kernel(rpb_ref, o_ref, *, rows):
    qc = lax.broadcasted_iota(I32, (GRID_W, LANE), 0)
    lane = lax.broadcasted_iota(I32, (GRID_W, LANE), 1)
    kc = lane & (GRID_W - 1)
    first_half = lane < GRID_W
    cs = jnp.clip(qc - WIN_C // 2, 0, GRID_W - WIN_C)
    col_valid = (kc >= cs) & (kc < cs + WIN_C)
    n_roff = 2 * WIN_R - 1
    shift_a = LANE - (WIN_C - 1)
    shift_b = GRID_W - (WIN_C - 1)
    n_steps = rows // ATT_QROWS
    for cls in range(ATT_CLASSES):
        i_rep = cls if cls <= ATT_TOP else cls - (ATT_TOP + 1) + (n_steps - ATT_BOT)
        s = _clip(ATT_QROWS * i_rep - WIN_R // 2, 0, rows - ATT_KROWS)
        for ri in range(ATT_QROWS):
            r = ATT_QROWS * i_rep + ri
            rs = _clip(r - WIN_R // 2, 0, rows - WIN_R)
            for m in range(ATT_KROWS // 2):
                krow_a = s + 2 * m
                krow_b = krow_a + 1
                va = rs <= krow_a < rs + WIN_R
                vb = rs <= krow_b < rs + WIN_R
                ro_a = _clip(krow_a - r + WIN_R - 1, 0, n_roff - 1)
                ro_b = _clip(krow_b - r + WIN_R - 1, 0, n_roff - 1)
                row_a = jnp.broadcast_to(rpb_ref[0, ro_a:ro_a + 1, :], (GRID_W, LANE))
                row_b = jnp.broadcast_to(rpb_ref[0, ro_b:ro_b + 1, :], (GRID_W, LANE))
                acc = jnp.where(first_half,
                                pltpu.roll(row_a, shift_a, 1, stride=1, stride_axis=0),
                                pltpu.roll(row_b, shift_b, 1, stride=1, stride_axis=0))
                valid = col_valid
                if not va:
                    valid = valid & jnp.logical_not(first_half)
                if not vb:
                    valid = valid & first_half
                tile = jnp.where(valid, acc, NEG_BIG)
                o_ref[0, cls, ri * GRID_W:(ri + 1) * GRID_W, m * LANE:(m + 1) * LANE] = tile


def _bias_table(rpb, rows):
    nq = ATT_QROWS * GRID_W
    nk = ATT_KROWS * GRID_W
    n_roff, n_coff = rpb.shape[1], rpb.shape[2]
    assert n_roff <= RPB_ROWS and n_coff <= LANE
    rpb_p = jnp.pad(rpb, ((0, 0), (0, RPB_ROWS - n_roff), (0, LANE - n_coff)))
    return pl.pallas_call(
        functools.partial(_bias_kernel, rows=rows),
        grid=(N_HEADS,),
        in_specs=[pl.BlockSpec((1, RPB_ROWS, LANE), lambda h: (h, 0, 0))],
        out_specs=pl.BlockSpec((1, ATT_CLASSES, nq, nk), lambda h: (h, 0, 0, 0)),
        out_shape=jax.ShapeDtypeStruct((N_HEADS, ATT_CLASSES, nq, nk), F32),
        compiler_params=_cparams(("parallel",), 32),
        name="bias_table",
    )(rpb_p)


_NT = (((1,), (1,)), ((), ()))


def _scores(qm, kw, kc, bias):
    sw = lax.dot_general(qm, kw, _NT, preferred_element_type=F32)
    if bias is not None:
        sw = sw + bias
    sc = lax.dot_general(qm, kc, _NT, preferred_element_type=F32) if kc is not None else None
    return sw, sc


def _probs(sw, sc):
    m = jnp.max(sw, axis=-1, keepdims=True)
    if sc is not None:
        m = jnp.maximum(m, jnp.max(sc, axis=-1, keepdims=True))
    pw = jnp.exp(sw - m)
    l = jnp.sum(pw, axis=-1, keepdims=True)
    pc = None
    if sc is not None:
        pc = jnp.exp(sc - m)
        l = l + jnp.sum(pc, axis=-1, keepdims=True)
        pc = pc.astype(BF16)
    return pw.astype(BF16), pc, l


def _pv(pw, pc, l, vw, vc):
    o = jnp.dot(pw, vw, preferred_element_type=F32)
    if pc is not None:
        o = o + jnp.dot(pc, vc, preferred_element_type=F32)
    return o / l


ATT_UNROLL = 4


ATT_KEY_TILE = 256


def _key_tiles(n):
    return [(off, min(ATT_KEY_TILE, n - off)) for off in range(0, n, ATT_KEY_TILE)]


def _lane_fold(x, op):
    out = x[:, :LANE]
    for c in range(1, x.shape[1] // LANE):
        out = op(out, x[:, c * LANE:(c + 1) * LANE])
    return out


def _nattn_kernel(k_ref, v_ref, q_ref, kc_ref, vc_ref, b_ref, o_ref, s_ref, m_ref, *, rows):
    nq = ATT_QROWS * GRID_W
    nk = ATT_KROWS * GRID_W
    lc = kc_ref.shape[1]
    n_steps = rows // ATT_QROWS
    lane = lax.broadcasted_iota(I32, (nq, HEAD_PAIR_W), 1)
    second = lane >= HEAD_DIM
    scale = HEAD_DIM ** -0.5
    win_tiles = _key_tiles(nk)
    ctx_tiles = _key_tiles(lc)

    ones_cols = jnp.ones((ATT_KEY_TILE, HEAD_PAIR_W), BF16)

    def with_ones(v):
        return jnp.concatenate([v, ones_cols[:v.shape[0]]], axis=1)

    def geometry(ii, u):
        i = ii * ATT_UNROLL + u
        s = jnp.clip(ATT_QROWS * i - WIN_R // 2, 0, rows - ATT_KROWS)
        kstart = pl.multiple_of(s * GRID_W, ATT_ALIGN)
        qstart = pl.multiple_of(i * nq, ATT_ALIGN)
        cls = jnp.where(i < ATT_TOP, i,
                        jnp.where(i < n_steps - ATT_BOT, ATT_TOP, i - (n_steps - ATT_BOT) + ATT_TOP + 1))
        return kstart, qstart, cls

    def pass1(ii, buf, u, hh):
        kstart, qstart, cls = geometry(ii, u)
        c = 2 * u + hh
        q2 = (q_ref[0, pl.ds(qstart, nq), :].astype(F32) * scale).astype(BF16)
        head_lanes = second if hh else jnp.logical_not(second)
        qm = jnp.where(head_lanes, q2, jnp.zeros_like(q2))
        m_run = None
        for off, width in win_tiles:
            kt = k_ref[0, pl.ds(kstart + off, width), :]
            st = lax.dot_general(qm, kt, _NT, preferred_element_type=F32) + b_ref[hh, cls, :, off:off + width]
            s_ref[buf, c, :, off:off + width] = st
            mt = _lane_fold(st, jnp.maximum)
            m_run = mt if m_run is None else jnp.maximum(m_run, mt)
        for off, width in ctx_tiles:
            st = lax.dot_general(qm, kc_ref[0, off:off + width, :], _NT, preferred_element_type=F32)
            s_ref[buf, c, :, nk + off:nk + off + width] = st
            m_run = jnp.maximum(m_run, _lane_fold(st, jnp.maximum))
        m_ref[buf, c] = jnp.max(m_run, axis=-1, keepdims=True)

    def pass2(ii, buf, u, hh):
        kstart, _, _ = geometry(ii, u)
        c = 2 * u + hh
        m = m_ref[buf, c]
        acc = None
        for off, width in win_tiles:
            pt = jnp.exp(s_ref[buf, c, :, off:off + width] - m).astype(BF16)
            pv = jnp.dot(pt, with_ones(v_ref[0, pl.ds(kstart + off, width), :]), preferred_element_type=F32)
            acc = pv if acc is None else acc + pv
        for off, width in ctx_tiles:
            pt = jnp.exp(s_ref[buf, c, :, nk + off:nk + off + width] - m).astype(BF16)
            acc = acc + jnp.dot(pt, with_ones(vc_ref[0, off:off + width, :]), preferred_element_type=F32)
        return acc[:, :HEAD_PAIR_W] / acc[:, HEAD_PAIR_W:]

    def store(ii, u, o0, o1):
        _, qstart, _ = geometry(ii, u)
        o_ref[0, pl.ds(qstart, nq), :] = jnp.where(second, o1, o0).astype(BF16)

    n_iter = n_steps // ATT_UNROLL
    for u in range(ATT_UNROLL):
        for hh in range(2):
            pass1(0, 0, u, hh)

    def overlapped(ii, new):
        for u in range(ATT_UNROLL):
            outs = []
            for hh in range(2):
                pass1(ii, new, u, hh)
                outs.append(pass2(ii - 1, 1 - new, u, hh))
            store(ii - 1, u, *outs)

    def body(jj, carry):
        overlapped(2 * jj + 1, 1)
        overlapped(2 * jj + 2, 0)
        return carry

    assert n_iter % 2 == 0
    lax.fori_loop(0, n_iter // 2 - 1, body, 0)
    last = n_iter - 1
    overlapped(last, 1)
    for u in range(ATT_UNROLL):
        store(last, u, *[pass2(last, 1, u, hh) for hh in range(2)])


def _nattn(pb3, pcb3, bias):
    b, seq, _ = pb3.shape
    lc = pcb3.shape[1]
    rows = seq // GRID_W
    nq = ATT_QROWS * GRID_W
    nk = ATT_KROWS * GRID_W
    kblk, vblk, qblk = K_OFF // LANE, V_OFF // LANE, Q_OFF // LANE
    return pl.pallas_call(
        functools.partial(_nattn_kernel, rows=rows),
        grid=(N_HEAD_PAIRS, b),
        in_specs=[
            pl.BlockSpec((1, seq, HEAD_PAIR_W), lambda hp, bi: (bi, 0, kblk + hp)),
            pl.BlockSpec((1, seq, HEAD_PAIR_W), lambda hp, bi: (bi, 0, vblk + hp)),
            pl.BlockSpec((1, seq, HEAD_PAIR_W), lambda hp, bi: (bi, 0, qblk + hp)),
            pl.BlockSpec((1, lc, HEAD_PAIR_W), lambda hp, bi: (bi, 0, kblk + hp)),
            pl.BlockSpec((1, lc, HEAD_PAIR_W), lambda hp, bi: (bi, 0, vblk + hp)),
            pl.BlockSpec((2, ATT_CLASSES, nq, nk), lambda hp, bi: (hp, 0, 0, 0)),
        ],
        out_specs=pl.BlockSpec((1, seq, HEAD_PAIR_W), lambda hp, bi: (bi, 0, hp)),
        out_shape=jax.ShapeDtypeStruct((b, seq, N_HEADS * HEAD_DIM), BF16),
        scratch_shapes=[pltpu.VMEM((2, 2 * ATT_UNROLL, nq, nk + lc), F32),
                        pltpu.VMEM((2, 2 * ATT_UNROLL, nq, 1), F32)],
        compiler_params=_cparams(("parallel", "parallel"), 40),
        name="nattn",
    )(pb3, pb3, pb3, pcb3, pcb3, bias)


def _cattn_kernel(k_ref, v_ref, q_ref, o_ref):
    lc = q_ref.shape[1]
    lane = lax.broadcasted_iota(I32, (lc, HEAD_PAIR_W), 1)
    second = lane >= HEAD_DIM
    chains = []
    for hp in range(N_HEAD_PAIRS):
        cols = slice(hp * HEAD_PAIR_W, (hp + 1) * HEAD_PAIR_W)
        q2 = (q_ref[0, :, cols].astype(F32) * (HEAD_DIM ** -0.5)).astype(BF16)
        for hh in range(2):
            head_lanes = second if hh else jnp.logical_not(second)
            qm = jnp.where(head_lanes, q2, jnp.zeros_like(q2))
            chains.append(_scores(qm, k_ref[0, :, cols], None, None))
    probs = [_probs(sw, sc) for sw, sc in chains]
    for hp in range(N_HEAD_PAIRS):
        cols = slice(hp * HEAD_PAIR_W, (hp + 1) * HEAD_PAIR_W)
        outs = [_pv(probs[2 * hp + hh][0], None, probs[2 * hp + hh][2], v_ref[0, :, cols], None) for hh in range(2)]
        o_ref[0, :, cols] = jnp.where(second, outs[1], outs[0]).astype(BF16)


def _cattn(pcb3):
    b, lc, _ = pcb3.shape
    width = N_HEADS * HEAD_DIM
    spec = lambda off: pl.BlockSpec((1, lc, width), lambda bi: (bi, 0, off // width))
    return pl.pallas_call(
        _cattn_kernel,
        grid=(b,),
        in_specs=[spec(K_OFF), spec(V_OFF), spec(Q_OFF)],
        out_specs=pl.BlockSpec((1, lc, width), lambda bi: (bi, 0, 0)),
        out_shape=jax.ShapeDtypeStruct((b, lc, width), BF16),
        compiler_params=_cparams(("parallel",), 32),
        name="cattn",
    )(pcb3, pcb3, pcb3)


def _dft_kernel(c_ref, s_ref, *, n):
    tk, ncols = c_ref.shape
    k = pl.program_id(0) * tk + lax.broadcasted_iota(I32, (tk, LANE), 0)
    lane = lax.broadcasted_iota(I32, (tk, LANE), 1)
    w = 2.0 * math.pi / n
    ang_p = ((k * lane) & (n - 1)).astype(F32) * w
    cp, sp = jnp.cos(ang_p), jnp.sin(ang_p)
    ang_q = ((k * (lane * LANE)) & (n - 1)).astype(F32) * w
    cq, sq = jnp.cos(ang_q), jnp.sin(ang_q)
    for q in range(ncols // LANE):
        cols = slice(q * LANE, (q + 1) * LANE)
        cqq, sqq = cq[:, q:q + 1], sq[:, q:q + 1]
        c_ref[:, cols] = (cqq * cp - sqq * sp).astype(BF16)
        s_ref[:, cols] = (sqq * cp + cqq * sp).astype(BF16)


def _dft_mats(n, ncols):
    tk = min(n, 256)
    return pl.pallas_call(
        functools.partial(_dft_kernel, n=n),
        grid=(n // tk,),
        in_specs=[],
        out_specs=[pl.BlockSpec((tk, ncols), lambda i: (i, 0))] * 2,
        out_shape=[jax.ShapeDtypeStruct((n, ncols), BF16)] * 2,
        compiler_params=_cparams(("parallel",), 48),
        name=f"dft_mats_{n}",
    )()


REV_BLOCK = 128


def _fourier_kernel(u_ref, cc_ref, sc_ref, cl_ref, sl_ref, o_ref, us_ref, ud_ref, a_ref, b_ref, ah_ref, *, seq):
    half = seq // 2
    nblk = seq // REV_BLOCK
    tk = o_ref.shape[1]

    @pl.when(pl.program_id(1) == 0)
    def _():
        d_i = lax.broadcasted_iota(I32, (REV_BLOCK, REV_BLOCK), 0)
        s_i = lax.broadcasted_iota(I32, (REV_BLOCK, REV_BLOCK), 1)
        flip = jnp.where((d_i >= 1) & (s_i == REV_BLOCK - d_i), 1.0, 0.0).astype(BF16)
        row = lax.broadcasted_iota(I32, (REV_BLOCK, FOURIER_W), 0)
        for blk in range(nblk // 2):
            lo = u_ref[0, blk * REV_BLOCK:(blk + 1) * REV_BLOCK, :].astype(F32)
            src = u_ref[0, (nblk - 1 - blk) * REV_BLOCK:(nblk - blk) * REV_BLOCK, :]
            rev = jnp.dot(flip, src, preferred_element_type=F32)
            if blk > 0:
                head = u_ref[0, (nblk - blk) * REV_BLOCK:(nblk - blk) * REV_BLOCK + 16, :].astype(F32)
                rev = jnp.where(row == 0, head[0:1, :], rev)
            rows = slice(blk * REV_BLOCK, (blk + 1) * REV_BLOCK)
            us_ref[rows, :] = (lo + rev).astype(BF16)
            ud_ref[rows, :] = (lo - rev).astype(BF16)
        mid = u_ref[0, half:half + 16, :]
        for g in range(FOURIER_GROUPS):
            sl = slice(g * FOURIER_GROUP_W, (g + 1) * FOURIER_GROUP_W)
            a_ref[:, sl] = jnp.dot(us_ref[:, sl], cc_ref[...], preferred_element_type=F32).astype(BF16)
            b_ref[:, sl] = jnp.dot(ud_ref[:, sl], sc_ref[...], preferred_element_type=F32).astype(BF16)
            ah_ref[:, sl] = jnp.dot(mid[:, sl], cc_ref[...], preferred_element_type=F32)

    k = pl.program_id(1) * tk + lax.broadcasted_iota(I32, (tk, 1), 0)
    sign = (1 - 2 * (k & 1)).astype(F32)
    y = (jnp.dot(cl_ref[...], a_ref[...], preferred_element_type=F32)
         - jnp.dot(sl_ref[...], b_ref[...], preferred_element_type=F32)
         + sign * ah_ref[0:1, :])
    o_ref[0] = (y * (1.0 / math.sqrt(seq * FOURIER_GROUP_W))).astype(BF16)


def _fourier(pb3, cc, sc, cl, sl):
    b, seq, _ = pb3.shape
    tk = min(seq, 512)
    half = seq // 2
    assert seq % (2 * REV_BLOCK) == 0
    return pl.pallas_call(
        functools.partial(_fourier_kernel, seq=seq),
        grid=(b, seq // tk),
        in_specs=[
            pl.BlockSpec((1, seq, FOURIER_W), lambda bi, k: (bi, 0, F_OFF // FOURIER_W)),
            pl.BlockSpec((FOURIER_GROUP_W, FOURIER_GROUP_W), lambda bi, k: (0, 0)),
            pl.BlockSpec((FOURIER_GROUP_W, FOURIER_GROUP_W), lambda bi, k: (0, 0)),
            pl.BlockSpec((tk, half), lambda bi, k: (k, 0)),
            pl.BlockSpec((tk, half), lambda bi, k: (k, 0)),
        ],
        out_specs=pl.BlockSpec((1, tk, FOURIER_W), lambda bi, k: (bi, k, 0)),
        out_shape=jax.ShapeDtypeStruct((b, seq, FOURIER_W), BF16),
        scratch_shapes=[pltpu.VMEM((half, FOURIER_W), BF16)] * 4 + [pltpu.VMEM((16, FOURIER_W), F32)],
        compiler_params=_cparams(("parallel", "arbitrary"), 48),
        name="fourier",
    )(pb3, cc, sc, cl, sl)


POOL_PAD = 8


def _pool_kernel(u_ref, o_ref, pad_ref, *, seq):
    t = lax.broadcasted_iota(I32, (seq, POOL_GROUP_W), 0)
    zeros = jnp.zeros((POOL_PAD, POOL_GROUP_W), F32)
    pad_ref[0:POOL_PAD, :] = zeros
    pad_ref[seq + POOL_PAD:seq + 2 * POOL_PAD, :] = zeros
    pad_ref[POOL_PAD:seq + POOL_PAD, :] = u_ref[0]
    for g, w in enumerate(POOL_WINDOWS):
        @pl.when(pl.program_id(1) == g)
        def _(w=w):
            acc = None
            for d in range(-(w // 2), w - w // 2):
                term = pad_ref[pl.ds(POOL_PAD + d, seq), :]
                acc = term if acc is None else acc + term
            cnt = (jnp.minimum(t + (w - w // 2), seq) - jnp.maximum(t - w // 2, 0)).astype(F32)
            o_ref[0] = (acc / cnt - u_ref[0]).astype(BF16)


def _pool(pp3):
    b, seq, _ = pp3.shape
    spec = pl.BlockSpec((1, seq, POOL_GROUP_W), lambda bi, g: (bi, 0, g))
    return pl.pallas_call(
        functools.partial(_pool_kernel, seq=seq),
        grid=(b, len(POOL_WINDOWS)),
        in_specs=[spec],
        out_specs=spec,
        out_shape=jax.ShapeDtypeStruct((b, seq, POOL_W), BF16),
        scratch_shapes=[pltpu.VMEM((seq + 2 * POOL_PAD, POOL_GROUP_W), F32)],
        compiler_params=_cparams(("parallel", "parallel"), 32),
        name="pool",
    )(pp3)


MERGE_PARTS = 4


def _merge_kernel(att_ref, four_ref, pool_ref, ga_ref, gf_ref, gp_ref, x_ref, g1_ref,
                  wao_ref, wf_ref, wp_ref, ps_ref, wo_ref, sh2_ref, sc2_ref, g2n_ref, wr_ref,
                  o_ref, h_ref, lg_ref):
    tm = x_ref.shape[1]
    n_parts = min(MERGE_PARTS, tm // LANE)
    parts = [slice(p * (tm // n_parts), (p + 1) * (tm // n_parts)) for p in range(n_parts)]
    wr = wr_ref[...]
    wr_hi = wr.astype(BF16)
    wr_lo = (wr - wr_hi.astype(F32)).astype(BF16)
    wr_split = jnp.where(lax.broadcasted_iota(I32, wr.shape, 1) < N_EXPERTS, wr_hi, wr_lo)

    def sigmoid(ref, rows):
        return 0.5 * jnp.tanh(0.5 * ref[0, rows, :].astype(F32)) + 0.5

    branches = []
    for rows in parts:
        y_att = jnp.dot(att_ref[0, rows, :], wao_ref[...], preferred_element_type=F32)
        y_four = jnp.dot(four_ref[0, rows, :], wf_ref[...], preferred_element_type=F32)
        pooled = pool_ref[0, rows, :]
        y_pool = jnp.concatenate(
            [jnp.dot(pooled[:, g * POOL_GROUP_W:(g + 1) * POOL_GROUP_W], wp_ref[g], preferred_element_type=F32)
             for g in range(len(POOL_WINDOWS))], axis=-1) * ps_ref[...]
        branches.append((y_att, y_four, y_pool))
    resid = []
    for rows, (y_att, y_four, y_pool) in zip(parts, branches):
        merged = (sigmoid(ga_ref, rows) * y_att + sigmoid(gf_ref, rows) * y_four + sigmoid(gp_ref, rows) * y_pool)
        y = jnp.dot(merged.astype(BF16), wo_ref[...], preferred_element_type=F32)
        xn = x_ref[0, rows, :] + g1_ref[0] * y
        o_ref[0, rows, :] = xn
        resid.append(xn)
    for rows, xn in zip(parts, resid):
        n = xn.shape[0]
        h = _modulate(xn, g2n_ref[...], sh2_ref[0], sc2_ref[0])
        h_hi = h.astype(BF16)
        h_ref[0, rows, :] = h_hi
        h_lo = (h - h_hi.astype(F32)).astype(BF16)
        r = jnp.dot(jnp.concatenate([h_hi, h_lo], axis=0), wr_split, preferred_element_type=F32)
        hi_t = r[:n].T
        lo_t = r[n:].T
        lg_ref[0, :, rows] = (hi_t[:N_EXPERTS] + hi_t[N_EXPERTS:2 * N_EXPERTS]
                              + lo_t[:N_EXPERTS] + lo_t[N_EXPERTS:2 * N_EXPERTS])


def _merge(att, four, pooled, pb3, x3, mod3, wao, wf, wp, ps, wo, g2n, wr_t, *, ctx_row):
    b, seq, _ = x3.shape
    tm = min(seq, 512)
    gblk = G_OFF // D_MODEL
    mrow = (lambda bi: bi) if ctx_row is None else (lambda bi: ctx_row)
    tok = lambda w: pl.BlockSpec((1, tm, w), lambda bi, i: (bi, i, 0))
    gate = lambda k: pl.BlockSpec((1, tm, D_MODEL), lambda bi, i: (bi, i, gblk + k))
    full = lambda a: pl.BlockSpec(a.shape, lambda bi, i: (0,) * a.ndim)
    modc = lambda k: pl.BlockSpec((1, 1, D_MODEL), lambda bi, i: (mrow(bi), 0, k))
    return pl.pallas_call(
        _merge_kernel,
        grid=(b, seq // tm),
        in_specs=[
            tok(D_MODEL), tok(FOURIER_W), tok(POOL_W), gate(0), gate(1), gate(2), tok(D_MODEL), modc(2),
            full(wao), full(wf), full(wp), full(ps), full(wo), modc(3), modc(4), full(g2n), full(wr_t),
        ],
        out_specs=[tok(D_MODEL), tok(D_MODEL), pl.BlockSpec((1, N_EXPERTS, tm), lambda bi, i: (bi, 0, i))],
        out_shape=[
            jax.ShapeDtypeStruct((b, seq, D_MODEL), F32),
            jax.ShapeDtypeStruct((b, seq, D_MODEL), BF16),
            jax.ShapeDtypeStruct((b, N_EXPERTS, seq), F32),
        ],
        compiler_params=_cparams(("parallel", "parallel"), 48),
        name="merge",
    )(att, four, pooled, pb3, pb3, pb3, x3, mod3, wao, wf, wp, ps, wo, mod3, mod3, g2n, wr_t)


def _exclusive_prefix(mask, tri):
    e, seq = mask.shape
    ones = jnp.where(mask, 1.0, 0.0)
    offs = jnp.zeros((e, 1), F32)
    pieces = []
    for k in range(seq // LANE):
        blk = ones[:, k * LANE:(k + 1) * LANE]
        local = jnp.dot(blk.astype(BF16), tri, preferred_element_type=F32)
        pieces.append(local + offs)
        offs = offs + jnp.sum(blk, axis=1, keepdims=True)
    return jnp.concatenate(pieces, axis=1)


META_TSTART, META_TEND, META_PSTART, META_PEND, META_W = 0, 4, 8, 12, 16
GATHER_SLOT_BLOCK = 128
GATHER_WIN = 1408
GATHER_ALIGN = 128
COMBINE_TILE = 1024
COMBINE_WIN = 192
COMBINE_ALIGN = 16


def _select_kernel(lg_ref, slot_ref, aff_ref, meta_ref, *, cap):
    z = lg_ref[0]
    z = z - jnp.max(z, axis=0, keepdims=True)
    ez = jnp.exp(z)
    a = ez / jnp.sum(ez, axis=0, keepdims=True)
    aff_ref[0] = a
    capf = float(cap)

    def count_ge(th):
        return jnp.sum(jnp.where(a >= th, 1.0, 0.0), axis=1, keepdims=True)

    def bisect(_, lohi):
        lo, hi = lohi
        q2 = (lo + hi) * 0.5
        q1 = (lo + q2) * 0.5
        q3 = (q2 + hi) * 0.5
        g1, g2, g3 = count_ge(q1) >= capf, count_ge(q2) >= capf, count_ge(q3) >= capf
        new_lo = jnp.where(g3, q3, jnp.where(g2, q2, jnp.where(g1, q1, lo)))
        new_hi = jnp.where(g3, hi, jnp.where(g2, q3, jnp.where(g1, q2, q1)))
        return new_lo, new_hi

    e = a.shape[0]
    lo, hi = lax.fori_loop(0, SELECT_ITERS, bisect,
                           (jnp.zeros((e, 1), F32), jnp.full((e, 1), 2.0, F32)))
    r_i = lax.broadcasted_iota(I32, (LANE, LANE), 0)
    c_i = lax.broadcasted_iota(I32, (LANE, LANE), 1)
    tri = jnp.where(r_i < c_i, 1.0, 0.0).astype(BF16)
    above = a >= hi
    n_above = jnp.sum(jnp.where(above, 1.0, 0.0), axis=1, keepdims=True)
    tied = (a >= lo) & jnp.logical_not(above)
    tie_rank = _exclusive_prefix(tied, tri)
    sel = above | (tied & (tie_rank < capf - n_above))
    pos = _exclusive_prefix(sel, tri)
    slot_ref[0] = jnp.where(sel, pos, -1.0).astype(I32)

    seq = a.shape[1]
    t = lax.broadcasted_iota(I32, (e, seq), 1).astype(F32)
    lane = lax.broadcasted_iota(I32, (e, LANE), 1)
    meta = jnp.zeros((e, LANE), F32)
    sb = min(GATHER_SLOT_BLOCK, cap)
    for s in range(cap // sb):
        first = jnp.min(jnp.where(sel & (pos >= float(s * sb)), t, float(seq)), axis=1, keepdims=True)
        last = jnp.max(jnp.where(sel & (pos < float((s + 1) * sb)), t, -1.0), axis=1, keepdims=True)
        meta = jnp.where(lane == META_TSTART + s, first, meta)
        meta = jnp.where(lane == META_TEND + s, last, meta)
    tt = min(COMBINE_TILE, seq)
    for i in range(seq // tt):
        before = jnp.sum(jnp.where(sel & (t < float(i * tt)), 1.0, 0.0), axis=1, keepdims=True)
        upto = jnp.sum(jnp.where(sel & (t < float((i + 1) * tt)), 1.0, 0.0), axis=1, keepdims=True)
        meta = jnp.where(lane == META_PSTART + i, before, meta)
        meta = jnp.where(lane == META_PEND + i, upto, meta)
    meta_ref[0] = meta.astype(I32)


def _select(logits_t, cap):
    b, e, seq = logits_t.shape
    assert cap // min(GATHER_SLOT_BLOCK, cap) <= 4 and seq // min(COMBINE_TILE, seq) <= 4
    spec = pl.BlockSpec((1, e, seq), lambda bi: (bi, 0, 0))
    mspec = pl.BlockSpec((1, e, LANE), lambda bi: (bi, 0, 0))
    return pl.pallas_call(
        functools.partial(_select_kernel, cap=cap),
        grid=(b,),
        in_specs=[spec],
        out_specs=[spec, spec, mspec],
        out_shape=[jax.ShapeDtypeStruct((b, e, seq), I32), jax.ShapeDtypeStruct((b, e, seq), F32),
                   jax.ShapeDtypeStruct((b, e, LANE), I32)],
        compiler_params=_cparams(("parallel",), 32),
        name="select",
    )(logits_t)


def _align_down(v, align):
    shift = align.bit_length() - 1
    return lax.shift_left(lax.shift_right_logical(v, shift), shift)


def _gather_kernel(meta_ref, h_ref, slot_ref, aff_ref, xg_ref, gate_ref, *, cap, sb, win):
    seq = h_ref.shape[1]
    experts = slot_ref.shape[1]
    base = (pl.program_id(0) * N_EXPERTS + pl.program_id(1)) * META_W

    n_blocks = cap // sb

    def gather_blocks(starts, width, k=0):
        hits = []
        for s in range(n_blocks):
            j = lax.broadcasted_iota(I32, (sb, width), 0) + s * sb
            hits.append(slot_ref[0, k, :, pl.ds(starts[s], width)] == j)
        for s in range(n_blocks):
            tok = pl.ds(starts[s], width)
            rows = slice(s * sb, (s + 1) * sb)
            onehot = jnp.where(hits[s], 1.0, 0.0).astype(BF16)
            xg_ref[k, 0, rows, :] = jnp.dot(onehot, h_ref[0, tok, :], preferred_element_type=F32).astype(BF16)
            gate_ref[k, 0, rows, :] = jnp.sum(jnp.where(hits[s], aff_ref[0, k, :, tok], 0.0), axis=1, keepdims=True)

    if win >= seq:
        for k in range(experts):
            gather_blocks([0] * n_blocks, seq, k)
        return
    assert experts == 1
    starts = []
    fits = None
    for s in range(n_blocks):
        first = meta_ref[base + META_TSTART + s]
        last = meta_ref[base + META_TEND + s]
        start = pl.multiple_of(jnp.minimum(_align_down(first, GATHER_ALIGN), seq - win), GATHER_ALIGN)
        starts.append(start)
        ok = last < start + win
        fits = ok if fits is None else jnp.logical_and(fits, ok)

    @pl.when(fits)
    def _():
        gather_blocks(starts, win)

    @pl.when(jnp.logical_not(fits))
    def _():
        gather_blocks([0] * n_blocks, seq)


def _gather(meta, h2, slots4, aff4, cap):
    b, seq, _ = h2.shape
    e = slots4.shape[1]
    sb = min(GATHER_SLOT_BLOCK, cap)
    win = GATHER_WIN if seq > GATHER_WIN else seq
    assert (seq - win) % GATHER_ALIGN == 0
    eg = e if win >= seq else 1
    row = pl.BlockSpec((1, eg, 1, seq), lambda bi, ei, m: (bi, ei, 0, 0))
    return pl.pallas_call(
        functools.partial(_gather_kernel, cap=cap, sb=sb, win=win),
        grid_spec=pltpu.PrefetchScalarGridSpec(
            num_scalar_prefetch=1,
            grid=(b, e // eg),
            in_specs=[pl.BlockSpec((1, seq, D_MODEL), lambda bi, ei, m: (bi, 0, 0)), row, row],
            out_specs=[
                pl.BlockSpec((eg, 1, cap, D_MODEL), lambda bi, ei, m: (ei, bi, 0, 0)),
                pl.BlockSpec((eg, 1, cap, 1), lambda bi, ei, m: (ei, bi, 0, 0)),
            ],
        ),
        out_shape=[
            jax.ShapeDtypeStruct((e, b, cap, D_MODEL), BF16),
            jax.ShapeDtypeStruct((e, b, cap, 1), F32),
        ],
        compiler_params=_cparams(("parallel", "arbitrary"), 48),
        name="gather",
    )(meta, h2, slots4, aff4)


FF_CHUNK = 256
FFN_ROW_TILES = 2


def _ffn_kernel(*refs, with_ctx):
    if with_ctx:
        x_ref, gate_ref, xc_ref, gatec_ref, wg_ref, wu_ref, wd_ref, o_ref, oc_ref, acc_ref, accc_ref = refs
    else:
        x_ref, gate_ref, wg_ref, wu_ref, wd_ref, o_ref, acc_ref = refs
    fc = pl.program_id(2)
    last_fc = pl.num_programs(2) - 1

    def swiglu_chunk(streams):
        @pl.when(fc == 0)
        def _():
            for _, _, _, acc in streams:
                acc[...] = jnp.zeros_like(acc)

        wg = wg_ref[0, 0].astype(BF16)
        wu = wu_ref[0, 0].astype(BF16)
        wd = wd_ref[0, 0].astype(BF16)
        xs = [x[0] for x, _, _, _ in streams]
        gates_ = [jnp.dot(x, wg, preferred_element_type=F32) for x in xs]
        ups = [jnp.dot(x, wu, preferred_element_type=F32) for x in xs]
        hmids = [(a * jax.nn.sigmoid(a) * u).astype(BF16) for a, u in zip(gates_, ups)]
        for (_, _, _, acc), hmid in zip(streams, hmids):
            acc[...] += jnp.dot(hmid, wd, preferred_element_type=F32)

        @pl.when(fc == last_fc)
        def _():
            for _, gate, o, acc in streams:
                o[0] = (acc[...] * gate[0]).astype(BF16)

    main = (x_ref, gate_ref, o_ref, acc_ref)
    if not with_ctx:
        swiglu_chunk([main])
        return
    on_last_tile = pl.program_id(1) == pl.num_programs(1) - 1

    @pl.when(jnp.logical_not(on_last_tile))
    def _():
        swiglu_chunk([main])

    @pl.when(on_last_tile)
    def _():
        swiglu_chunk([main, (xc_ref, gatec_ref, oc_ref, accc_ref)])


def _ffn(xg3, gate3, xc3, gatec3, wg_all, wu_all, wd_all, layer):
    e, m, _ = xg3.shape
    tm = m // FFN_ROW_TILES
    assert m % tm == 0 and tm % 16 == 0
    with_ctx = xc3 is not None
    row = lambda w: pl.BlockSpec((1, tm, w), lambda ei, i, fc: (ei, i, 0))
    in_specs = [row(D_MODEL), row(1)]
    out_specs = [row(D_MODEL)]
    out_shape = [jax.ShapeDtypeStruct((e, m, D_MODEL), BF16)]
    scratch = [pltpu.VMEM((tm, D_MODEL), F32)]
    args = [xg3, gate3]
    if with_ctx:
        mc = xc3.shape[1]
        crow = lambda w: pl.BlockSpec((1, mc, w), lambda ei, i, fc: (ei, 0, 0))
        in_specs += [crow(D_MODEL), crow(1)]
        out_specs.append(crow(D_MODEL))
        out_shape.append(jax.ShapeDtypeStruct((e, mc, D_MODEL), BF16))
        scratch.append(pltpu.VMEM((mc, D_MODEL), F32))
        args += [xc3, gatec3]
    in_specs += [
        pl.BlockSpec((1, 1, D_MODEL, FF_CHUNK), lambda ei, i, fc: (layer, ei, 0, fc)),
        pl.BlockSpec((1, 1, D_MODEL, FF_CHUNK), lambda ei, i, fc: (layer, ei, 0, fc)),
        pl.BlockSpec((1, 1, FF_CHUNK, D_MODEL), lambda ei, i, fc: (layer, ei, fc, 0)),
    ]
    res = pl.pallas_call(
        functools.partial(_ffn_kernel, with_ctx=with_ctx),
        grid=(e, m // tm, EXPERT_FF // FF_CHUNK),
        in_specs=in_specs,
        out_specs=out_specs,
        out_shape=out_shape,
        scratch_shapes=scratch,
        compiler_params=_cparams(("parallel", "arbitrary", "arbitrary"), 56),
        name="ffn",
    )(*args, wg_all, wu_all, wd_all)
    return (res[0], res[1]) if with_ctx else (res[0], None)


COMBINE_EXPERTS = 4


def _combine_kernel(meta_ref, slot_ref, gy_ref, x_ref, g2_ref, gf_ref, o_ref, acc_ref, *, cap, win, final):
    bi = pl.program_id(0)
    ti = pl.program_id(1)
    ec = pl.program_id(2)
    tt = x_ref.shape[1]
    tn = (((0,), (0,)), ((), ()))

    @pl.when(ec == 0)
    def _():
        acc_ref[...] = jnp.zeros_like(acc_ref)

    def scatter(starts, width):
        j = lax.broadcasted_iota(I32, (width, tt), 0)
        onehot = jnp.concatenate(
            [jnp.where(slot_ref[0, k] == j + starts[k], 1.0, 0.0).astype(BF16) for k in range(COMBINE_EXPERTS)],
            axis=0)
        gy = jnp.concatenate([gy_ref[k, pl.ds(starts[k], width), :] for k in range(COMBINE_EXPERTS)], axis=0)
        acc_ref[...] += lax.dot_general(onehot, gy, tn, preferred_element_type=F32)

    if win >= cap:
        scatter([0] * COMBINE_EXPERTS, cap)
    else:
        starts = []
        fits = None
        for k in range(COMBINE_EXPERTS):
            base = (bi * N_EXPERTS + ec * COMBINE_EXPERTS + k) * META_W
            before = meta_ref[base + META_PSTART + ti]
            upto = meta_ref[base + META_PEND + ti]
            start = pl.multiple_of(jnp.minimum(_align_down(before, COMBINE_ALIGN), cap - win), COMBINE_ALIGN)
            starts.append(start)
            ok = upto <= start + win
            fits = ok if fits is None else jnp.logical_and(fits, ok)

        @pl.when(fits)
        def _():
            scatter(starts, win)

        @pl.when(jnp.logical_not(fits))
        def _():
            scatter([0] * COMBINE_EXPERTS, cap)

    @pl.when(ec == pl.num_programs(2) - 1)
    def _():
        xn = x_ref[0] + g2_ref[0] * acc_ref[...]
        if final:
            ms = jnp.mean(xn * xn, axis=-1, keepdims=True)
            xn = (xn * lax.rsqrt(ms + RMS_EPS)) * gf_ref[...]
        o_ref[0] = xn


def _combine(meta, slots4, gy3, row_off, x3, mod3, gfinal, *, cap, ctx_row, final):
    b, seq, _ = x3.shape
    e = slots4.shape[1]
    tt = min(seq, COMBINE_TILE)
    win = COMBINE_WIN if cap > COMBINE_WIN else cap
    assert row_off % cap == 0 and (cap - win) % COMBINE_ALIGN == 0
    blk_off = row_off // cap
    mrow = (lambda bi: bi) if ctx_row is None else (lambda bi: ctx_row)
    return pl.pallas_call(
        functools.partial(_combine_kernel, cap=cap, win=win, final=final),
        grid_spec=pltpu.PrefetchScalarGridSpec(
            num_scalar_prefetch=1,
            grid=(b, seq // tt, e // COMBINE_EXPERTS),
            in_specs=[
                pl.BlockSpec((1, COMBINE_EXPERTS, 1, tt), lambda bi, i, ec, m: (bi, ec, 0, i)),
                pl.BlockSpec((COMBINE_EXPERTS, cap, D_MODEL), lambda bi, i, ec, m: (ec, blk_off + bi, 0)),
                pl.BlockSpec((1, tt, D_MODEL), lambda bi, i, ec, m: (bi, i, 0)),
                pl.BlockSpec((1, 1, D_MODEL), lambda bi, i, ec, m: (mrow(bi), 0, 5)),
                pl.BlockSpec((1, D_MODEL), lambda bi, i, ec, m: (0, 0)),
            ],
            out_specs=pl.BlockSpec((1, tt, D_MODEL), lambda bi, i, ec, m: (bi, i, 0)),
            scratch_shapes=[pltpu.VMEM((tt, D_MODEL), F32)],
        ),
        out_shape=jax.ShapeDtypeStruct((b, seq, D_MODEL), F32),
        compiler_params=_cparams(("parallel", "parallel", "arbitrary"), 48),
        name="combine",
    )(meta, slots4, gy3, x3, mod3, gfinal)


def _moe_route(h2, logits_t):
    b, seq, _ = h2.shape
    cap = EC_CAPACITY_FACTOR * seq // N_EXPERTS
    slots, aff, meta = _select(logits_t, cap)
    meta = meta[:, :, :META_W].reshape(-1)
    slots4 = slots.reshape(b, N_EXPERTS, 1, seq)
    aff4 = aff.reshape(b, N_EXPERTS, 1, seq)
    xg, gate = _gather(meta, h2, slots4, aff4, cap)
    return meta, slots4, xg.reshape(N_EXPERTS, b * cap, D_MODEL), gate.reshape(N_EXPERTS, b * cap, 1), cap


def kernel(x, c, ctx, c_ctx, ada_w, ada_b, norm1_g, norm2_g, w_in, rpb, w_att_o, w_fourier, w_pool,
           pool_scale, w_out, w_router, w_exp_gate, w_exp_up, w_exp_down, final_norm_g):
    b, seq, d = x.shape
    lc = ctx.shape[1]
    assert d == D_MODEL and seq % (GRID_W * ATT_QROWS) == 0 and b + 1 <= MOD_ROWS
    rows = seq // GRID_W
    ctx_row = b

    cond = jnp.concatenate([c, c_ctx[None, :], jnp.zeros((MOD_ROWS - b - 1, d), F32)], axis=0)
    cl, sl = _dft_mats(seq, seq // 2)
    clc, slc = _dft_mats(lc, lc // 2)
    cc, sc = _dft_mats(FOURIER_GROUP_W, FOURIER_GROUP_W)
    gfinal = final_norm_g.reshape(1, d)

    for i in range(DEPTH):
        update_ctx = i < DEPTH - 1
        mod3 = _adaln(cond, ada_w[i], ada_b[i]).reshape(MOD_ROWS, 1, 6 * d)
        g1n = norm1_g[i].reshape(1, d)
        g2n = norm2_g[i].reshape(1, d)
        w_in_b = w_in[i].astype(BF16)
        wao = w_att_o[i].astype(BF16)
        wf = w_fourier[i].astype(BF16)
        wp = w_pool[i].astype(BF16)
        ps = pool_scale[i].reshape(1, d)
        wo = w_out[i].astype(BF16)
        wr_t = jnp.pad(jnp.concatenate([w_router[i], w_router[i]], axis=1), ((0, 0), (0, LANE - 2 * N_EXPERTS)))
        bias = _bias_table(rpb[i], rows)

        pb, pp = _modproj(x.reshape(b * seq, d), mod3, g1n, w_in_b, seq=seq, ctx_row=None,
                          n_out=IN_W, with_pool=True)
        n_ctx = IN_W if update_ctx else Q_OFF
        pcb, pcp = _modproj(ctx.reshape(b * lc, d), mod3, g1n, w_in_b[:, :n_ctx], seq=lc, ctx_row=ctx_row,
                            n_out=n_ctx, with_pool=update_ctx)
        pb3 = pb.reshape(b, seq, IN_W)
        pcb3 = pcb.reshape(b, lc, n_ctx)

        att = _nattn(pb3, pcb3, bias)
        four = _fourier(pb3, cc, sc, cl, sl)
        pooled = _pool(pp.reshape(b, seq, POOL_W))
        x, h2, logits_t = _merge(att, four, pooled, pb3, x, mod3, wao, wf, wp, ps, wo, g2n, wr_t, ctx_row=None)
        meta_x, slots_x, xg, gate, cap = _moe_route(h2, logits_t)

        if update_ctx:
            att_c = _cattn(pcb3)
            four_c = _fourier(pcb3, cc, sc, clc, slc)
            pooled_c = _pool(pcp.reshape(b, lc, POOL_W))
            ctx, h2_c, logits_c = _merge(att_c, four_c, pooled_c, pcb3, ctx, mod3, wao, wf, wp, ps, wo, g2n, wr_t,
                                         ctx_row=ctx_row)
            meta_c, slots_c, xg_c, gate_c, cap_c = _moe_route(h2_c, logits_c)
        else:
            xg_c = gate_c = None

        gy, gy_c = _ffn(xg, gate, xg_c, gate_c, w_exp_gate, w_exp_up, w_exp_down, i)
        if update_ctx:
            ctx = _combine(meta_c, slots_c, gy_c, 0, ctx, mod3, gfinal, cap=cap_c, ctx_row=ctx_row, final=False)
        x = _combine(meta_x, slots_x, gy, 0, x, mod3, gfinal, cap=cap, ctx_row=None, final=not update_ctx)
    return x
```
